```python
import math
import jax
import jax.numpy as jnp
from jax import lax
import numpy as np

D_MODEL = 2048
BATCH = 2
SEQ = 16384
DEPTH = 2
DEC_BATCH = 16
DEC_SEQ = 64
PAST_LEN = 4096

CHUNK = 64
EPS = 1e-6
CONV_W = 4
GDN_HEADS = 8
GDN_DK = 128
GDN_DV = 128
GDN_QK_W = GDN_HEADS * GDN_DK
GDN_V_W = GDN_HEADS * GDN_DV
GDN_CONV_CH = 2 * GDN_QK_W + GDN_V_W
LRU_WIDTH = 1024
LRU_BLOCKS = 8
LRU_BLOCK_W = LRU_WIDTH // LRU_BLOCKS
LRU_C = 8.0
IN0_W = GDN_CONV_CH + GDN_V_W + 2 * GDN_HEADS + 2 * LRU_WIDTH
MIX0_W = GDN_V_W + LRU_WIDTH
SWA_Q_HEADS = 16
SWA_KV_HEADS = 4
SWA_HD = 64
SWA_GROUP = SWA_Q_HEADS // SWA_KV_HEADS
SWA_Q_W = SWA_Q_HEADS * SWA_HD
SWA_KV_W = SWA_KV_HEADS * SWA_HD
WINDOW = 128
ROPE_THETA = 10000.0
SMLP_GROUPS = 8
SMLP_GROUP_W = 128
SMLP_W = SMLP_GROUPS * SMLP_GROUP_W
SMLP_CHUNK = 128
IN1_W = SWA_Q_W + 2 * SWA_KV_W + 2 * SMLP_W
MIX1_W = SWA_Q_W + SMLP_W
MEM_LEN = 256
MEM_HEADS = 4
MEM_HD = 128
MEM_W = MEM_HEADS * MEM_HD
FF_DENSE = 5632
N_EXPERTS = 8
TOP_K = 2
FF_EXPERT = 2816
MOE_BLOCK = 256
N_NORMS = 6

kernel_name = 'hybrid_streaming_encoder_step'


def split_cols(a, widths):
    idx = [int(i) for i in np.cumsum(widths)[:-1]]
    return jnp.split(a, idx, axis=-1)


def rms_norm(x, g):
    x32 = x.astype(jnp.float32)
    y = x32 * lax.rsqrt(jnp.mean(x32 * x32, axis=-1, keepdims=True) + EPS)
    return (y * g.astype(jnp.float32)).astype(x.dtype)


def layer_norm(x, g, b):
    x32 = x.astype(jnp.float32)
    mu = jnp.mean(x32, axis=-1, keepdims=True)
    xc = x32 - mu
    y = xc * lax.rsqrt(jnp.mean(xc * xc, axis=-1, keepdims=True) + EPS)
    return (y * g.astype(jnp.float32) + b.astype(jnp.float32)).astype(x.dtype)


def l2_normalize(x):
    x32 = x.astype(jnp.float32)
    return x32 * lax.rsqrt(jnp.sum(x32 * x32, axis=-1, keepdims=True) + EPS)


def causal_conv(x, w, prev):
    T = x.shape[1]
    xp = jnp.concatenate([prev.astype(x.dtype), x], axis=1)
    y = sum(xp[:, j:j + T] * w[j] for j in range(CONV_W))
    return y, xp[:, -(CONV_W - 1):]


def rope(x, pos):
    half = x.shape[-1] // 2
    inv_freq = jnp.exp(-math.log(ROPE_THETA) * jnp.arange(half, dtype=jnp.float32) / half)
    ang = pos.astype(jnp.float32)[:, None] * inv_freq[None, :]
    cos = jnp.cos(ang)[None, :, None, :]
    sin = jnp.sin(ang)[None, :, None, :]
    x32 = x.astype(jnp.float32)
    x1, x2 = x32[..., :half], x32[..., half:]
    return jnp.concatenate([x1 * cos - x2 * sin, x2 * cos + x1 * sin], axis=-1).astype(x.dtype)


def gated_delta_rule(q, k, v, g, beta, s0, chunk):
    B, T, H, DK = q.shape
    DV = v.shape[-1]
    n = T // chunk
    f32 = jnp.float32

    def blocks(a):
        a = a.astype(f32).reshape((B, n, chunk, H) + a.shape[3:])
        return jnp.moveaxis(jnp.swapaxes(a, 2, 3), 1, 0)

    qc, kc, vc, gc, bc = blocks(q), blocks(k), blocks(v), blocks(g), blocks(beta)
    G = jnp.cumsum(gc, axis=-1)
    i = jnp.arange(chunk)
    causal = i[:, None] >= i[None, :]
    strict = i[:, None] > i[None, :]
    decay = jnp.exp(jnp.where(causal, G[..., :, None] - G[..., None, :], -jnp.inf))
    kb = kc * bc[..., None]
    a_low = jnp.where(strict, jnp.einsum('nbhid,nbhjd->nbhij', kb, kc) * decay, 0.0)
    rhs = jnp.concatenate([vc * bc[..., None], kb * jnp.exp(G)[..., None]], axis=-1)
    sol = lax.linalg.triangular_solve(a_low, rhs, left_side=True, lower=True, unit_diagonal=True)
    u, w = sol[..., :DV], sol[..., DV:]
    qk = jnp.einsum('nbhid,nbhjd->nbhij', qc, kc) * decay
    qg = qc * jnp.exp(G)[..., None]
    kdec = kc * jnp.exp(G[..., -1:] - G)[..., None]
    g_last = jnp.exp(G[..., -1])

    def step(S, blk):
        u_c, w_c, qk_c, qg_c, kd_c, gl_c = blk
        v_new = u_c - jnp.einsum('bhck,bhkv->bhcv', w_c, S)
        o = jnp.einsum('bhck,bhkv->bhcv', qg_c, S) + jnp.einsum('bhij,bhjv->bhiv', qk_c, v_new)
        S = S * gl_c[..., None, None] + jnp.einsum('bhck,bhcv->bhkv', kd_c, v_new)
        return S, o

    s_final, o = lax.scan(step, s0.astype(f32), (u, w, qk, qg, kdec, g_last))
    o = jnp.swapaxes(jnp.moveaxis(o, 0, 1), 2, 3).reshape(B, T, H, DV)
    return o, s_final


def linear_recurrence(a, b, h0):
    b = b.at[:, 0].add(a[:, 0] * h0.astype(jnp.float32))

    def combine(l, r):
        return l[0] * r[0], r[0] * l[1] + r[1]

    _, h = lax.associative_scan(combine, (a, b), axis=1)
    return h, h[:, -1]


def banded_sink_attention(q, k_full, v_full, key_valid, sinks, lc):
    B, T, HQ, HD = q.shape
    P = k_full.shape[1] - T
    n = T // lc
    nk = P + lc
    idx = jnp.arange(n)[:, None] * lc + jnp.arange(nk)[None, :]
    kb = k_full[:, idx].astype(jnp.float32)
    vb = v_full[:, idx].astype(jnp.float32)
    valid = key_valid[idx]
    qb = q.reshape(B, n, lc, SWA_KV_HEADS, SWA_GROUP, HD).astype(jnp.float32)
    s = jnp.einsum('bnqhgd,bnkhd->bnhgqk', qb, kb) * (HD ** -0.5)
    s = jnp.where(valid[None, :, None, None, None, :], s, -jnp.inf)
    sink = sinks.astype(jnp.float32).reshape(SWA_KV_HEADS, SWA_GROUP)[None, None, :, :, None, None]
    m = jnp.maximum(jnp.max(s, axis=-1, keepdims=True), sink)
    p = jnp.exp(s - m)
    p = p / (jnp.sum(p, axis=-1, keepdims=True) + jnp.exp(sink - m))
    o = jnp.einsum('bnhgqk,bnkhd->bnqhgd', p, vb)
    return o.reshape(B, T, HQ * HD)


def spatial_gate(vg, w_spatial, b_spatial):
    B, T, _ = vg.shape
    lc = min(SMLP_CHUNK, T)
    n = T // lc
    vb = vg.reshape(B, n, lc, SMLP_GROUPS, SMLP_GROUP_W)
    w = jnp.tril(w_spatial[:, :lc, :lc])
    s = jnp.einsum('gij,bnjgc->bnigc', w, vb) + jnp.transpose(b_spatial[:, :lc])[None, None, :, :, None]
    return s.reshape(B, T, SMLP_W)


def mix_even(xn, gdn_conv_prev, gdn_s0, lru_conv_prev, lru_h0, w_in0, gdn_conv_w, gdn_a_log, gdn_dt_bias,
             gdn_norm_g, lru_conv_w, lru_conv_b, lru_w_a, lru_b_a, lru_w_x, lru_b_x, lru_lambda, w_out0):
    B, T, _ = xn.shape
    f32 = jnp.float32
    qkv, z, beta_in, decay_in, lru_in, lru_gate = split_cols(
        xn @ w_in0, (GDN_CONV_CH, GDN_V_W, GDN_HEADS, GDN_HEADS, LRU_WIDTH, LRU_WIDTH))
    qkv, gdn_conv_new = causal_conv(qkv, gdn_conv_w, gdn_conv_prev)
    q, k, v = split_cols(jax.nn.silu(qkv), (GDN_QK_W, GDN_QK_W, GDN_V_W))
    q = l2_normalize(q.reshape(B, T, GDN_HEADS, GDN_DK)) * (GDN_DK ** -0.5)
    k = l2_normalize(k.reshape(B, T, GDN_HEADS, GDN_DK))
    v = v.reshape(B, T, GDN_HEADS, GDN_DV)
    beta = jax.nn.sigmoid(beta_in.astype(f32))
    g = -jnp.exp(gdn_a_log.astype(f32)) * jax.nn.softplus(decay_in.astype(f32) + gdn_dt_bias.astype(f32))
    o, gdn_s_new = gated_delta_rule(q, k, v, g, beta, gdn_s0, min(CHUNK, T))
    o = rms_norm(o.astype(xn.dtype), gdn_norm_g) * jax.nn.silu(z.reshape(B, T, GDN_HEADS, GDN_DV))
    xr, lru_conv_new = causal_conv(lru_in, lru_conv_w, lru_conv_prev)
    xr = xr + lru_conv_b
    xb = xr.reshape(B, T, LRU_BLOCKS, LRU_BLOCK_W)
    gate_a = jax.nn.sigmoid((jnp.einsum('btnc,ncd->btnd', xb, lru_w_a).reshape(B, T, LRU_WIDTH) + lru_b_a).astype(f32))
    gate_x = jax.nn.sigmoid((jnp.einsum('btnc,ncd->btnd', xb, lru_w_x).reshape(B, T, LRU_WIDTH) + lru_b_x).astype(f32))
    log_a = -LRU_C * gate_a * jax.nn.softplus(-lru_lambda.astype(f32))
    a = jnp.exp(log_a)
    b_in = jnp.sqrt(-jnp.expm1(2.0 * log_a)) * gate_x * xr.astype(f32)
    h, lru_h_new = linear_recurrence(a, b_in, lru_h0)
    y_lru = h.astype(xn.dtype) * jax.nn.gelu(lru_gate)
    out = jnp.concatenate([o.reshape(B, T, GDN_V_W), y_lru], axis=-1) @ w_out0
    return out, gdn_conv_new, gdn_s_new, lru_conv_new, lru_h_new


def mix_odd(xn, start, swa_k_prev, swa_v_prev, w_in1, swa_sinks, smlp_ln_g, smlp_ln_b, w_spatial, b_spatial, w_out1):
    B, T, _ = xn.shape
    q, k, v, u, vg = split_cols(xn @ w_in1, (SWA_Q_W, SWA_KV_W, SWA_KV_W, SMLP_W, SMLP_W))
    pos = start + jnp.arange(T)
    q = rope(q.reshape(B, T, SWA_Q_HEADS, SWA_HD), pos)
    k = rope(k.reshape(B, T, SWA_KV_HEADS, SWA_HD), pos)
    v = v.reshape(B, T, SWA_KV_HEADS, SWA_HD)
    n_prev = swa_k_prev.shape[1]
    k_full = jnp.concatenate([swa_k_prev.astype(k.dtype), k], axis=1)
    v_full = jnp.concatenate([swa_v_prev.astype(v.dtype), v], axis=1)
    key_valid = (start - n_prev + jnp.arange(n_prev + T)) >= 0
    attn = banded_sink_attention(q, k_full, v_full, key_valid, swa_sinks, min(CHUNK, T)).astype(xn.dtype)
    u = jax.nn.gelu(u)
    vg = layer_norm(jax.nn.gelu(vg), smlp_ln_g, smlp_ln_b)
    y_smlp = u * spatial_gate(vg, w_spatial, b_spatial).astype(xn.dtype)
    out = jnp.concatenate([attn, y_smlp], axis=-1) @ w_out1
    return out, k, v, vg


def memory_proj(mem, g, w):
    B, M, _ = mem.shape
    return (rms_norm(mem, g) @ w).reshape(B, M, MEM_HEADS, MEM_HD)


def cross_attention(hn, mem_k, mem_v, w_xq, w_xo):
    B, T, _ = hn.shape
    q = (hn @ w_xq).reshape(B, T, MEM_HEADS, MEM_HD).astype(jnp.float32)
    s = jnp.einsum('bqhd,bkhd->bhqk', q, mem_k.astype(jnp.float32)) * (MEM_HD ** -0.5)
    p = jax.nn.softmax(s, axis=-1)
    o = jnp.einsum('bhqk,bkhd->bqhd', p, mem_v.astype(jnp.float32)).reshape(B, T, MEM_W)
    return o.astype(hn.dtype) @ w_xo


def dense_swiglu(x, w_gate, w_up, w_down):
    return (jax.nn.silu(x @ w_gate) * (x @ w_up)) @ w_down


def moe_swiglu(x, w_router, w_gate, w_up, w_down):
    B, T, D = x.shape
    xf = x.reshape(B * T, D)
    n_tok = B * T
    logits = (xf @ w_router).astype(jnp.float32)
    top_val, top_idx = lax.top_k(logits, TOP_K)
    gate = jax.nn.softmax(top_val, axis=-1)
    n_assign = n_tok * TOP_K
    flat_e = top_idx.reshape(-1).astype(jnp.int32)
    flat_tok = jnp.arange(n_assign, dtype=jnp.int32) // TOP_K
    flat_g = gate.reshape(-1)
    order = jnp.argsort(flat_e)
    se, stok, sg = flat_e[order], flat_tok[order], flat_g[order]
    counts = jnp.bincount(flat_e, length=N_EXPERTS).astype(jnp.int32)
    padded = (counts + MOE_BLOCK - 1) // MOE_BLOCK * MOE_BLOCK
    pad_end = jnp.cumsum(padded)
    pad_start = pad_end - padded
    grp_start = jnp.cumsum(counts) - counts
    dest = pad_start[se] + jnp.arange(n_assign, dtype=jnp.int32) - grp_start[se]
    n_blocks = -(-n_assign // MOE_BLOCK) + N_EXPERTS
    n_rows = n_blocks * MOE_BLOCK
    row_tok = jnp.zeros((n_rows,), jnp.int32).at[dest].set(stok)
    row_g = jnp.zeros((n_rows,), jnp.float32).at[dest].set(sg)
    blk_e = jnp.minimum(jnp.searchsorted(pad_end, jnp.arange(n_blocks, dtype=jnp.int32) * MOE_BLOCK, side='right'),
                        N_EXPERTS - 1)

    def step(y, blk):
        tok, g, e = blk
        xe = xf[tok]
        h = jax.nn.silu(xe @ w_gate[e]) * (xe @ w_up[e])
        return y.at[tok].add((h @ w_down[e]) * g[:, None].astype(x.dtype)), None

    y, _ = lax.scan(step, jnp.zeros_like(xf),
                    (row_tok.reshape(n_blocks, MOE_BLOCK), row_g.reshape(n_blocks, MOE_BLOCK), blk_e))
    return y.reshape(B, T, D)


def forward(x, start, mem_k, mem_v, gdn_conv_prev, gdn_s0, lru_conv_prev, lru_h0, swa_k_prev, swa_v_prev,
            norm_g, w_in0, gdn_conv_w, gdn_a_log, gdn_dt_bias, gdn_norm_g, lru_conv_w, lru_conv_b, lru_w_a, lru_b_a,
            lru_w_x, lru_b_x, lru_lambda, w_out0, w_in1, swa_sinks, smlp_ln_g, smlp_ln_b, w_spatial, b_spatial, w_out1,
            w_xq, w_xo, w_ff_gate, w_ff_up, w_ff_down, w_router, w_moe_gate, w_moe_up, w_moe_down):
    for layer in range(DEPTH):
        g = norm_g[layer]
        xn = rms_norm(x, g[0])
        if layer % 2 == 0:
            mix, gdn_conv_new, gdn_s_new, lru_conv_new, lru_h_new = mix_even(
                xn, gdn_conv_prev, gdn_s0, lru_conv_prev, lru_h0, w_in0, gdn_conv_w, gdn_a_log, gdn_dt_bias,
                gdn_norm_g, lru_conv_w, lru_conv_b, lru_w_a, lru_b_a, lru_w_x, lru_b_x, lru_lambda, w_out0)
        else:
            mix, swa_k_new, swa_v_new, smlp_v_new = mix_odd(
                xn, start, swa_k_prev, swa_v_prev, w_in1, swa_sinks, smlp_ln_g, smlp_ln_b, w_spatial, b_spatial,
                w_out1)
        x = x + rms_norm(mix, g[1])
        x = x + rms_norm(cross_attention(rms_norm(x, g[2]), mem_k[layer], mem_v[layer], w_xq[layer], w_xo[layer]),
                         g[3])
        hn = rms_norm(x, g[4])
        if layer % 2 == 0:
            ff = dense_swiglu(hn, w_ff_gate, w_ff_up, w_ff_down)
        else:
            ff = moe_swiglu(hn, w_router, w_moe_gate, w_moe_up, w_moe_down)
        x = x + rms_norm(ff, g[5])
    return x, gdn_conv_new, gdn_s_new, lru_conv_new, lru_h_new, swa_k_new, swa_v_new, smlp_v_new


def setup_inputs(seed: int = 0) -> dict:
    key = jax.random.key(seed)
    ks = iter(jax.random.split(key, 64))
    f32 = jnp.float32

    def nrm(shape, scale):
        return scale * jax.random.normal(next(ks), shape, f32)

    def unif(shape, lo, hi):
        return jax.random.uniform(next(ks), shape, f32, minval=lo, maxval=hi)

    D = D_MODEL
    swa_keep = min(WINDOW, PAST_LEN)
    dt = jnp.exp(unif((GDN_HEADS,), math.log(1e-3), math.log(1e-1)))
    lam_u = unif((LRU_WIDTH,), 0.9, 0.999)
    return {
        'x_prompt': nrm((BATCH, SEQ, D), 1.0),
        'x_sample': nrm((DEC_BATCH, DEC_SEQ, D), 1.0),
        'mem_prompt': nrm((BATCH, MEM_LEN, D), 1.0),
        'cache_mem_k': nrm((DEPTH, DEC_BATCH, MEM_LEN, MEM_HEADS, MEM_HD), 1.0),
        'cache_mem_v': nrm((DEPTH, DEC_BATCH, MEM_LEN, MEM_HEADS, MEM_HD), 1.0),
        'state_gdn': nrm((DEC_BATCH, GDN_HEADS, GDN_DK, GDN_DV), GDN_DK ** -0.5),
        'state_gdn_conv': nrm((DEC_BATCH, CONV_W - 1, GDN_CONV_CH), 1.0),
        'state_rglru_h': nrm((DEC_BATCH, LRU_WIDTH), 0.5),
        'state_rglru_conv': nrm((DEC_BATCH, CONV_W - 1, LRU_WIDTH), 1.0),
        'cache_swa_k': nrm((DEC_BATCH, swa_keep, SWA_KV_HEADS, SWA_HD), 1.0),
        'cache_swa_v': nrm((DEC_BATCH, swa_keep, SWA_KV_HEADS, SWA_HD), 1.0),
        'norm_g': 1.0 + nrm((DEPTH, N_NORMS, D), 0.05),
        'mem_norm_g': 1.0 + nrm((DEPTH, D), 0.05),
        'w_in0': nrm((D, IN0_W), D ** -0.5),
        'gdn_conv_w': nrm((CONV_W, GDN_CONV_CH), CONV_W ** -0.5),
        'gdn_a_log': jnp.log(unif((GDN_HEADS,), 1.0, 16.0)),
        'gdn_dt_bias': dt + jnp.log(-jnp.expm1(-dt)),
        'gdn_norm_g': 1.0 + nrm((GDN_DV,), 0.05),
        'lru_conv_w': nrm((CONV_W, LRU_WIDTH), CONV_W ** -0.5),
        'lru_conv_b': nrm((LRU_WIDTH,), 0.02),
        'lru_w_a': nrm((LRU_BLOCKS, LRU_BLOCK_W, LRU_BLOCK_W), LRU_BLOCK_W ** -0.5),
        'lru_b_a': nrm((LRU_WIDTH,), 0.02),
        'lru_w_x': nrm((LRU_BLOCKS, LRU_BLOCK_W, LRU_BLOCK_W), LRU_BLOCK_W ** -0.5),
        'lru_b_x': nrm((LRU_WIDTH,), 0.02),
        'lru_lambda': jnp.log(lam_u) - jnp.log1p(-lam_u),
        'w_out0': nrm((MIX0_W, D), MIX0_W ** -0.5),
        'w_in1': nrm((D, IN1_W), D ** -0.5),
        'swa_sinks': nrm((SWA_Q_HEADS,), 1.0),
        'smlp_ln_g': 1.0 + nrm((SMLP_W,), 0.05),
        'smlp_ln_b': nrm((SMLP_W,), 0.02),
        'w_spatial': nrm((SMLP_GROUPS, SMLP_CHUNK, SMLP_CHUNK), SMLP_CHUNK ** -0.5),
        'b_spatial': 1.0 + nrm((SMLP_GROUPS, SMLP_CHUNK), 0.05),
        'w_out1': nrm((MIX1_W, D), MIX1_W ** -0.5),
        'w_xq': nrm((DEPTH, D, MEM_W), D ** -0.5),
        'w_xk': nrm((DEPTH, D, MEM_W), D ** -0.5),
        'w_xv': nrm((DEPTH, D, MEM_W), D ** -0.5),
        'w_xo': nrm((DEPTH, MEM_W, D), MEM_W ** -0.5),
        'w_ff_gate': nrm((D, FF_DENSE), D ** -0.5),
        'w_ff_up': nrm((D, FF_DENSE), D ** -0.5),
        'w_ff_down': nrm((FF_DENSE, D), FF_DENSE ** -0.5),
        'w_router': nrm((D, N_EXPERTS), D ** -0.5),
        'w_moe_gate': nrm((N_EXPERTS, D, FF_EXPERT), D ** -0.5),
        'w_moe_up': nrm((N_EXPERTS, D, FF_EXPERT), D ** -0.5),
        'w_moe_down': nrm((N_EXPERTS, FF_EXPERT, D), FF_EXPERT ** -0.5),
    }


def reference(x_prompt, x_sample, mem_prompt, cache_mem_k, cache_mem_v, state_gdn, state_gdn_conv, state_rglru_h,
              state_rglru_conv, cache_swa_k, cache_swa_v, norm_g, mem_norm_g, w_in0, gdn_conv_w, gdn_a_log,
              gdn_dt_bias, gdn_norm_g, lru_conv_w, lru_conv_b, lru_w_a, lru_b_a, lru_w_x, lru_b_x, lru_lambda, w_out0,
              w_in1, swa_sinks, smlp_ln_g, smlp_ln_b, w_spatial, b_spatial, w_out1, w_xq, w_xk, w_xv, w_xo,
              w_ff_gate, w_ff_up, w_ff_down, w_router, w_moe_gate, w_moe_up, w_moe_down):
    weights = (norm_g, w_in0, gdn_conv_w, gdn_a_log, gdn_dt_bias, gdn_norm_g, lru_conv_w, lru_conv_b, lru_w_a,
               lru_b_a, lru_w_x, lru_b_x, lru_lambda, w_out0, w_in1, swa_sinks, smlp_ln_g, smlp_ln_b, w_spatial,
               b_spatial, w_out1, w_xq, w_xo, w_ff_gate, w_ff_up, w_ff_down, w_router, w_moe_gate, w_moe_up,
               w_moe_down)
    B, T = x_prompt.shape[0], x_prompt.shape[1]
    dt = x_prompt.dtype
    mem_k_p = jnp.stack([memory_proj(mem_prompt, mem_norm_g[l], w_xk[l]) for l in range(DEPTH)])
    mem_v_p = jnp.stack([memory_proj(mem_prompt, mem_norm_g[l], w_xv[l]) for l in range(DEPTH)])
    (y_prompt, gdn_conv_p, gdn_s_p, lru_conv_p, lru_h_p, swa_k_rows, swa_v_rows, _) = forward(
        x_prompt, 0, mem_k_p, mem_v_p,
        jnp.zeros((B, CONV_W - 1, GDN_CONV_CH), dt),
        jnp.zeros((B, GDN_HEADS, GDN_DK, GDN_DV), jnp.float32),
        jnp.zeros((B, CONV_W - 1, LRU_WIDTH), dt),
        jnp.zeros((B, LRU_WIDTH), jnp.float32),
        jnp.zeros((B, WINDOW, SWA_KV_HEADS, SWA_HD), dt),
        jnp.zeros((B, WINDOW, SWA_KV_HEADS, SWA_HD), dt),
        *weights)
    keep = min(WINDOW, T)
    swa_k_p = swa_k_rows[:, T - keep:]
    swa_v_p = swa_v_rows[:, T - keep:]
    (y_sample, gdn_conv_s, gdn_s_s, lru_conv_s, lru_h_s, swa_k_s, swa_v_s, smlp_v_s) = forward(
        x_sample, PAST_LEN, cache_mem_k, cache_mem_v, state_gdn_conv, state_gdn, state_rglru_conv, state_rglru_h,
        cache_swa_k, cache_swa_v, *weights)
    return (y_prompt, y_sample, mem_k_p, mem_v_p, gdn_s_p, gdn_conv_p, lru_h_p, lru_conv_p, swa_k_p, swa_v_p,
            gdn_s_s, gdn_conv_s, lru_h_s, lru_conv_s, swa_k_s, swa_v_s, smlp_v_s)
```

```python
import functools
import math

import jax
import jax.numpy as jnp
from jax import lax
from jax.experimental import pallas as pl
from jax.experimental.pallas import tpu as pltpu

F32 = jnp.float32
BF16 = jnp.bfloat16
HIGHEST = lax.Precision.HIGHEST

D_MODEL = 2048
EPS = 1e-6
CHUNK = 64
CONV_W = 4
CONV_PAD = 8
GDN_HEADS = 8
GDN_D = 128
GDN_QKV_W = 3 * GDN_HEADS * GDN_D
LRU_W = 1024
LRU_BLOCKS = 8
LRU_BLOCK_W = LRU_W // LRU_BLOCKS
LRU_C = 8.0
IN0_PAD_W = 6400
BD_COL_BLOCK = 6144 // 128
SWA_Q_HEADS = 16
SWA_KV_HEADS = 4
SWA_GROUP = SWA_Q_HEADS // SWA_KV_HEADS
SWA_HD = 64
SWA_KV_W = SWA_KV_HEADS * SWA_HD
WINDOW = 128
ROPE_THETA = 10000.0
PAST_LEN = 4096
SMLP_GROUPS = 8
SMLP_GROUP_W = 128
SMLP_W = SMLP_GROUPS * SMLP_GROUP_W
SMLP_CHUNK = 128
IN1_W = 3584
MEM_LEN = 256
MEM_HEADS = 4
MEM_HD = 128
MEM_W = MEM_HEADS * MEM_HD
FF_DENSE = 5632
N_EXPERTS = 8
FF_EXPERT = 2816
MOE_BM = 512
LANES = 128

VMEM_LIMIT_MB = 56


def _cparams(semantics, vmem_mb=VMEM_LIMIT_MB):
    return pltpu.CompilerParams(dimension_semantics=semantics, vmem_limit_bytes=vmem_mb * 2 ** 20)


def _rms(x, g):
    return x * lax.rsqrt(jnp.mean(x * x, axis=-1, keepdims=True) + EPS) * g


def _sigmoid(x):
    return 1.0 / (1.0 + jnp.exp(-x))


def _silu(x):
    return x * _sigmoid(x)


def _softplus(x):
    return jnp.maximum(x, 0.0) + jnp.log(1.0 + jnp.exp(-jnp.abs(x)))


def _gelu(x):
    c = math.sqrt(2.0 / math.pi)
    return 0.5 * x * (1.0 + jnp.tanh(c * (x + 0.044715 * (x * x * x))))


def _dot(a, b):
    return jnp.dot(a.astype(BF16), b.astype(BF16), preferred_element_type=F32)


def _dot_nt(a, b):
    return lax.dot_general(a.astype(BF16), b.astype(BF16), (((1,), (1,)), ((), ())),
                           preferred_element_type=F32)


def _dot_tn(a, b):
    return lax.dot_general(a.astype(BF16), b.astype(BF16), (((0,), (0,)), ((), ())),
                           preferred_element_type=F32)


def _dot_hi(a, b):
    return jnp.dot(a, b, precision=HIGHEST, preferred_element_type=F32)


def _norm_matmul_body(x_ref, g_ref, w_ref, o_ref, xn_ref):
    @pl.when(pl.program_id(1) == 0)
    def _():
        xn_ref[...] = _rms(x_ref[...], g_ref[...]).astype(BF16)

    o_ref[...] = jnp.dot(xn_ref[...], w_ref[...], preferred_element_type=F32)


def norm_matmul(x, g, w, tm, tn):
    n, k = x.shape
    nout = w.shape[1]
    tm = min(tm, n)
    return pl.pallas_call(
        _norm_matmul_body,
        grid=(n // tm, nout // tn),
        in_specs=[pl.BlockSpec((tm, k), lambda i, j: (i, 0)),
                  pl.BlockSpec((1, k), lambda i, j: (0, 0)),
                  pl.BlockSpec((k, tn), lambda i, j: (0, j))],
        out_specs=pl.BlockSpec((tm, tn), lambda i, j: (i, j)),
        out_shape=jax.ShapeDtypeStruct((n, nout), F32),
        scratch_shapes=[pltpu.VMEM((tm, k), BF16)],
        compiler_params=_cparams(("arbitrary", "arbitrary")),
        name="norm_matmul",
    )(x, g.reshape(1, k), w)


def _outproj_body(a_ref, b_ref, wa_ref, wb_ref, r_ref, g_ref, o_ref):
    acc = jnp.dot(a_ref[...].astype(BF16), wa_ref[...], preferred_element_type=F32)
    acc = acc + jnp.dot(b_ref[...].astype(BF16), wb_ref[...], preferred_element_type=F32)
    o_ref[...] = r_ref[...] + _rms(acc, g_ref[...])


def outproj_norm_resid(a, b, w, resid, g, tm=256):
    n, ka = a.shape
    kb = b.shape[1]
    d = w.shape[1]
    tm = min(tm, n)
    return pl.pallas_call(
        _outproj_body,
        grid=(n // tm,),
        in_specs=[pl.BlockSpec((tm, ka), lambda i: (i, 0)),
                  pl.BlockSpec((tm, kb), lambda i: (i, 0)),
                  pl.BlockSpec((ka, d), lambda i: (0, 0)),
                  pl.BlockSpec((kb, d), lambda i: (1, 0)),
                  pl.BlockSpec((tm, d), lambda i: (i, 0)),
                  pl.BlockSpec((1, d), lambda i: (0, 0))],
        out_specs=pl.BlockSpec((tm, d), lambda i: (i, 0)),
        out_shape=jax.ShapeDtypeStruct((n, d), F32),
        compiler_params=_cparams(("arbitrary",)),
        name="outproj_norm_resid",
    )(a, b, w, w, resid, g.reshape(1, d))


def _gdn_body(qkv_ref, z_ref, bd_ref, conv0_ref, s0_ref, cw_ref, pvec_ref, ng_ref,
              o_ref, sfin_ref, cfin_ref, s_scr, xbuf):
    c = pl.program_id(1)
    last = pl.num_programs(1) - 1
    lo = CONV_PAD - (CONV_W - 1)

    @pl.when(c == 0)
    def _():
        s_scr[...] = s0_ref[0]
        xbuf[lo:CONV_PAD, :] = conv0_ref[0]

    xbuf[CONV_PAD:CONV_PAD + CHUNK, :] = qkv_ref[...]
    y = xbuf[lo:lo + CHUNK, :] * cw_ref[0:1, :]
    for j in range(1, CONV_W):
        y = y + xbuf[lo + j:lo + j + CHUNK, :] * cw_ref[j:j + 1, :]
    tail = xbuf[CHUNK + lo:CHUNK + CONV_PAD, :]
    xbuf[lo:CONV_PAD, :] = tail

    @pl.when(c == last)
    def _():
        cfin_ref[0] = tail

    act = _silu(y)
    bd = bd_ref[...]
    beta = _sigmoid(bd)
    g_all = -jnp.exp(pvec_ref[0:1, :]) * _softplus(bd + pvec_ref[1:2, :])

    row = lax.broadcasted_iota(jnp.int32, (CHUNK, CHUNK), 0)
    col = lax.broadcasted_iota(jnp.int32, (CHUNK, CHUNK), 1)
    causal = row >= col
    strict = row > col
    eye = (row == col).astype(F32)
    g_cum = _dot_hi(causal.astype(F32), g_all)
    g_cum_t = _dot_hi(g_all.T, (row <= col).astype(F32))

    for h in range(GDN_HEADS):
        gl = GDN_HEADS + h
        gc = g_cum[:, gl:gl + 1]
        gr = g_cum_t[gl:gl + 1, :]
        decay = jnp.where(causal, jnp.exp(jnp.where(causal, gc - gr, 0.0)), 0.0)
        bcol = beta[:, h:h + 1]
        qh = act[:, GDN_D * h:GDN_D * (h + 1)]
        kh = act[:, GDN_HEADS * GDN_D + GDN_D * h:GDN_HEADS * GDN_D + GDN_D * (h + 1)]
        vh = act[:, 2 * GDN_HEADS * GDN_D + GDN_D * h:2 * GDN_HEADS * GDN_D + GDN_D * (h + 1)]
        qh = qh * lax.rsqrt(jnp.sum(qh * qh, axis=-1, keepdims=True) + EPS) * (GDN_D ** -0.5)
        kh = kh * lax.rsqrt(jnp.sum(kh * kh, axis=-1, keepdims=True) + EPS)
        kb = kh * bcol
        eg = jnp.exp(gc)
        a_low = jnp.where(strict, _dot_nt(kb, kh) * decay, 0.0)
        rhs = jnp.concatenate([vh * bcol, kb * eg], axis=-1)
        p = -a_low
        t_inv = eye + p
        for _ in range(5):
            p = _dot_hi(p, p)
            t_inv = t_inv + _dot_hi(t_inv, p)
        sol = _dot_hi(t_inv, rhs)
        u = sol[:, :GDN_D]
        w = sol[:, GDN_D:]
        qk = _dot_nt(qh, kh) * decay
        s = s_scr[h]
        g_last = g_cum[CHUNK - 1:CHUNK, gl:gl + 1]
        v_new = u - _dot(w, s)
        o = _dot(qh * eg, s) + _dot(qk, v_new)
        s_scr[h] = s * jnp.exp(g_last) + _dot_tn(kh * jnp.exp(g_last - gc), v_new)
        zh = z_ref[:, GDN_D * h:GDN_D * (h + 1)]
        o_ref[:, GDN_D * h:GDN_D * (h + 1)] = _rms(o, ng_ref[...]) * _silu(zh)

    @pl.when(c == last)
    def _():
        sfin_ref[0] = s_scr[...]


def gdn_mixer(p0, bsz, t, conv0, s0, conv_w, a_log, dt_bias, norm_g):
    nc = t // CHUNK
    n = bsz * t
    pvec = jnp.zeros((2, LANES), F32)
    pvec = pvec.at[0, GDN_HEADS:2 * GDN_HEADS].set(a_log).at[1, GDN_HEADS:2 * GDN_HEADS].set(dt_bias)
    vw = GDN_HEADS * GDN_D
    return pl.pallas_call(
        _gdn_body,
        grid=(bsz, nc),
        in_specs=[pl.BlockSpec((CHUNK, GDN_QKV_W), lambda b, c: (b * nc + c, 0)),
                  pl.BlockSpec((CHUNK, vw), lambda b, c: (b * nc + c, GDN_QKV_W // vw)),
                  pl.BlockSpec((CHUNK, LANES), lambda b, c: (b * nc + c, BD_COL_BLOCK)),
                  pl.BlockSpec((1, CONV_W - 1, GDN_QKV_W), lambda b, c: (b, 0, 0)),
                  pl.BlockSpec((1, GDN_HEADS, GDN_D, GDN_D), lambda b, c: (b, 0, 0, 0)),
                  pl.BlockSpec((CONV_W, GDN_QKV_W), lambda b, c: (0, 0)),
                  pl.BlockSpec((2, LANES), lambda b, c: (0, 0)),
                  pl.BlockSpec((1, GDN_D), lambda b, c: (0, 0))],
        out_specs=[pl.BlockSpec((CHUNK, vw), lambda b, c: (b * nc + c, 0)),
                   pl.BlockSpec((1, GDN_HEADS, GDN_D, GDN_D), lambda b, c: (b, 0, 0, 0)),
                   pl.BlockSpec((1, CONV_W - 1, GDN_QKV_W), lambda b, c: (b, 0, 0))],
        out_shape=[jax.ShapeDtypeStruct((n, vw), F32),
                   jax.ShapeDtypeStruct((bsz, GDN_HEADS, GDN_D, GDN_D), F32),
                   jax.ShapeDtypeStruct((bsz, CONV_W - 1, GDN_QKV_W), F32)],
        scratch_shapes=[pltpu.VMEM((GDN_HEADS, GDN_D, GDN_D), F32),
                        pltpu.VMEM((CONV_PAD + CHUNK, GDN_QKV_W), F32)],
        compiler_params=_cparams(("arbitrary", "arbitrary")),
        name="gdn_mixer",
    )(p0, p0, p0, conv0, s0, conv_w, pvec, norm_g.reshape(1, GDN_D))


def _lru_body(tl, x_ref, gate_ref, conv0_ref, h0_ref, cw_ref, cb_ref, wa_ref, ba_ref, wx_ref, bx_ref,
              lam_ref, y_ref, hfin_ref, cfin_ref, h_scr, xbuf, abuf, bbuf):
    c = pl.program_id(1)
    last = pl.num_programs(1) - 1
    lo = CONV_PAD - (CONV_W - 1)
    pad = tl // 2

    @pl.when(c == 0)
    def _():
        h_scr[...] = h0_ref[0]
        xbuf[lo:CONV_PAD, :] = conv0_ref[0]
        abuf[0:pad, :] = jnp.ones((pad, LRU_W), F32)
        bbuf[0:pad, :] = jnp.zeros((pad, LRU_W), F32)

    xbuf[CONV_PAD:CONV_PAD + tl, :] = x_ref[...]
    xr = xbuf[lo:lo + tl, :] * cw_ref[0:1, :]
    for j in range(1, CONV_W):
        xr = xr + xbuf[lo + j:lo + j + tl, :] * cw_ref[j:j + 1, :]
    tail = xbuf[tl + lo:tl + CONV_PAD, :]
    xbuf[lo:CONV_PAD, :] = tail

    @pl.when(c == last)
    def _():
        cfin_ref[0] = tail

    xr = xr + cb_ref[...]
    ga = jnp.concatenate([_dot(xr[:, LRU_BLOCK_W * n:LRU_BLOCK_W * (n + 1)], wa_ref[n])
                          for n in range(LRU_BLOCKS)], axis=-1)
    gx = jnp.concatenate([_dot(xr[:, LRU_BLOCK_W * n:LRU_BLOCK_W * (n + 1)], wx_ref[n])
                          for n in range(LRU_BLOCKS)], axis=-1)
    gate_a = _sigmoid(ga + ba_ref[...])
    gate_x = _sigmoid(gx + bx_ref[...])
    log_a = -LRU_C * gate_a * _softplus(-lam_ref[...])
    a = jnp.exp(log_a)
    b = jnp.sqrt(1.0 - jnp.exp(2.0 * log_a)) * gate_x * xr
    d = 1
    while d < tl:
        abuf[pad:pad + tl, :] = a
        bbuf[pad:pad + tl, :] = b
        a_sh = abuf[pad - d:pad - d + tl, :]
        b_sh = bbuf[pad - d:pad - d + tl, :]
        b = a * b_sh + b
        a = a * a_sh
        d *= 2
    h = a * h_scr[...] + b
    h_last = h[tl - 1:tl, :]
    h_scr[...] = h_last
    y_ref[...] = h * _gelu(gate_ref[...])

    @pl.when(c == last)
    def _():
        hfin_ref[0] = h_last


def lru_mixer(p0, bsz, t, conv0, h0, conv_w, conv_b, w_a, b_a, w_x, b_x, lam):
    tl = min(t, 256)
    nc = t // tl
    n = bsz * t
    row = lambda v: v.reshape(1, LRU_W)
    return pl.pallas_call(
        functools.partial(_lru_body, tl),
        grid=(bsz, nc),
        in_specs=[pl.BlockSpec((tl, LRU_W), lambda b, c: (b * nc + c, 4)),
                  pl.BlockSpec((tl, LRU_W), lambda b, c: (b * nc + c, 5)),
                  pl.BlockSpec((1, CONV_W - 1, LRU_W), lambda b, c: (b, 0, 0)),
                  pl.BlockSpec((1, 1, LRU_W), lambda b, c: (b, 0, 0)),
                  pl.BlockSpec((CONV_W, LRU_W), lambda b, c: (0, 0)),
                  pl.BlockSpec((1, LRU_W), lambda b, c: (0, 0)),
                  pl.BlockSpec((LRU_BLOCKS, LRU_BLOCK_W, LRU_BLOCK_W), lambda b, c: (0, 0, 0)),
                  pl.BlockSpec((1, LRU_W), lambda b, c: (0, 0)),
                  pl.BlockSpec((LRU_BLOCKS, LRU_BLOCK_W, LRU_BLOCK_W), lambda b, c: (0, 0, 0)),
                  pl.BlockSpec((1, LRU_W), lambda b, c: (0, 0)),
                  pl.BlockSpec((1, LRU_W), lambda b, c: (0, 0))],
        out_specs=[pl.BlockSpec((tl, LRU_W), lambda b, c: (b * nc + c, 0)),
                   pl.BlockSpec((1, 1, LRU_W), lambda b, c: (b, 0, 0)),
                   pl.BlockSpec((1, CONV_W - 1, LRU_W), lambda b, c: (b, 0, 0))],
        out_shape=[jax.ShapeDtypeStruct((n, LRU_W), F32),
                   jax.ShapeDtypeStruct((bsz, 1, LRU_W), F32),
                   jax.ShapeDtypeStruct((bsz, CONV_W - 1, LRU_W), F32)],
        scratch_shapes=[pltpu.VMEM((1, LRU_W), F32),
                        pltpu.VMEM((CONV_PAD + tl, LRU_W), F32),
                        pltpu.VMEM((tl // 2 + tl, LRU_W), F32),
                        pltpu.VMEM((tl // 2 + tl, LRU_W), F32)],
        compiler_params=_cparams(("arbitrary", "arbitrary")),
        name="lru_mixer",
    )(p0, p0, conv0, h0.reshape(bsz, 1, LRU_W), conv_w, row(conv_b), w_a, row(b_a), w_x, row(b_x), row(lam))


def _xattn_body(x_ref, mk_ref, mv_ref, wq_ref, wo_ref, g_in_ref, g_out_ref, o_ref):
    x = x_ref[...]
    q = jnp.dot(_rms(x, g_in_ref[...]).astype(BF16), wq_ref[...], preferred_element_type=F32)
    mk = mk_ref[0].astype(BF16)
    mv = mv_ref[0].astype(BF16)
    outs = []
    for h in range(MEM_HEADS):
        sl = slice(MEM_HD * h, MEM_HD * (h + 1))
        s = _dot_nt(q[:, sl], mk[:, sl]) * (MEM_HD ** -0.5)
        m = jnp.max(s, axis=-1, keepdims=True)
        p = jnp.exp(s - m)
        outs.append(_dot(p, mv[:, sl]) / jnp.sum(p, axis=-1, keepdims=True))
    o = jnp.concatenate(outs, axis=-1)
    y = jnp.dot(o.astype(BF16), wo_ref[...], preferred_element_type=F32)
    o_ref[...] = x + _rms(y, g_out_ref[...])


def cross_attention(x, bsz, t, mem_k, mem_v, wq, wo, g_in, g_out):
    tm = min(t, 512)
    nt = t // tm
    n, d = x.shape
    return pl.pallas_call(
        _xattn_body,
        grid=(bsz, nt),
        in_specs=[pl.BlockSpec((tm, d), lambda b, i: (b * nt + i, 0)),
                  pl.BlockSpec((1, MEM_LEN, MEM_W), lambda b, i: (b, 0, 0)),
                  pl.BlockSpec((1, MEM_LEN, MEM_W), lambda b, i: (b, 0, 0)),
                  pl.BlockSpec((d, MEM_W), lambda b, i: (0, 0)),
                  pl.BlockSpec((MEM_W, d), lambda b, i: (0, 0)),
                  pl.BlockSpec((1, d), lambda b, i: (0, 0)),
                  pl.BlockSpec((1, d), lambda b, i: (0, 0))],
        out_specs=pl.BlockSpec((tm, d), lambda b, i: (b * nt + i, 0)),
        out_shape=jax.ShapeDtypeStruct((n, d), F32),
        compiler_params=_cparams(("arbitrary", "arbitrary")),
        name="cross_attention",
    )(x, mem_k, mem_v, wq, wo, g_in.reshape(1, d), g_out.reshape(1, d))


def _ffn_body(x_ref, g_in_ref, wg_ref, wu_ref, wd_ref, g_out_ref, o_ref, xn_ref, acc_ref):
    f = pl.program_id(1)

    @pl.when(f == 0)
    def _():
        xn_ref[...] = _rms(x_ref[...], g_in_ref[...]).astype(BF16)
        acc_ref[...] = jnp.zeros_like(acc_ref)

    xn = xn_ref[...]
    gate = jnp.dot(xn, wg_ref[...], preferred_element_type=F32)
    up = jnp.dot(xn, wu_ref[...], preferred_element_type=F32)
    acc_ref[...] += jnp.dot((_silu(gate) * up).astype(BF16), wd_ref[...], preferred_element_type=F32)

    @pl.when(f == pl.num_programs(1) - 1)
    def _():
        o_ref[...] = x_ref[...] + _rms(acc_ref[...], g_out_ref[...])


def dense_ffn(x, g_in, wg, wu, wd, g_out, tm=512, tf=512):
    n, d = x.shape
    ff = wg.shape[1]
    tm = min(tm, n)
    return pl.pallas_call(
        _ffn_body,
        grid=(n // tm, ff // tf),
        in_specs=[pl.BlockSpec((tm, d), lambda i, f: (i, 0)),
                  pl.BlockSpec((1, d), lambda i, f: (0, 0)),
                  pl.BlockSpec((d, tf), lambda i, f: (0, f)),
                  pl.BlockSpec((d, tf), lambda i, f: (0, f)),
                  pl.BlockSpec((tf, d), lambda i, f: (f, 0)),
                  pl.BlockSpec((1, d), lambda i, f: (0, 0))],
        out_specs=pl.BlockSpec((tm, d), lambda i, f: (i, 0)),
        out_shape=jax.ShapeDtypeStruct((n, d), F32),
        scratch_shapes=[pltpu.VMEM((tm, d), BF16), pltpu.VMEM((tm, d), F32)],
        compiler_params=_cparams(("arbitrary", "arbitrary")),
        name="dense_ffn",
    )(x, g_in.reshape(1, d), wg, wu, wd, g_out.reshape(1, d))


def _swa_body(start, q_ref, kv_ref, cos_ref, sin_ref, kprev_ref, vprev_ref, sink_ref,
              o_ref, krot_ref, kbuf, vbuf):
    c = pl.program_id(1)

    @pl.when(c == 0)
    def _():
        kbuf[0:WINDOW, :] = kprev_ref[0]
        vbuf[0:WINDOW, :] = vprev_ref[0]

    cos = cos_ref[...]
    sin = sin_ref[...]
    lane = lax.broadcasted_iota(jnp.int32, (CHUNK, LANES), 1)
    first_half = (lane % SWA_HD) < (SWA_HD // 2)

    def rope(x):
        outs = []
        for j in range(x.shape[1] // LANES):
            xb = x[:, LANES * j:LANES * (j + 1)]
            fwd = pltpu.roll(xb, LANES - SWA_HD // 2, 1)
            bwd = pltpu.roll(xb, SWA_HD // 2, 1)
            outs.append(xb * cos + jnp.where(first_half, fwd, bwd) * sin)
        return jnp.concatenate(outs, axis=-1)

    q = rope(q_ref[...])
    kv = kv_ref[...]
    k = rope(kv[:, :SWA_KV_W])
    krot_ref[...] = k
    kbuf[WINDOW:WINDOW + CHUNK, :] = k
    vbuf[WINDOW:WINDOW + CHUNK, :] = kv[:, SWA_KV_W:]

    nk = WINDOW + CHUNK
    key_pos = start + c * CHUNK - WINDOW + lax.broadcasted_iota(jnp.int32, (CHUNK, nk), 1)
    valid = key_pos >= 0
    for hk in range(SWA_KV_HEADS):
        kh = kbuf[:, SWA_HD * hk:SWA_HD * (hk + 1)]
        vh = vbuf[:, SWA_HD * hk:SWA_HD * (hk + 1)]
        for gi in range(SWA_GROUP):
            hq = hk * SWA_GROUP + gi
            qh = q[:, SWA_HD * hq:SWA_HD * (hq + 1)]
            s = jnp.where(valid, _dot_nt(qh, kh) * (SWA_HD ** -0.5), -jnp.inf)
            sink = sink_ref[hq]
            m = jnp.maximum(jnp.max(s, axis=-1, keepdims=True), sink)
            p = jnp.exp(s - m)
            denom = jnp.sum(p, axis=-1, keepdims=True) + jnp.exp(sink - m)
            o_ref[:, SWA_HD * hq:SWA_HD * (hq + 1)] = _dot(p, vh) / denom

    kbuf[0:CHUNK, :] = kbuf[CHUNK:2 * CHUNK, :]
    kbuf[CHUNK:2 * CHUNK, :] = kbuf[2 * CHUNK:3 * CHUNK, :]
    vbuf[0:CHUNK, :] = vbuf[CHUNK:2 * CHUNK, :]
    vbuf[CHUNK:2 * CHUNK, :] = vbuf[2 * CHUNK:3 * CHUNK, :]


def _rope_tables(start, t):
    half = SWA_HD // 2
    inv_freq = jnp.exp(-math.log(ROPE_THETA) * jnp.arange(half, dtype=F32) / half)
    ang = (start + jnp.arange(t)).astype(F32)[:, None] * inv_freq[None, :]
    cos = jnp.cos(ang)
    sin = jnp.sin(ang)
    return jnp.tile(cos, (1, LANES // half)), jnp.tile(jnp.concatenate([-sin, sin], axis=-1), (1, LANES // SWA_HD))


def swa_mixer(p1, bsz, t, start, k_prev, v_prev, sinks):
    nc = t // CHUNK
    n = bsz * t
    qw = SWA_Q_HEADS * SWA_HD
    cos, sin = _rope_tables(start, t)
    return pl.pallas_call(
        functools.partial(_swa_body, start),
        grid=(bsz, nc),
        in_specs=[pl.BlockSpec((CHUNK, qw), lambda b, c: (b * nc + c, 0)),
                  pl.BlockSpec((CHUNK, 2 * SWA_KV_W), lambda b, c: (b * nc + c, 3 * qw // (2 * SWA_KV_W))),
                  pl.BlockSpec((CHUNK, LANES), lambda b, c: (c, 0)),
                  pl.BlockSpec((CHUNK, LANES), lambda b, c: (c, 0)),
                  pl.BlockSpec((1, WINDOW, SWA_KV_W), lambda b, c: (b, 0, 0)),
                  pl.BlockSpec((1, WINDOW, SWA_KV_W), lambda b, c: (b, 0, 0)),
                  pl.BlockSpec(memory_space=pltpu.SMEM)],
        out_specs=[pl.BlockSpec((CHUNK, qw), lambda b, c: (b * nc + c, 0)),
                   pl.BlockSpec((CHUNK, SWA_KV_W), lambda b, c: (b * nc + c, 0))],
        out_shape=[jax.ShapeDtypeStruct((n, qw), F32),
                   jax.ShapeDtypeStruct((n, SWA_KV_W), F32)],
        scratch_shapes=[pltpu.VMEM((WINDOW + CHUNK, SWA_KV_W), F32),
                        pltpu.VMEM((WINDOW + CHUNK, SWA_KV_W), F32)],
        compiler_params=_cparams(("arbitrary", "arbitrary")),
        name="swa_mixer",
    )(p1, p1, cos, sin, k_prev, v_prev, sinks)


def _smlp_body(lc, u_ref, v_ref, lg_ref, lb_ref, ws_ref, bs_ref, y_ref, vn_ref):
    v = _gelu(v_ref[...])
    mu = jnp.mean(v, axis=-1, keepdims=True)
    vc = v - mu
    vn = vc * lax.rsqrt(jnp.mean(vc * vc, axis=-1, keepdims=True) + EPS) * lg_ref[...] + lb_ref[...]
    vn_ref[...] = vn
    u = _gelu(u_ref[...])
    row = lax.broadcasted_iota(jnp.int32, (lc, lc), 0)
    col = lax.broadcasted_iota(jnp.int32, (lc, lc), 1)
    for g in range(SMLP_GROUPS):
        sl = slice(SMLP_GROUP_W * g, SMLP_GROUP_W * (g + 1))
        w = jnp.where(row >= col, ws_ref[g, 0:lc, 0:lc], 0.0)
        s = _dot(w, vn[:, sl]) + bs_ref[0:lc, g:g + 1]
        y_ref[:, sl] = u[:, sl] * s


def smlp_mixer(p1, bsz, t, ln_g, ln_b, w_spatial, b_spatial):
    lc = min(SMLP_CHUNK, t)
    n = bsz * t
    row = lambda v: v.reshape(1, SMLP_W)
    return pl.pallas_call(
        functools.partial(_smlp_body, lc),
        grid=(n // lc,),
        in_specs=[pl.BlockSpec((lc, SMLP_W), lambda i: (i, 1)),
                  pl.BlockSpec((lc, SMLP_W), lambda i: (i, 2)),
                  pl.BlockSpec((1, SMLP_W), lambda i: (0, 0)),
                  pl.BlockSpec((1, SMLP_W), lambda i: (0, 0)),
                  pl.BlockSpec((SMLP_GROUPS, SMLP_CHUNK, SMLP_CHUNK), lambda i: (0, 0, 0)),
                  pl.BlockSpec((SMLP_CHUNK, SMLP_GROUPS), lambda i: (0, 0))],
        out_specs=[pl.BlockSpec((lc, SMLP_W), lambda i: (i, 0)),
                   pl.BlockSpec((lc, SMLP_W), lambda i: (i, 0))],
        out_shape=[jax.ShapeDtypeStruct((n, SMLP_W), F32),
                   jax.ShapeDtypeStruct((n, SMLP_W), F32)],
        compiler_params=_cparams(("arbitrary",)),
        name="smlp_mixer",
    )(p1, p1, row(ln_g), row(ln_b), w_spatial, b_spatial.T)


def _router_body(x_ref, g_ref, wr_ref, hn_ref, idx_ref, gate_ref):
    hn = _rms(x_ref[...], g_ref[...])
    hn_ref[...] = hn
    logits = lax.dot_general(wr_ref[...], hn, (((1,), (1,)), ((), ())), precision=HIGHEST,
                             preferred_element_type=F32)
    e_iota = lax.broadcasted_iota(jnp.int32, logits.shape, 0)
    m1 = jnp.max(logits, axis=0, keepdims=True)
    i1 = jnp.min(jnp.where(logits == m1, e_iota, N_EXPERTS), axis=0, keepdims=True)
    rest = jnp.where(e_iota == i1, -jnp.inf, logits)
    m2 = jnp.max(rest, axis=0, keepdims=True)
    i2 = jnp.min(jnp.where(rest == m2, e_iota, N_EXPERTS), axis=0, keepdims=True)
    e2 = jnp.exp(m2 - m1)
    den = 1.0 + e2
    idx_ref[...] = jnp.concatenate([i1, i2], axis=0)
    gate_ref[...] = jnp.concatenate([1.0 / den, e2 / den], axis=0)


def moe_router(x, g, w_router, tm=512):
    n, d = x.shape
    tm = min(tm, n)
    return pl.pallas_call(
        _router_body,
        grid=(n // tm,),
        in_specs=[pl.BlockSpec((tm, d), lambda i: (i, 0)),
                  pl.BlockSpec((1, d), lambda i: (0, 0)),
                  pl.BlockSpec((N_EXPERTS, d), lambda i: (0, 0))],
        out_specs=[pl.BlockSpec((tm, d), lambda i: (i, 0)),
                   pl.BlockSpec((2, tm), lambda i: (0, i)),
                   pl.BlockSpec((2, tm), lambda i: (0, i))],
        out_shape=[jax.ShapeDtypeStruct((n, d), F32),
                   jax.ShapeDtypeStruct((2, n), jnp.int32),
                   jax.ShapeDtypeStruct((2, n), F32)],
        compiler_params=_cparams(("arbitrary",)),
        name="moe_router",
    )(x, g.reshape(1, d), w_router.T)


def _row_copy(src_hbm, src_row, dst_vmem, dst_row, sem):
    return pltpu.make_async_copy(src_hbm.at[pl.ds(src_row, 1), :], dst_vmem.at[pl.ds(dst_row, 1), :], sem)


def _moe_ffn_body(blk_e_ref, nact_ref, tok_ref, hn_hbm, rg_ref, wg_ref, wu_ref, wd_ref, ys_ref,
                  xs_scr, xb_scr, acc_scr, sem):
    i = pl.program_id(0)
    f = pl.program_id(1)
    active = i < nact_ref[0]

    @pl.when(jnp.logical_and(active, f == 0))
    def _():
        def start(r, carry):
            _row_copy(hn_hbm, tok_ref[0, 0, r], xs_scr, r, sem).start()
            return carry

        lax.fori_loop(0, MOE_BM, start, 0)

        def wait(r, carry):
            _row_copy(hn_hbm, 0, xs_scr, r, sem).wait()
            return carry

        lax.fori_loop(0, MOE_BM, wait, 0)
        xb_scr[...] = xs_scr[...].astype(BF16)
        acc_scr[...] = jnp.zeros_like(acc_scr)

    @pl.when(active)
    def _():
        xb = xb_scr[...]
        gate = jnp.dot(xb, wg_ref[0], preferred_element_type=F32)
        up = jnp.dot(xb, wu_ref[0], preferred_element_type=F32)
        acc_scr[...] += jnp.dot((_silu(gate) * up).astype(BF16), wd_ref[0], preferred_element_type=F32)

    last = f == pl.num_programs(1) - 1

    @pl.when(jnp.logical_and(active, last))
    def _():
        ys_ref[...] = acc_scr[...] * rg_ref[:, 0:1]

    @pl.when(jnp.logical_and(jnp.logical_not(active), last))
    def _():
        ys_ref[...] = jnp.zeros_like(ys_ref)


def moe_expert_ffn(hn, blk_e, n_active, row_tok, row_gate, wg, wu, wd, tf=256):
    n_rows = row_tok.shape[0]
    n_blk = n_rows // MOE_BM
    d = hn.shape[1]
    nf = FF_EXPERT // tf

    def w_col(i, f, be, na):
        return (be[i], 0, jnp.where(i < na[0], f, nf - 1))

    def w_row(i, f, be, na):
        return (be[i], jnp.where(i < na[0], f, nf - 1), 0)

    return pl.pallas_call(
        _moe_ffn_body,
        grid_spec=pltpu.PrefetchScalarGridSpec(
            num_scalar_prefetch=2,
            grid=(n_blk, nf),
            in_specs=[pl.BlockSpec((1, 1, MOE_BM), lambda i, f, be, na: (i, 0, 0), memory_space=pltpu.SMEM),
                      pl.BlockSpec(memory_space=pl.ANY),
                      pl.BlockSpec((MOE_BM, LANES), lambda i, f, be, na: (i, 0)),
                      pl.BlockSpec((1, d, tf), w_col),
                      pl.BlockSpec((1, d, tf), w_col),
                      pl.BlockSpec((1, tf, d), w_row)],
            out_specs=pl.BlockSpec((MOE_BM, d), lambda i, f, be, na: (i, 0)),
            scratch_shapes=[pltpu.VMEM((MOE_BM, d), F32), pltpu.VMEM((MOE_BM, d), BF16),
                            pltpu.VMEM((MOE_BM, d), F32), pltpu.SemaphoreType.DMA(())]),
        out_shape=jax.ShapeDtypeStruct((n_rows, d), F32),
        compiler_params=_cparams(("arbitrary", "arbitrary")),
        name="moe_expert_ffn",
    )(blk_e, n_active, row_tok.reshape(n_blk, 1, MOE_BM),
      hn, jnp.broadcast_to(row_gate[:, None], (n_rows, LANES)), wg, wu, wd)


def _moe_combine_body(tc, pos_ref, ys_hbm, x_ref, g_ref, o_ref, buf0, buf1, sem):
    def start(r, carry):
        _row_copy(ys_hbm, pos_ref[0, 0, r], buf0, r, sem).start()
        _row_copy(ys_hbm, pos_ref[0, 1, r], buf1, r, sem).start()
        return carry

    lax.fori_loop(0, tc, start, 0)

    def wait(r, carry):
        _row_copy(ys_hbm, 0, buf0, r, sem).wait()
        _row_copy(ys_hbm, 0, buf1, r, sem).wait()
        return carry

    lax.fori_loop(0, tc, wait, 0)
    o_ref[...] = x_ref[...] + _rms(buf0[...] + buf1[...], g_ref[...])


def moe_combine(ys, pos, x, g, tc=256):
    n, d = x.shape
    tc = min(tc, n)
    nb = n // tc
    pos_blocks = pos.reshape(2, nb, tc).transpose(1, 0, 2)
    return pl.pallas_call(
        functools.partial(_moe_combine_body, tc),
        grid=(nb,),
        in_specs=[pl.BlockSpec((1, 2, tc), lambda i: (i, 0, 0), memory_space=pltpu.SMEM),
                  pl.BlockSpec(memory_space=pl.ANY),
                  pl.BlockSpec((tc, d), lambda i: (i, 0)),
                  pl.BlockSpec((1, d), lambda i: (0, 0))],
        out_specs=pl.BlockSpec((tc, d), lambda i: (i, 0)),
        out_shape=jax.ShapeDtypeStruct((n, d), F32),
        scratch_shapes=[pltpu.VMEM((tc, d), F32), pltpu.VMEM((tc, d), F32), pltpu.SemaphoreType.DMA(())],
        compiler_params=_cparams(("arbitrary",)),
        name="moe_combine",
    )(pos_blocks, ys, x, g.reshape(1, d))


def _moe_plan(top_idx, top_gate):
    n = top_idx.shape[1]
    n_assign = 2 * n
    flat_e = top_idx.reshape(-1)
    flat_tok = jnp.tile(jnp.arange(n, dtype=jnp.int32), 2)
    onehot = (flat_e[:, None] == jnp.arange(N_EXPERTS, dtype=jnp.int32)[None, :]).astype(jnp.int32)
    rank = jnp.take_along_axis(jnp.cumsum(onehot, axis=0), flat_e[:, None], axis=1)[:, 0] - 1
    counts = jnp.sum(onehot, axis=0)
    padded = (counts + MOE_BM - 1) // MOE_BM * MOE_BM
    pad_end = jnp.cumsum(padded)
    pad_start = pad_end - padded
    dest = pad_start[flat_e] + rank
    n_blk = -(-n_assign // MOE_BM) + N_EXPERTS
    n_rows = n_blk * MOE_BM
    row_tok = jnp.zeros((n_rows,), jnp.int32).at[dest].set(flat_tok)
    row_gate = jnp.zeros((n_rows,), F32).at[dest].set(top_gate.reshape(-1))
    blk_start = jnp.arange(n_blk, dtype=jnp.int32) * MOE_BM
    blk_e = jnp.minimum(jnp.searchsorted(pad_end, blk_start, side='right'), N_EXPERTS - 1).astype(jnp.int32)
    n_active = (pad_end[-1] // MOE_BM).astype(jnp.int32).reshape(1)
    return row_tok, row_gate, blk_e, n_active, dest.reshape(2, n).astype(jnp.int32)


def moe_block(x, g_in, w_router, wg, wu, wd, g_out):
    hn, top_idx, top_gate = moe_router(x, g_in, w_router)
    row_tok, row_gate, blk_e, n_active, pos = _moe_plan(top_idx, top_gate)
    ys = moe_expert_ffn(hn, blk_e, n_active, row_tok, row_gate, wg, wu, wd)
    return moe_combine(ys, pos, x, g_out)


def _forward(x, start, mem_k, mem_v, gdn_conv0, gdn_s0, lru_conv0, lru_h0, swa_k0, swa_v0, p):
    bsz, t, d = x.shape
    x = x.reshape(bsz * t, d)
    ng = p['norm_g']
    p0 = norm_matmul(x, ng[0, 0], p['w_in0'], 512, 1280)
    o_gdn, gdn_s, gdn_conv = gdn_mixer(p0, bsz, t, gdn_conv0, gdn_s0, p['gdn_conv_w'], p['gdn_a_log'],
                                       p['gdn_dt_bias'], p['gdn_norm_g'])
    y_lru, lru_h, lru_conv = lru_mixer(p0, bsz, t, lru_conv0, lru_h0, p['lru_conv_w'], p['lru_conv_b'],
                                       p['lru_w_a'], p['lru_b_a'], p['lru_w_x'], p['lru_b_x'], p['lru_lambda'])
    x = outproj_norm_resid(o_gdn, y_lru, p['w_out0'], x, ng[0, 1])
    x = cross_attention(x, bsz, t, mem_k[0], mem_v[0], p['w_xq'][0], p['w_xo'][0], ng[0, 2], ng[0, 3])
    x = dense_ffn(x, ng[0, 4], p['w_ff_gate'], p['w_ff_up'], p['w_ff_down'], ng[0, 5])
    p1 = norm_matmul(x, ng[1, 0], p['w_in1'], 512, 1792)
    attn, k_rot = swa_mixer(p1, bsz, t, start, swa_k0, swa_v0, p['swa_sinks'])
    y_smlp, smlp_v = smlp_mixer(p1, bsz, t, p['smlp_ln_g'], p['smlp_ln_b'], p['w_spatial'], p['b_spatial'])
    x = outproj_norm_resid(attn, y_smlp, p['w_out1'], x, ng[1, 1])
    x = cross_attention(x, bsz, t, mem_k[1], mem_v[1], p['w_xq'][1], p['w_xo'][1], ng[1, 2], ng[1, 3])
    x = moe_block(x, ng[1, 4], p['w_router'], p['w_moe_gate'], p['w_moe_up'], p['w_moe_down'], ng[1, 5])
    v_rows = p1[:, IN1_W - SWA_KV_W:].reshape(bsz, t, SWA_KV_HEADS, SWA_HD)
    return (x.reshape(bsz, t, d), gdn_conv, gdn_s, lru_conv, lru_h.reshape(bsz, LRU_W),
            k_rot.reshape(bsz, t, SWA_KV_HEADS, SWA_HD), v_rows, smlp_v.reshape(bsz, t, SMLP_W))


def _prepare_weights(norm_g, w_in0, gdn_conv_w, gdn_a_log, gdn_dt_bias, gdn_norm_g, lru_conv_w, lru_conv_b,
                     lru_w_a, lru_b_a, lru_w_x, lru_b_x, lru_lambda, w_out0, w_in1, swa_sinks, smlp_ln_g,
                     smlp_ln_b, w_spatial, b_spatial, w_out1, w_xq, w_xo, w_ff_gate, w_ff_up, w_ff_down,
                     w_router, w_moe_gate, w_moe_up, w_moe_down):
    qkvz_w = GDN_QKV_W + GDN_HEADS * GDN_D
    bd_w = 2 * GDN_HEADS
    w0 = jnp.concatenate([w_in0[:, :qkvz_w], w_in0[:, qkvz_w + bd_w:], w_in0[:, qkvz_w:qkvz_w + bd_w],
                          jnp.zeros((D_MODEL, IN0_PAD_W - w_in0.shape[1]), w_in0.dtype)], axis=1)
    qw = SWA_Q_HEADS * SWA_HD
    w1 = jnp.concatenate([w_in1[:, :qw], w_in1[:, qw + 2 * SWA_KV_W:], w_in1[:, qw:qw + 2 * SWA_KV_W]], axis=1)
    return dict(
        norm_g=norm_g, w_in0=w0.astype(BF16), gdn_conv_w=gdn_conv_w, gdn_a_log=gdn_a_log,
        gdn_dt_bias=gdn_dt_bias, gdn_norm_g=gdn_norm_g, lru_conv_w=lru_conv_w, lru_conv_b=lru_conv_b,
        lru_w_a=lru_w_a, lru_b_a=lru_b_a, lru_w_x=lru_w_x, lru_b_x=lru_b_x, lru_lambda=lru_lambda,
        w_out0=w_out0.astype(BF16), w_in1=w1.astype(BF16), swa_sinks=swa_sinks, smlp_ln_g=smlp_ln_g,
        smlp_ln_b=smlp_ln_b, w_spatial=w_spatial, b_spatial=b_spatial, w_out1=w_out1.astype(BF16),
        w_xq=w_xq.astype(BF16), w_xo=w_xo.astype(BF16), w_ff_gate=w_ff_gate.astype(BF16),
        w_ff_up=w_ff_up.astype(BF16), w_ff_down=w_ff_down.astype(BF16), w_router=w_router,
        w_moe_gate=w_moe_gate.astype(BF16), w_moe_up=w_moe_up.astype(BF16), w_moe_down=w_moe_down.astype(BF16))


def kernel(x_prompt, x_sample, mem_prompt, cache_mem_k, cache_mem_v, state_gdn, state_gdn_conv, state_rglru_h, state_rglru_conv, cache_swa_k, cache_swa_v, norm_g, mem_norm_g, w_in0, gdn_conv_w, gdn_a_log, gdn_dt_bias, gdn_norm_g, lru_conv_w, lru_conv_b, lru_w_a, lru_b_a, lru_w_x, lru_b_x, lru_lambda, w_out0, w_in1, swa_sinks, smlp_ln_g, smlp_ln_b, w_spatial, b_spatial, w_out1, w_xq, w_xk, w_xv, w_xo, w_ff_gate, w_ff_up, w_ff_down, w_router, w_moe_gate, w_moe_up, w_moe_down):
    p = _prepare_weights(norm_g, w_in0, gdn_conv_w, gdn_a_log, gdn_dt_bias, gdn_norm_g, lru_conv_w, lru_conv_b,
                         lru_w_a, lru_b_a, lru_w_x, lru_b_x, lru_lambda, w_out0, w_in1, swa_sinks, smlp_ln_g,
                         smlp_ln_b, w_spatial, b_spatial, w_out1, w_xq, w_xo, w_ff_gate, w_ff_up, w_ff_down,
                         w_router, w_moe_gate, w_moe_up, w_moe_down)
    bsz, t, d = x_prompt.shape
    depth = w_xk.shape[0]
    mem_flat = mem_prompt.reshape(bsz * MEM_LEN, d)
    mem_k_p = jnp.stack([norm_matmul(mem_flat, mem_norm_g[l], w_xk[l].astype(BF16), 512, MEM_W)
                         for l in range(depth)]).reshape(depth, bsz, MEM_LEN, MEM_W)
    mem_v_p = jnp.stack([norm_matmul(mem_flat, mem_norm_g[l], w_xv[l].astype(BF16), 512, MEM_W)
                         for l in range(depth)]).reshape(depth, bsz, MEM_LEN, MEM_W)
    (y_p, gdn_conv_p, gdn_s_p, lru_conv_p, lru_h_p, k_rows_p, v_rows_p, _) = _forward(
        x_prompt, 0, mem_k_p, mem_v_p,
        jnp.zeros((bsz, CONV_W - 1, GDN_QKV_W), F32), jnp.zeros((bsz, GDN_HEADS, GDN_D, GDN_D), F32),
        jnp.zeros((bsz, CONV_W - 1, LRU_W), F32), jnp.zeros((bsz, LRU_W), F32),
        jnp.zeros((bsz, WINDOW, SWA_KV_W), F32), jnp.zeros((bsz, WINDOW, SWA_KV_W), F32), p)
    keep = min(WINDOW, t)
    dbs = x_sample.shape[0]
    n_prev = cache_swa_k.shape[1]
    assert n_prev == WINDOW
    (y_s, gdn_conv_s, gdn_s_s, lru_conv_s, lru_h_s, k_rows_s, v_rows_s, smlp_v_s) = _forward(
        x_sample, PAST_LEN,cache_mem_k.reshape(depth, dbs, MEM_LEN, MEM_W),
        cache_mem_v.reshape(depth, dbs, MEM_LEN, MEM_W), state_gdn_conv, state_gdn, state_rglru_conv,
        state_rglru_h, cache_swa_k.reshape(dbs, n_prev, SWA_KV_W), cache_swa_v.reshape(dbs, n_prev, SWA_KV_W), p)
    shape5 = (depth, bsz, MEM_LEN, MEM_HEADS, MEM_HD)
    return (y_p, y_s, mem_k_p.reshape(shape5), mem_v_p.reshape(shape5), gdn_s_p, gdn_conv_p, lru_h_p, lru_conv_p,
            k_rows_p[:, t - keep:], v_rows_p[:, t - keep:], gdn_s_s, gdn_conv_s, lru_h_s, lru_conv_s,
            k_rows_s, v_rows_s, smlp_v_s)
```

```python
import functools
import math

import jax
import jax.numpy as jnp
from jax import lax
from jax.experimental import pallas as pl
from jax.experimental.pallas import tpu as pltpu

F32 = jnp.float32
BF16 = jnp.bfloat16
HIGHEST = lax.Precision.HIGHEST

D_MODEL = 2048
EPS = 1e-6
CHUNK = 64
CONV_W = 4
CONV_PAD = 8
GDN_HEADS = 8
GDN_D = 128
GDN_QKV_W = 3 * GDN_HEADS * GDN_D
LRU_W = 1024
LRU_BLOCKS = 8
LRU_BLOCK_W = LRU_W // LRU_BLOCKS
LRU_C = 8.0
IN0_PAD_W = 6400
BD_COL_BLOCK = 6144 // 128
SWA_Q_HEADS = 16
SWA_KV_HEADS = 4
SWA_GROUP = SWA_Q_HEADS // SWA_KV_HEADS
SWA_HD = 64
SWA_KV_W = SWA_KV_HEADS * SWA_HD
WINDOW = 128
ROPE_THETA = 10000.0
PAST_LEN = 4096
SMLP_GROUPS = 8
SMLP_GROUP_W = 128
SMLP_W = SMLP_GROUPS * SMLP_GROUP_W
SMLP_CHUNK = 128
IN1_W = 3584
MEM_LEN = 256
MEM_HEADS = 4
MEM_HD = 128
MEM_W = MEM_HEADS * MEM_HD
FF_DENSE = 5632
N_EXPERTS = 8
FF_EXPERT = 2816
MOE_BM = 512
LANES = 128

VMEM_LIMIT_MB = 56


def _cparams(semantics, vmem_mb=VMEM_LIMIT_MB):
    return pltpu.CompilerParams(dimension_semantics=semantics, vmem_limit_bytes=vmem_mb * 2 ** 20)


def _rms(x, g):
    return x * lax.rsqrt(jnp.mean(x * x, axis=-1, keepdims=True) + EPS) * g


def _sigmoid(x):
    return 1.0 / (1.0 + jnp.exp(-x))


def _silu(x):
    return x * _sigmoid(x)


def _softplus(x):
    return jnp.maximum(x, 0.0) + jnp.log(1.0 + jnp.exp(-jnp.abs(x)))


def _gelu(x):
    c = math.sqrt(2.0 / math.pi)
    return 0.5 * x * (1.0 + jnp.tanh(c * (x + 0.044715 * (x * x * x))))


def _dot(a, b):
    return jnp.dot(a.astype(BF16), b.astype(BF16), preferred_element_type=F32)


def _dot_nt(a, b):
    return lax.dot_general(a.astype(BF16), b.astype(BF16), (((1,), (1,)), ((), ())),
                           preferred_element_type=F32)


def _dot_tn(a, b):
    return lax.dot_general(a.astype(BF16), b.astype(BF16), (((0,), (0,)), ((), ())),
                           preferred_element_type=F32)


def _split3(x):
    x1 = x.astype(BF16)
    r1 = x - x1.astype(F32)
    x2 = r1.astype(BF16)
    x3 = (r1 - x2.astype(F32)).astype(BF16)
    return x1, x2, x3


def _norm_matmul_body(x_ref, g_ref, w_ref, o_ref, xn_ref):
    @pl.when(pl.program_id(1) == 0)
    def _():
        xn_ref[...] = _rms(x_ref[...], g_ref[...]).astype(BF16)

    o_ref[...] = jnp.dot(xn_ref[...], w_ref[...], preferred_element_type=F32)


def norm_matmul(x, g, w, tm, tn):
    n, k = x.shape
    nout = w.shape[1]
    tm = min(tm, n)
    return pl.pallas_call(
        _norm_matmul_body,
        grid=(n // tm, nout // tn),
        in_specs=[pl.BlockSpec((tm, k), lambda i, j: (i, 0)),
                  pl.BlockSpec((1, k), lambda i, j: (0, 0)),
                  pl.BlockSpec((k, tn), lambda i, j: (0, j))],
        out_specs=pl.BlockSpec((tm, tn), lambda i, j: (i, j)),
        out_shape=jax.ShapeDtypeStruct((n, nout), F32),
        scratch_shapes=[pltpu.VMEM((tm, k), BF16)],
        compiler_params=_cparams(("arbitrary", "arbitrary")),
        name="norm_matmul",
    )(x, g.reshape(1, k), w)


def _outproj_body(a_ref, b_ref, wa_ref, wb_ref, r_ref, g_ref, o_ref):
    acc = jnp.dot(a_ref[...].astype(BF16), wa_ref[...], preferred_element_type=F32)
    acc = acc + jnp.dot(b_ref[...].astype(BF16), wb_ref[...], preferred_element_type=F32)
    o_ref[...] = r_ref[...] + _rms(acc, g_ref[...])


def outproj_norm_resid(a, b, w, resid, g, tm=256):
    n, ka = a.shape
    kb = b.shape[1]
    d = w.shape[1]
    tm = min(tm, n)
    return pl.pallas_call(
        _outproj_body,
        grid=(n // tm,),
        in_specs=[pl.BlockSpec((tm, ka), lambda i: (i, 0)),
                  pl.BlockSpec((tm, kb), lambda i: (i, 0)),
                  pl.BlockSpec((ka, d), lambda i: (0, 0)),
                  pl.BlockSpec((kb, d), lambda i: (1, 0)),
                  pl.BlockSpec((tm, d), lambda i: (i, 0)),
                  pl.BlockSpec((1, d), lambda i: (0, 0))],
        out_specs=pl.BlockSpec((tm, d), lambda i: (i, 0)),
        out_shape=jax.ShapeDtypeStruct((n, d), F32),
        compiler_params=_cparams(("arbitrary",)),
        name="outproj_norm_resid",
    )(a, b, w, w, resid, g.reshape(1, d))


def _gdn_body(qkv_ref, z_ref, bd_ref, conv0_ref, s0_ref, cw_ref, pvec_ref, ng_ref,
              o_ref, sfin_ref, cfin_ref, s_scr, xbuf):
    c = pl.program_id(1)
    last = pl.num_programs(1) - 1
    lo = CONV_PAD - (CONV_W - 1)

    @pl.when(c == 0)
    def _():
        s_scr[...] = s0_ref[0]
        xbuf[lo:CONV_PAD, :] = conv0_ref[0]

    xbuf[CONV_PAD:CONV_PAD + CHUNK, :] = qkv_ref[...]
    y = xbuf[lo:lo + CHUNK, :] * cw_ref[0:1, :]
    for j in range(1, CONV_W):
        y = y + xbuf[lo + j:lo + j + CHUNK, :] * cw_ref[j:j + 1, :]
    tail = xbuf[CHUNK + lo:CHUNK + CONV_PAD, :]
    xbuf[lo:CONV_PAD, :] = tail

    @pl.when(c == last)
    def _():
        cfin_ref[0] = tail

    act = _silu(y)
    bd = bd_ref[...]
    beta = _sigmoid(bd)
    g_all = -jnp.exp(pvec_ref[0:1, :]) * _softplus(bd + pvec_ref[1:2, :])

    row = lax.broadcasted_iota(jnp.int32, (CHUNK, CHUNK), 0)
    col = lax.broadcasted_iota(jnp.int32, (CHUNK, CHUNK), 1)
    causal = row >= col
    strict = row > col
    blk_xor = row ^ col
    g_cum3 = jnp.dot(causal.astype(BF16), jnp.concatenate(_split3(g_all), axis=-1), preferred_element_type=F32)
    g_cum = g_cum3[:, :LANES] + g_cum3[:, LANES:2 * LANES] + g_cum3[:, 2 * LANES:]
    g_cum_t = g_cum.T

    heads = range(GDN_HEADS)
    gc = [g_cum[:, GDN_HEADS + h:GDN_HEADS + h + 1] for h in heads]
    gr = [g_cum_t[GDN_HEADS + h:GDN_HEADS + h + 1, :] for h in heads]
    decay = [jnp.where(causal, jnp.exp(jnp.where(causal, gc[h] - gr[h], 0.0)), 0.0) for h in heads]
    bcol = [beta[:, h:h + 1] for h in heads]
    hw = GDN_HEADS * GDN_D
    q = [act[:, GDN_D * h:GDN_D * (h + 1)] for h in heads]
    k = [act[:, hw + GDN_D * h:hw + GDN_D * (h + 1)] for h in heads]
    v = [act[:, 2 * hw + GDN_D * h:2 * hw + GDN_D * (h + 1)] for h in heads]
    q = [x * lax.rsqrt(jnp.sum(x * x, axis=-1, keepdims=True) + EPS) * (GDN_D ** -0.5) for x in q]
    k = [x * lax.rsqrt(jnp.sum(x * x, axis=-1, keepdims=True) + EPS) for x in k]
    kb = [k[h] * bcol[h] for h in heads]
    eg = [jnp.exp(gc[h]) for h in heads]
    qa = [_dot_nt(jnp.concatenate([q[h], kb[h]], axis=0), k[h]) for h in heads]
    qk = [qa[h][:CHUNK] * decay[h] for h in heads]
    a_low = [jnp.where(strict, qa[h][CHUNK:] * decay[h], 0.0) for h in heads]
    m = [jnp.where((blk_xor >> 2) == 0, -a_low[h], 0.0) for h in heads]
    m2 = [_dot(m[h], m[h]) for h in heads]
    n = [m[h] + m2[h] + _dot(m[h], m2[h]) for h in heads]
    for lg in range(2, 6):
        low = [jnp.where((blk_xor >> lg) == 1, a_low[h], 0.0) for h in heads]
        tl = [low[h] + _dot(n[h], low[h]) for h in heads]
        n = [n[h] - (tl[h] + _dot(tl[h], n[h])) for h in heads]
    rhs = [jnp.concatenate([v[h] * bcol[h], kb[h] * eg[h]], axis=-1) for h in heads]
    sol = [rhs[h] + _dot(n[h], rhs[h]) for h in heads]
    s = [s_scr[h] for h in heads]
    ws = [_dot(jnp.concatenate([sol[h][:, GDN_D:], q[h] * eg[h]], axis=0), s[h]) for h in heads]
    v_new = [sol[h][:, :GDN_D] - ws[h][:CHUNK] for h in heads]
    o = [ws[h][CHUNK:] + _dot(qk[h], v_new[h]) for h in heads]
    g_last = [g_cum[CHUNK - 1:CHUNK, GDN_HEADS + h:GDN_HEADS + h + 1] for h in heads]
    for h in heads:
        s_scr[h] = s[h] * jnp.exp(g_last[h]) + _dot_tn(k[h] * jnp.exp(g_last[h] - gc[h]), v_new[h])
    for h in heads:
        zh = z_ref[:, GDN_D * h:GDN_D * (h + 1)]
        o_ref[:, GDN_D * h:GDN_D * (h + 1)] = _rms(o[h], ng_ref[...]) * _silu(zh)

    @pl.when(c == last)
    def _():
        sfin_ref[0] = s_scr[...]


def gdn_mixer(p0, bsz, t, conv0, s0, conv_w, a_log, dt_bias, norm_g):
    nc = t // CHUNK
    n = bsz * t
    pvec = jnp.zeros((2, LANES), F32)
    pvec = pvec.at[0, GDN_HEADS:2 * GDN_HEADS].set(a_log).at[1, GDN_HEADS:2 * GDN_HEADS].set(dt_bias)
    vw = GDN_HEADS * GDN_D
    return pl.pallas_call(
        _gdn_body,
        grid=(bsz, nc),
        in_specs=[pl.BlockSpec((CHUNK, GDN_QKV_W), lambda b, c: (b * nc + c, 0)),
                  pl.BlockSpec((CHUNK, vw), lambda b, c: (b * nc + c, GDN_QKV_W // vw)),
                  pl.BlockSpec((CHUNK, LANES), lambda b, c: (b * nc + c, BD_COL_BLOCK)),
                  pl.BlockSpec((1, CONV_W - 1, GDN_QKV_W), lambda b, c: (b, 0, 0)),
                  pl.BlockSpec((1, GDN_HEADS, GDN_D, GDN_D), lambda b, c: (b, 0, 0, 0)),
                  pl.BlockSpec((CONV_W, GDN_QKV_W), lambda b, c: (0, 0)),
                  pl.BlockSpec((2, LANES), lambda b, c: (0, 0)),
                  pl.BlockSpec((1, GDN_D), lambda b, c: (0, 0))],
        out_specs=[pl.BlockSpec((CHUNK, vw), lambda b, c: (b * nc + c, 0)),
                   pl.BlockSpec((1, GDN_HEADS, GDN_D, GDN_D), lambda b, c: (b, 0, 0, 0)),
                   pl.BlockSpec((1, CONV_W - 1, GDN_QKV_W), lambda b, c: (b, 0, 0))],
        out_shape=[jax.ShapeDtypeStruct((n, vw), F32),
                   jax.ShapeDtypeStruct((bsz, GDN_HEADS, GDN_D, GDN_D), F32),
                   jax.ShapeDtypeStruct((bsz, CONV_W - 1, GDN_QKV_W), F32)],
        scratch_shapes=[pltpu.VMEM((GDN_HEADS, GDN_D, GDN_D), F32),
                        pltpu.VMEM((CONV_PAD + CHUNK, GDN_QKV_W), F32)],
        compiler_params=_cparams(("arbitrary", "arbitrary")),
        name="gdn_mixer",
    )(p0, p0, p0, conv0, s0, conv_w, pvec, norm_g.reshape(1, GDN_D))


def _lru_body(tl, x_ref, gate_ref, conv0_ref, h0_ref, cw_ref, cb_ref, wa_ref, ba_ref, wx_ref, bx_ref,
              lam_ref, y_ref, hfin_ref, cfin_ref, h_scr, xbuf, abuf, bbuf):
    c = pl.program_id(1)
    last = pl.num_programs(1) - 1
    lo = CONV_PAD - (CONV_W - 1)
    pad = tl // 2

    @pl.when(c == 0)
    def _():
        h_scr[...] = h0_ref[0]
        xbuf[lo:CONV_PAD, :] = conv0_ref[0]
        abuf[0:pad, :] = jnp.ones((pad, LRU_W), F32)
        bbuf[0:pad, :] = jnp.zeros((pad, LRU_W), F32)

    xbuf[CONV_PAD:CONV_PAD + tl, :] = x_ref[...]
    xr = xbuf[lo:lo + tl, :] * cw_ref[0:1, :]
    for j in range(1, CONV_W):
        xr = xr + xbuf[lo + j:lo + j + tl, :] * cw_ref[j:j + 1, :]
    tail = xbuf[tl + lo:tl + CONV_PAD, :]
    xbuf[lo:CONV_PAD, :] = tail

    @pl.when(c == last)
    def _():
        cfin_ref[0] = tail

    xr = xr + cb_ref[...]
    ga = jnp.concatenate([_dot(xr[:, LRU_BLOCK_W * n:LRU_BLOCK_W * (n + 1)], wa_ref[n])
                          for n in range(LRU_BLOCKS)], axis=-1)
    gx = jnp.concatenate([_dot(xr[:, LRU_BLOCK_W * n:LRU_BLOCK_W * (n + 1)], wx_ref[n])
                          for n in range(LRU_BLOCKS)], axis=-1)
    gate_a = _sigmoid(ga + ba_ref[...])
    gate_x = _sigmoid(gx + bx_ref[...])
    log_a = -LRU_C * gate_a * _softplus(-lam_ref[...])
    a = jnp.exp(log_a)
    b = jnp.sqrt(1.0 - jnp.exp(2.0 * log_a)) * gate_x * xr
    d = 1
    while d < tl:
        abuf[pad:pad + tl, :] = a
        bbuf[pad:pad + tl, :] = b
        a_sh = abuf[pad - d:pad - d + tl, :]
        b_sh = bbuf[pad - d:pad - d + tl, :]
        b = a * b_sh + b
        a = a * a_sh
        d *= 2
    h = a * h_scr[...] + b
    h_last = h[tl - 1:tl, :]
    h_scr[...] = h_last
    y_ref[...] = h * _gelu(gate_ref[...])

    @pl.when(c == last)
    def _():
        hfin_ref[0] = h_last


def lru_mixer(p0, bsz, t, conv0, h0, conv_w, conv_b, w_a, b_a, w_x, b_x, lam):
    tl = min(t, 256)
    nc = t // tl
    n = bsz * t
    row = lambda v: v.reshape(1, LRU_W)
    return pl.pallas_call(
        functools.partial(_lru_body, tl),
        grid=(bsz, nc),
        in_specs=[pl.BlockSpec((tl, LRU_W), lambda b, c: (b * nc + c, 4)),
                  pl.BlockSpec((tl, LRU_W), lambda b, c: (b * nc + c, 5)),
                  pl.BlockSpec((1, CONV_W - 1, LRU_W), lambda b, c: (b, 0, 0)),
                  pl.BlockSpec((1, 1, LRU_W), lambda b, c: (b, 0, 0)),
                  pl.BlockSpec((CONV_W, LRU_W), lambda b, c: (0, 0)),
                  pl.BlockSpec((1, LRU_W), lambda b, c: (0, 0)),
                  pl.BlockSpec((LRU_BLOCKS, LRU_BLOCK_W, LRU_BLOCK_W), lambda b, c: (0, 0, 0)),
                  pl.BlockSpec((1, LRU_W), lambda b, c: (0, 0)),
                  pl.BlockSpec((LRU_BLOCKS, LRU_BLOCK_W, LRU_BLOCK_W), lambda b, c: (0, 0, 0)),
                  pl.BlockSpec((1, LRU_W), lambda b, c: (0, 0)),
                  pl.BlockSpec((1, LRU_W), lambda b, c: (0, 0))],
        out_specs=[pl.BlockSpec((tl, LRU_W), lambda b, c: (b * nc + c, 0)),
                   pl.BlockSpec((1, 1, LRU_W), lambda b, c: (b, 0, 0)),
                   pl.BlockSpec((1, CONV_W - 1, LRU_W), lambda b, c: (b, 0, 0))],
        out_shape=[jax.ShapeDtypeStruct((n, LRU_W), F32),
                   jax.ShapeDtypeStruct((bsz, 1, LRU_W), F32),
                   jax.ShapeDtypeStruct((bsz, CONV_W - 1, LRU_W), F32)],
        scratch_shapes=[pltpu.VMEM((1, LRU_W), F32),
                        pltpu.VMEM((CONV_PAD + tl, LRU_W), F32),
                        pltpu.VMEM((tl // 2 + tl, LRU_W), F32),
                        pltpu.VMEM((tl // 2 + tl, LRU_W), F32)],
        compiler_params=_cparams(("arbitrary", "arbitrary")),
        name="lru_mixer",
    )(p0, p0, conv0, h0.reshape(bsz, 1, LRU_W), conv_w, row(conv_b), w_a, row(b_a), w_x, row(b_x), row(lam))


def _xattn_body(x_ref, mk_ref, mv_ref, wq_ref, wo_ref, g_in_ref, g_out_ref, o_ref):
    x = x_ref[...]
    q = jnp.dot(_rms(x, g_in_ref[...]).astype(BF16), wq_ref[...], preferred_element_type=F32)
    mk = mk_ref[0].astype(BF16)
    mv = mv_ref[0].astype(BF16)
    outs = []
    for h in range(MEM_HEADS):
        sl = slice(MEM_HD * h, MEM_HD * (h + 1))
        s = _dot_nt(q[:, sl], mk[:, sl]) * (MEM_HD ** -0.5)
        m = jnp.max(s, axis=-1, keepdims=True)
        p = jnp.exp(s - m)
        outs.append(_dot(p, mv[:, sl]) / jnp.sum(p, axis=-1, keepdims=True))
    o = jnp.concatenate(outs, axis=-1)
    y = jnp.dot(o.astype(BF16), wo_ref[...], preferred_element_type=F32)
    o_ref[...] = x + _rms(y, g_out_ref[...])


def cross_attention(x, bsz, t, mem_k, mem_v, wq, wo, g_in, g_out):
    tm = min(t, 512)
    nt = t // tm
    n, d = x.shape
    return pl.pallas_call(
        _xattn_body,
        grid=(bsz, nt),
        in_specs=[pl.BlockSpec((tm, d), lambda b, i: (b * nt + i, 0)),
                  pl.BlockSpec((1, MEM_LEN, MEM_W), lambda b, i: (b, 0, 0)),
                  pl.BlockSpec((1, MEM_LEN, MEM_W), lambda b, i: (b, 0, 0)),
                  pl.BlockSpec((d, MEM_W), lambda b, i: (0, 0)),
                  pl.BlockSpec((MEM_W, d), lambda b, i: (0, 0)),
                  pl.BlockSpec((1, d), lambda b, i: (0, 0)),
                  pl.BlockSpec((1, d), lambda b, i: (0, 0))],
        out_specs=pl.BlockSpec((tm, d), lambda b, i: (b * nt + i, 0)),
        out_shape=jax.ShapeDtypeStruct((n, d), F32),
        compiler_params=_cparams(("arbitrary", "arbitrary")),
        name="cross_attention",
    )(x, mem_k, mem_v, wq, wo, g_in.reshape(1, d), g_out.reshape(1, d))


def _ffn_body(x_ref, g_in_ref, wg_ref, wu_ref, wd_ref, g_out_ref, o_ref, xn_ref, acc_ref):
    f = pl.program_id(1)

    @pl.when(f == 0)
    def _():
        xn_ref[...] = _rms(x_ref[...], g_in_ref[...]).astype(BF16)
        acc_ref[...] = jnp.zeros_like(acc_ref)

    xn = xn_ref[...]
    gate = jnp.dot(xn, wg_ref[...], preferred_element_type=F32)
    up = jnp.dot(xn, wu_ref[...], preferred_element_type=F32)
    acc_ref[...] += jnp.dot((_silu(gate) * up).astype(BF16), wd_ref[...], preferred_element_type=F32)

    @pl.when(f == pl.num_programs(1) - 1)
    def _():
        o_ref[...] = x_ref[...] + _rms(acc_ref[...], g_out_ref[...])


def dense_ffn(x, g_in, wg, wu, wd, g_out, tm=512, tf=512):
    n, d = x.shape
    ff = wg.shape[1]
    tm = min(tm, n)
    return pl.pallas_call(
        _ffn_body,
        grid=(n // tm, ff // tf),
        in_specs=[pl.BlockSpec((tm, d), lambda i, f: (i, 0)),
                  pl.BlockSpec((1, d), lambda i, f: (0, 0)),
                  pl.BlockSpec((d, tf), lambda i, f: (0, f)),
                  pl.BlockSpec((d, tf), lambda i, f: (0, f)),
                  pl.BlockSpec((tf, d), lambda i, f: (f, 0)),
                  pl.BlockSpec((1, d), lambda i, f: (0, 0))],
        out_specs=pl.BlockSpec((tm, d), lambda i, f: (i, 0)),
        out_shape=jax.ShapeDtypeStruct((n, d), F32),
        scratch_shapes=[pltpu.VMEM((tm, d), BF16), pltpu.VMEM((tm, d), F32)],
        compiler_params=_cparams(("arbitrary", "arbitrary")),
        name="dense_ffn",
    )(x, g_in.reshape(1, d), wg, wu, wd, g_out.reshape(1, d))


def _swa_body(start, q_ref, kv_ref, cos_ref, sin_ref, kprev_ref, vprev_ref, sink_ref,
              o_ref, krot_ref, kbuf, vbuf):
    c = pl.program_id(1)

    @pl.when(c == 0)
    def _():
        kbuf[0:WINDOW, :] = kprev_ref[0]
        vbuf[0:WINDOW, :] = vprev_ref[0]

    cos = cos_ref[...]
    sin = sin_ref[...]
    lane = lax.broadcasted_iota(jnp.int32, (CHUNK, LANES), 1)
    first_half = (lane % SWA_HD) < (SWA_HD // 2)

    def rope(x):
        outs = []
        for j in range(x.shape[1] // LANES):
            xb = x[:, LANES * j:LANES * (j + 1)]
            fwd = pltpu.roll(xb, LANES - SWA_HD // 2, 1)
            bwd = pltpu.roll(xb, SWA_HD // 2, 1)
            outs.append(xb * cos + jnp.where(first_half, fwd, bwd) * sin)
        return jnp.concatenate(outs, axis=-1)

    q = rope(q_ref[...])
    kv = kv_ref[...]
    k = rope(kv[:, :SWA_KV_W])
    krot_ref[...] = k
    kbuf[WINDOW:WINDOW + CHUNK, :] = k
    vbuf[WINDOW:WINDOW + CHUNK, :] = kv[:, SWA_KV_W:]

    nk = WINDOW + CHUNK
    rows = SWA_GROUP * CHUNK
    key_pos = start + c * CHUNK - WINDOW + lax.broadcasted_iota(jnp.int32, (rows, nk), 1)
    valid = key_pos >= 0
    row_head = lax.broadcasted_iota(jnp.int32, (rows, 1), 0) // CHUNK
    kvh = range(SWA_KV_HEADS)
    qg = [jnp.concatenate([q[:, SWA_HD * (hk * SWA_GROUP + gi):SWA_HD * (hk * SWA_GROUP + gi + 1)]
                           for gi in range(SWA_GROUP)], axis=0) for hk in kvh]
    kh = [kbuf[:, SWA_HD * hk:SWA_HD * (hk + 1)] for hk in kvh]
    vh = [vbuf[:, SWA_HD * hk:SWA_HD * (hk + 1)] for hk in kvh]
    sink = []
    for hk in kvh:
        col = jnp.full((rows, 1), sink_ref[hk * SWA_GROUP], F32)
        for gi in range(1, SWA_GROUP):
            col = jnp.where(row_head == gi, sink_ref[hk * SWA_GROUP + gi], col)
        sink.append(col)
    s = [jnp.where(valid, _dot_nt(qg[hk], kh[hk]) * (SWA_HD ** -0.5), -jnp.inf) for hk in kvh]
    m = [jnp.maximum(jnp.max(s[hk], axis=-1, keepdims=True), sink[hk]) for hk in kvh]
    p = [jnp.exp(s[hk] - m[hk]) for hk in kvh]
    denom = [jnp.sum(p[hk], axis=-1, keepdims=True) + jnp.exp(sink[hk] - m[hk]) for hk in kvh]
    og = [_dot(p[hk], vh[hk]) / denom[hk] for hk in kvh]
    for hk in kvh:
        for pair in range(SWA_GROUP // 2):
            lo_rows = og[hk][CHUNK * 2 * pair:CHUNK * (2 * pair + 1)]
            hi_rows = og[hk][CHUNK * (2 * pair + 1):CHUNK * (2 * pair + 2)]
            lane0 = SWA_HD * (hk * SWA_GROUP + 2 * pair)
            o_ref[:, lane0:lane0 + 2 * SWA_HD] = jnp.concatenate([lo_rows, hi_rows], axis=-1)

    kbuf[0:CHUNK, :] = kbuf[CHUNK:2 * CHUNK, :]
    kbuf[CHUNK:2 * CHUNK, :] = kbuf[2 * CHUNK:3 * CHUNK, :]
    vbuf[0:CHUNK, :] = vbuf[CHUNK:2 * CHUNK, :]
    vbuf[CHUNK:2 * CHUNK, :] = vbuf[2 * CHUNK:3 * CHUNK, :]


def _rope_tables(start, t):
    half = SWA_HD // 2
    inv_freq = jnp.exp(-math.log(ROPE_THETA) * jnp.arange(half, dtype=F32) / half)
    ang = (start + jnp.arange(t)).astype(F32)[:, None] * inv_freq[None, :]
    cos = jnp.cos(ang)
    sin = jnp.sin(ang)
    return jnp.tile(cos, (1, LANES // half)), jnp.tile(jnp.concatenate([-sin, sin], axis=-1), (1, LANES // SWA_HD))


def swa_mixer(p1, bsz, t, start, k_prev, v_prev, sinks):
    nc = t // CHUNK
    n = bsz * t
    qw = SWA_Q_HEADS * SWA_HD
    cos, sin = _rope_tables(start, t)
    return pl.pallas_call(
        functools.partial(_swa_body, start),
        grid=(bsz, nc),
        in_specs=[pl.BlockSpec((CHUNK, qw), lambda b, c: (b * nc + c, 0)),
                  pl.BlockSpec((CHUNK, 2 * SWA_KV_W), lambda b, c: (b * nc + c, 3 * qw // (2 * SWA_KV_W))),
                  pl.BlockSpec((CHUNK, LANES), lambda b, c: (c, 0)),
                  pl.BlockSpec((CHUNK, LANES), lambda b, c: (c, 0)),
                  pl.BlockSpec((1, WINDOW, SWA_KV_W), lambda b, c: (b, 0, 0)),
                  pl.BlockSpec((1, WINDOW, SWA_KV_W), lambda b, c: (b, 0, 0)),
                  pl.BlockSpec(memory_space=pltpu.SMEM)],
        out_specs=[pl.BlockSpec((CHUNK, qw), lambda b, c: (b * nc + c, 0)),
                   pl.BlockSpec((CHUNK, SWA_KV_W), lambda b, c: (b * nc + c, 0))],
        out_shape=[jax.ShapeDtypeStruct((n, qw), F32),
                   jax.ShapeDtypeStruct((n, SWA_KV_W), F32)],
        scratch_shapes=[pltpu.VMEM((WINDOW + CHUNK, SWA_KV_W), F32),
                        pltpu.VMEM((WINDOW + CHUNK, SWA_KV_W), F32)],
        compiler_params=_cparams(("arbitrary", "arbitrary")),
        name="swa_mixer",
    )(p1, p1, cos, sin, k_prev, v_prev, sinks)


def _smlp_body(lc, u_ref, v_ref, lg_ref, lb_ref, ws_ref, bs_ref, y_ref, vn_ref):
    v = _gelu(v_ref[...])
    mu = jnp.mean(v, axis=-1, keepdims=True)
    vc = v - mu
    vn = vc * lax.rsqrt(jnp.mean(vc * vc, axis=-1, keepdims=True) + EPS) * lg_ref[...] + lb_ref[...]
    vn_ref[...] = vn
    u = _gelu(u_ref[...])
    row = lax.broadcasted_iota(jnp.int32, (lc, lc), 0)
    col = lax.broadcasted_iota(jnp.int32, (lc, lc), 1)
    for g in range(SMLP_GROUPS):
        sl = slice(SMLP_GROUP_W * g, SMLP_GROUP_W * (g + 1))
        w = jnp.where(row >= col, ws_ref[g, 0:lc, 0:lc], 0.0)
        s = _dot(w, vn[:, sl]) + bs_ref[0:lc, g:g + 1]
        y_ref[:, sl] = u[:, sl] * s


def smlp_mixer(p1, bsz, t, ln_g, ln_b, w_spatial, b_spatial):
    lc = min(SMLP_CHUNK, t)
    n = bsz * t
    row = lambda v: v.reshape(1, SMLP_W)
    return pl.pallas_call(
        functools.partial(_smlp_body, lc),
        grid=(n // lc,),
        in_specs=[pl.BlockSpec((lc, SMLP_W), lambda i: (i, 1)),
                  pl.BlockSpec((lc, SMLP_W), lambda i: (i, 2)),
                  pl.BlockSpec((1, SMLP_W), lambda i: (0, 0)),
                  pl.BlockSpec((1, SMLP_W), lambda i: (0, 0)),
                  pl.BlockSpec((SMLP_GROUPS, SMLP_CHUNK, SMLP_CHUNK), lambda i: (0, 0, 0)),
                  pl.BlockSpec((SMLP_CHUNK, SMLP_GROUPS), lambda i: (0, 0))],
        out_specs=[pl.BlockSpec((lc, SMLP_W), lambda i: (i, 0)),
                   pl.BlockSpec((lc, SMLP_W), lambda i: (i, 0))],
        out_shape=[jax.ShapeDtypeStruct((n, SMLP_W), F32),
                   jax.ShapeDtypeStruct((n, SMLP_W), F32)],
        compiler_params=_cparams(("arbitrary",)),
        name="smlp_mixer",
    )(p1, p1, row(ln_g), row(ln_b), w_spatial, b_spatial.T)


def _router_body(x_ref, g_ref, wr_ref, idx_ref, gate_ref):
    hn = _rms(x_ref[...], g_ref[...])
    logits = lax.dot_general(wr_ref[...], hn, (((1,), (1,)), ((), ())), precision=HIGHEST,
                             preferred_element_type=F32)
    e_iota = lax.broadcasted_iota(jnp.int32, logits.shape, 0)
    m1 = jnp.max(logits, axis=0, keepdims=True)
    i1 = jnp.min(jnp.where(logits == m1, e_iota, N_EXPERTS), axis=0, keepdims=True)
    rest = jnp.where(e_iota == i1, -jnp.inf, logits)
    m2 = jnp.max(rest, axis=0, keepdims=True)
    i2 = jnp.min(jnp.where(rest == m2, e_iota, N_EXPERTS), axis=0, keepdims=True)
    e2 = jnp.exp(m2 - m1)
    den = 1.0 + e2
    idx_ref[...] = jnp.concatenate([i1, i2], axis=0)
    tm = logits.shape[1]
    gates = jnp.concatenate([1.0 / den, e2 / den, jnp.zeros((LANES - 2, tm), F32)], axis=0)
    gate_ref[...] = gates.T


def moe_router(x, g, w_router, tm=512):
    n, d = x.shape
    tm = min(tm, n)
    return pl.pallas_call(
        _router_body,
        grid=(n // tm,),
        in_specs=[pl.BlockSpec((tm, d), lambda i: (i, 0)),
                  pl.BlockSpec((1, d), lambda i: (0, 0)),
                  pl.BlockSpec((N_EXPERTS, d), lambda i: (0, 0))],
        out_specs=[pl.BlockSpec((2, tm), lambda i: (0, i)),
                   pl.BlockSpec((tm, LANES), lambda i: (i, 0))],
        out_shape=[jax.ShapeDtypeStruct((2, n), jnp.int32),
                   jax.ShapeDtypeStruct((n, LANES), F32)],
        compiler_params=_cparams(("arbitrary",)),
        name="moe_router",
    )(x, g.reshape(1, d), w_router.T)


def _moe_plan(top_idx):
    n = top_idx.shape[1]
    flat_e = top_idx.reshape(-1)
    onehot = (flat_e[:, None] == jnp.arange(N_EXPERTS, dtype=jnp.int32)[None, :]).astype(jnp.int32)
    rank = jnp.sum(jnp.cumsum(onehot, axis=0) * onehot, axis=1) - 1
    counts = jnp.sum(onehot, axis=0)
    padded = (counts + MOE_BM - 1) // MOE_BM * MOE_BM
    pad_end = jnp.cumsum(padded)
    pad_start = pad_end - padded
    dest = jnp.sum(onehot * pad_start[None, :], axis=1) + rank
    n_blk = -(-2 * n // MOE_BM) + N_EXPERTS
    blk_start = jnp.arange(n_blk, dtype=jnp.int32) * MOE_BM
    blk_e = jnp.minimum(jnp.sum((blk_start[:, None] >= pad_end[None, :]).astype(jnp.int32), axis=1),
                        N_EXPERTS - 1)
    blk_valid = jnp.clip((pad_start + counts)[blk_e] - blk_start, 0, MOE_BM).astype(jnp.int32)
    n_active = (pad_end[-1] // MOE_BM).astype(jnp.int32).reshape(1)
    return dest.reshape(2, n).astype(jnp.int32), blk_e.astype(jnp.int32), blk_valid, n_active, n_blk


def _moe_dispatch_body(tc, dest_ref, x_ref, g_ref, xs_hbm, hn_scr, sem):
    hn_scr[...] = _rms(x_ref[...], g_ref[...])

    def row_copy(r, slot):
        return pltpu.make_async_copy(hn_scr.at[pl.ds(r, 1), :], xs_hbm.at[pl.ds(dest_ref[0, slot, r], 1), :], sem)

    def start(r, carry):
        row_copy(r, 0).start()
        row_copy(r, 1).start()
        return carry

    lax.fori_loop(0, tc, start, 0)

    def wait(r, carry):
        row_copy(r, 0).wait()
        row_copy(r, 1).wait()
        return carry

    lax.fori_loop(0, tc, wait, 0)


def moe_dispatch(x, g, dest_blocks, n_rows, tc):
    n, d = x.shape
    return pl.pallas_call(
        functools.partial(_moe_dispatch_body, tc),
        grid=(n // tc,),
        in_specs=[pl.BlockSpec((1, 2, tc), lambda i: (i, 0, 0), memory_space=pltpu.SMEM),
                  pl.BlockSpec((tc, d), lambda i: (i, 0)),
                  pl.BlockSpec((1, d), lambda i: (0, 0))],
        out_specs=pl.BlockSpec(memory_space=pl.ANY),
        out_shape=jax.ShapeDtypeStruct((n_rows, d), F32),
        scratch_shapes=[pltpu.VMEM((tc, d), F32), pltpu.SemaphoreType.DMA(())],
        compiler_params=_cparams(("arbitrary",)),
        name="moe_dispatch",
    )(dest_blocks, x, g.reshape(1, d))


def _moe_ffn_body(blk_e_ref, blk_valid_ref, nact_ref, xs_ref, wg_ref, wu_ref, wd_ref, ys_ref, xb_scr, acc_scr):
    i = pl.program_id(0)
    f = pl.program_id(1)
    active = i < nact_ref[0]

    @pl.when(jnp.logical_and(active, f == 0))
    def _():
        row = lax.broadcasted_iota(jnp.int32, (MOE_BM, 1), 0)
        xb_scr[...] = jnp.where(row < blk_valid_ref[i], xs_ref[...], 0.0).astype(BF16)
        acc_scr[...] = jnp.zeros_like(acc_scr)

    @pl.when(active)
    def _():
        xb = xb_scr[...]
        gate = jnp.dot(xb, wg_ref[0], preferred_element_type=F32)
        up = jnp.dot(xb, wu_ref[0], preferred_element_type=F32)
        acc_scr[...] += jnp.dot((_silu(gate) * up).astype(BF16), wd_ref[0], preferred_element_type=F32)

    last = f == pl.num_programs(1) - 1

    @pl.when(jnp.logical_and(active, last))
    def _():
        ys_ref[...] = acc_scr[...]

    @pl.when(jnp.logical_and(jnp.logical_not(active), last))
    def _():
        ys_ref[...] = jnp.zeros_like(ys_ref)


def moe_expert_ffn(xs, blk_e, blk_valid, n_active, wg, wu, wd, tf=256):
    n_rows, d = xs.shape
    n_blk = n_rows // MOE_BM
    nf = FF_EXPERT // tf

    def x_blk(i, f, be, bv, na):
        return (jnp.minimum(i, na[0] - 1), 0)

    def w_col(i, f, be, bv, na):
        return (be[i], 0, jnp.where(i < na[0], f, nf - 1))

    def w_row(i, f, be, bv, na):
        return (be[i], jnp.where(i < na[0], f, nf - 1), 0)

    return pl.pallas_call(
        _moe_ffn_body,
        grid_spec=pltpu.PrefetchScalarGridSpec(
            num_scalar_prefetch=3,
            grid=(n_blk, nf),
            in_specs=[pl.BlockSpec((MOE_BM, d), x_blk),
                      pl.BlockSpec((1, d, tf), w_col),
                      pl.BlockSpec((1, d, tf), w_col),
                      pl.BlockSpec((1, tf, d), w_row)],
            out_specs=pl.BlockSpec((MOE_BM, d), lambda i, f, be, bv, na: (i, 0)),
            scratch_shapes=[pltpu.VMEM((MOE_BM, d), BF16), pltpu.VMEM((MOE_BM, d), F32)]),
        out_shape=jax.ShapeDtypeStruct((n_rows, d), F32),
        compiler_params=_cparams(("arbitrary", "arbitrary")),
        name="moe_expert_ffn",
    )(blk_e, blk_valid, n_active, xs, wg, wu, wd)


def _moe_combine_body(tc, pos_ref, pos_next_ref, ys_hbm, gate_ref, x_ref, g_ref, o_ref, buf, sem):
    i = pl.program_id(0)
    nb = pl.num_programs(0)
    slot = i % 2

    def issue(p_ref, s):
        def start(r, carry):
            for choice in range(2):
                pltpu.make_async_copy(ys_hbm.at[pl.ds(p_ref[0, choice, r], 1), :],
                                      buf.at[s, choice, pl.ds(r, 1), :], sem.at[s]).start()
            return carry

        lax.fori_loop(0, tc, start, 0)

    @pl.when(i == 0)
    def _():
        issue(pos_ref, 0)

    @pl.when(i + 1 < nb)
    def _():
        issue(pos_next_ref, 1 - slot)

    for choice in range(2):
        pltpu.make_async_copy(ys_hbm.at[pl.ds(0, tc), :], buf.at[slot, choice], sem.at[slot]).wait()
    gates = gate_ref[...]
    y = gates[:, 0:1] * buf[slot, 0] + gates[:, 1:2] * buf[slot, 1]
    o_ref[...] = x_ref[...] + _rms(y, g_ref[...])


def moe_combine(ys, pos_blocks, gates, x, g, tc):
    n, d = x.shape
    nb = n // tc
    return pl.pallas_call(
        functools.partial(_moe_combine_body, tc),
        grid=(nb,),
        in_specs=[pl.BlockSpec((1, 2, tc), lambda i: (i, 0, 0), memory_space=pltpu.SMEM),
                  pl.BlockSpec((1, 2, tc), lambda i: (jnp.minimum(i + 1, nb - 1), 0, 0), memory_space=pltpu.SMEM),
                  pl.BlockSpec(memory_space=pl.ANY),
                  pl.BlockSpec((tc, LANES), lambda i: (i, 0)),
                  pl.BlockSpec((tc, d), lambda i: (i, 0)),
                  pl.BlockSpec((1, d), lambda i: (0, 0))],
        out_specs=pl.BlockSpec((tc, d), lambda i: (i, 0)),
        out_shape=jax.ShapeDtypeStruct((n, d), F32),
        scratch_shapes=[pltpu.VMEM((2, 2, tc, d), F32), pltpu.SemaphoreType.DMA((2,))],
        compiler_params=_cparams(("arbitrary",)),
        name="moe_combine",
    )(pos_blocks, pos_blocks, ys, gates, x, g.reshape(1, d))


def moe_block(x, g_in, w_router, wg, wu, wd, g_out, tc=256):
    n = x.shape[0]
    tc = min(tc, n)
    top_idx, gates = moe_router(x, g_in, w_router)
    dest, blk_e, blk_valid, n_active, n_blk = _moe_plan(top_idx)
    dest_blocks = dest.reshape(2, n // tc, tc).transpose(1, 0, 2)
    xs = moe_dispatch(x, g_in, dest_blocks, n_blk * MOE_BM, tc)
    ys = moe_expert_ffn(xs, blk_e, blk_valid, n_active, wg, wu, wd)
    return moe_combine(ys, dest_blocks, gates, x, g_out, tc)


def _forward(x, start, keep, mem_k, mem_v, gdn_conv0, gdn_s0, lru_conv0, lru_h0, swa_k0, swa_v0, p):
    bsz, t, d = x.shape
    x = x.reshape(bsz * t, d)
    ng = p['norm_g']
    p0 = norm_matmul(x, ng[0, 0], p['w_in0'], 512, 1280)
    o_gdn, gdn_s, gdn_conv = gdn_mixer(p0, bsz, t, gdn_conv0, gdn_s0, p['gdn_conv_w'], p['gdn_a_log'],
                                       p['gdn_dt_bias'], p['gdn_norm_g'])
    y_lru, lru_h, lru_conv = lru_mixer(p0, bsz, t, lru_conv0, lru_h0, p['lru_conv_w'], p['lru_conv_b'],
                                       p['lru_w_a'], p['lru_b_a'], p['lru_w_x'], p['lru_b_x'], p['lru_lambda'])
    x = outproj_norm_resid(o_gdn, y_lru, p['w_out0'], x, ng[0, 1])
    x = cross_attention(x, bsz, t, mem_k[0], mem_v[0], p['w_xq'][0], p['w_xo'][0], ng[0, 2], ng[0, 3])
    x = dense_ffn(x, ng[0, 4], p['w_ff_gate'], p['w_ff_up'], p['w_ff_down'], ng[0, 5])
    p1 = norm_matmul(x, ng[1, 0], p['w_in1'], 512, 1792)
    attn, k_rot = swa_mixer(p1, bsz, t, start, swa_k0, swa_v0, p['swa_sinks'])
    y_smlp, smlp_v = smlp_mixer(p1, bsz, t, p['smlp_ln_g'], p['smlp_ln_b'], p['w_spatial'], p['b_spatial'])
    x = outproj_norm_resid(attn, y_smlp, p['w_out1'], x, ng[1, 1])
    x = cross_attention(x, bsz, t, mem_k[1], mem_v[1], p['w_xq'][1], p['w_xo'][1], ng[1, 2], ng[1, 3])
    x = moe_block(x, ng[1, 4], p['w_router'], p['w_moe_gate'], p['w_moe_up'], p['w_moe_down'], ng[1, 5])
    k_rows = k_rot.reshape(bsz, t, SWA_KV_W)[:, t - keep:].reshape(bsz, keep, SWA_KV_HEADS, SWA_HD)
    v_rows = p1.reshape(bsz, t, IN1_W)[:, t - keep:, IN1_W - SWA_KV_W:].reshape(bsz, keep, SWA_KV_HEADS, SWA_HD)
    return (x.reshape(bsz, t, d), gdn_conv, gdn_s, lru_conv, lru_h.reshape(bsz, LRU_W),
            k_rows, v_rows, smlp_v.reshape(bsz, t, SMLP_W))


def _prepare_weights(norm_g, w_in0, gdn_conv_w, gdn_a_log, gdn_dt_bias, gdn_norm_g, lru_conv_w, lru_conv_b,
                     lru_w_a, lru_b_a, lru_w_x, lru_b_x, lru_lambda, w_out0, w_in1, swa_sinks, smlp_ln_g,
                     smlp_ln_b, w_spatial, b_spatial, w_out1, w_xq, w_xo, w_ff_gate, w_ff_up, w_ff_down,
                     w_router, w_moe_gate, w_moe_up, w_moe_down):
    qkvz_w = GDN_QKV_W + GDN_HEADS * GDN_D
    bd_w = 2 * GDN_HEADS
    w0 = jnp.concatenate([w_in0[:, :qkvz_w], w_in0[:, qkvz_w + bd_w:], w_in0[:, qkvz_w:qkvz_w + bd_w],
                          jnp.zeros((D_MODEL, IN0_PAD_W - w_in0.shape[1]), w_in0.dtype)], axis=1)
    qw = SWA_Q_HEADS * SWA_HD
    w1 = jnp.concatenate([w_in1[:, :qw], w_in1[:, qw + 2 * SWA_KV_W:], w_in1[:, qw:qw + 2 * SWA_KV_W]], axis=1)
    return dict(
        norm_g=norm_g, w_in0=w0.astype(BF16), gdn_conv_w=gdn_conv_w, gdn_a_log=gdn_a_log,
        gdn_dt_bias=gdn_dt_bias, gdn_norm_g=gdn_norm_g, lru_conv_w=lru_conv_w, lru_conv_b=lru_conv_b,
        lru_w_a=lru_w_a, lru_b_a=lru_b_a, lru_w_x=lru_w_x, lru_b_x=lru_b_x, lru_lambda=lru_lambda,
        w_out0=w_out0.astype(BF16), w_in1=w1.astype(BF16), swa_sinks=swa_sinks, smlp_ln_g=smlp_ln_g,
        smlp_ln_b=smlp_ln_b, w_spatial=w_spatial, b_spatial=b_spatial, w_out1=w_out1.astype(BF16),
        w_xq=w_xq.astype(BF16), w_xo=w_xo.astype(BF16), w_ff_gate=w_ff_gate.astype(BF16),
        w_ff_up=w_ff_up.astype(BF16), w_ff_down=w_ff_down.astype(BF16), w_router=w_router,
        w_moe_gate=w_moe_gate.astype(BF16), w_moe_up=w_moe_up.astype(BF16), w_moe_down=w_moe_down.astype(BF16))


def kernel(x_prompt, x_sample, mem_prompt, cache_mem_k, cache_mem_v, state_gdn, state_gdn_conv, state_rglru_h, state_rglru_conv, cache_swa_k, cache_swa_v, norm_g, mem_norm_g, w_in0, gdn_conv_w, gdn_a_log, gdn_dt_bias, gdn_norm_g, lru_conv_w, lru_conv_b, lru_w_a, lru_b_a, lru_w_x, lru_b_x, lru_lambda, w_out0, w_in1, swa_sinks, smlp_ln_g, smlp_ln_b, w_spatial, b_spatial, w_out1, w_xq, w_xk, w_xv, w_xo, w_ff_gate, w_ff_up, w_ff_down, w_router, w_moe_gate, w_moe_up, w_moe_down):
    p = _prepare_weights(norm_g, w_in0, gdn_conv_w, gdn_a_log, gdn_dt_bias, gdn_norm_g, lru_conv_w, lru_conv_b,
                         lru_w_a, lru_b_a, lru_w_x, lru_b_x, lru_lambda, w_out0, w_in1, swa_sinks, smlp_ln_g,
                         smlp_ln_b, w_spatial, b_spatial, w_out1, w_xq, w_xo, w_ff_gate, w_ff_up, w_ff_down,
                         w_router, w_moe_gate, w_moe_up, w_moe_down)
    bsz, t, d = x_prompt.shape
    depth = w_xk.shape[0]
    mem_flat = mem_prompt.reshape(bsz * MEM_LEN, d)
    mem_k_p = jnp.stack([norm_matmul(mem_flat, mem_norm_g[l], w_xk[l].astype(BF16), 512, MEM_W)
                         for l in range(depth)]).reshape(depth, bsz, MEM_LEN, MEM_W)
    mem_v_p = jnp.stack([norm_matmul(mem_flat, mem_norm_g[l], w_xv[l].astype(BF16), 512, MEM_W)
                         for l in range(depth)]).reshape(depth, bsz, MEM_LEN, MEM_W)
    keep = min(WINDOW, t)
    (y_p, gdn_conv_p, gdn_s_p, lru_conv_p, lru_h_p, k_rows_p, v_rows_p, _) = _forward(
        x_prompt, 0, keep, mem_k_p, mem_v_p,
        jnp.zeros((bsz, CONV_W - 1, GDN_QKV_W), F32), jnp.zeros((bsz, GDN_HEADS, GDN_D, GDN_D), F32),
        jnp.zeros((bsz, CONV_W - 1, LRU_W), F32), jnp.zeros((bsz, LRU_W), F32),
        jnp.zeros((bsz, WINDOW, SWA_KV_W), F32), jnp.zeros((bsz, WINDOW, SWA_KV_W), F32), p)
    dbs, dec_t = x_sample.shape[:2]
    n_prev = cache_swa_k.shape[1]
    assert n_prev == WINDOW
    (y_s, gdn_conv_s, gdn_s_s, lru_conv_s, lru_h_s, k_rows_s, v_rows_s, smlp_v_s) = _forward(
        x_sample, PAST_LEN, dec_t, cache_mem_k.reshape(depth, dbs, MEM_LEN, MEM_W),
        cache_mem_v.reshape(depth, dbs, MEM_LEN, MEM_W), state_gdn_conv, state_gdn, state_rglru_conv,
        state_rglru_h, cache_swa_k.reshape(dbs, n_prev, SWA_KV_W), cache_swa_v.reshape(dbs, n_prev, SWA_KV_W), p)
    shape5 = (depth, bsz, MEM_LEN, MEM_HEADS, MEM_HD)
    return (y_p, y_s, mem_k_p.reshape(shape5), mem_v_p.reshape(shape5), gdn_s_p, gdn_conv_p, lru_h_p, lru_conv_p,
            k_rows_p, v_rows_p, gdn_s_s, gdn_conv_s, lru_h_s, lru_conv_s,
            k_rows_s, v_rows_s, smlp_v_s)
```

```python
import functools
import math

import jax
import jax.numpy as jnp
from jax import lax
from jax.experimental import pallas as pl
from jax.experimental.pallas import tpu as pltpu

F32 = jnp.float32
BF16 = jnp.bfloat16
HIGHEST = lax.Precision.HIGHEST

D_MODEL = 2048
EPS = 1e-6
CHUNK = 64
CONV_W = 4
CONV_PAD = 8
GDN_HEADS = 8
GDN_D = 128
GDN_QKV_W = 3 * GDN_HEADS * GDN_D
LRU_W = 1024
LRU_BLOCKS = 8
LRU_BLOCK_W = LRU_W // LRU_BLOCKS
LRU_C = 8.0
IN0_PAD_W = 6400
BD_COL_BLOCK = 6144 // 128
SWA_Q_HEADS = 16
SWA_KV_HEADS = 4
SWA_GROUP = SWA_Q_HEADS // SWA_KV_HEADS
SWA_HD = 64
SWA_KV_W = SWA_KV_HEADS * SWA_HD
WINDOW = 128
ROPE_THETA = 10000.0
PAST_LEN = 4096
SMLP_GROUPS = 8
SMLP_GROUP_W = 128
SMLP_W = SMLP_GROUPS * SMLP_GROUP_W
SMLP_CHUNK = 128
IN1_W = 3584
MEM_LEN = 256
MEM_HEADS = 4
MEM_HD = 128
MEM_W = MEM_HEADS * MEM_HD
FF_DENSE = 5632
N_EXPERTS = 8
FF_EXPERT = 2816
MOE_BM = 256
MOE_TF = FF_EXPERT // 2
LANES = 128
SUBLANES = 8

VMEM_LIMIT_MB = 56


def _cparams(semantics, vmem_mb=VMEM_LIMIT_MB):
    return pltpu.CompilerParams(dimension_semantics=semantics, vmem_limit_bytes=vmem_mb * 2 ** 20)


def _rms(x, g):
    return x * lax.rsqrt(jnp.mean(x * x, axis=-1, keepdims=True) + EPS) * g


def _sigmoid(x):
    return 1.0 / (1.0 + jnp.exp(-x))


def _silu(x):
    return x * _sigmoid(x)


def _softplus(x):
    return jnp.maximum(x, 0.0) + jnp.log(1.0 + jnp.exp(-jnp.abs(x)))


def _gelu(x):
    c = math.sqrt(2.0 / math.pi)
    return 0.5 * x * (1.0 + jnp.tanh(c * (x + 0.044715 * (x * x * x))))


def _dot(a, b):
    return jnp.dot(a.astype(BF16), b.astype(BF16), preferred_element_type=F32)


def _dot_nt(a, b):
    return lax.dot_general(a.astype(BF16), b.astype(BF16), (((1,), (1,)), ((), ())),
                           preferred_element_type=F32)


def _dot_tn(a, b):
    return lax.dot_general(a.astype(BF16), b.astype(BF16), (((0,), (0,)), ((), ())),
                           preferred_element_type=F32)


def _split3(x):
    x1 = x.astype(BF16)
    r1 = x - x1.astype(F32)
    x2 = r1.astype(BF16)
    x3 = (r1 - x2.astype(F32)).astype(BF16)
    return x1, x2, x3


def _norm_matmul_body(x_ref, g_ref, w_ref, o_ref, xn_ref):
    @pl.when(pl.program_id(1) == 0)
    def _():
        xn_ref[...] = _rms(x_ref[...], g_ref[...]).astype(BF16)

    o_ref[...] = jnp.dot(xn_ref[...], w_ref[...], preferred_element_type=F32)


def norm_matmul(x, g, w, tm, tn):
    n, k = x.shape
    nout = w.shape[1]
    tm = min(tm, n)
    return pl.pallas_call(
        _norm_matmul_body,
        grid=(n // tm, nout // tn),
        in_specs=[pl.BlockSpec((tm, k), lambda i, j: (i, 0)),
                  pl.BlockSpec((1, k), lambda i, j: (0, 0)),
                  pl.BlockSpec((k, tn), lambda i, j: (0, j))],
        out_specs=pl.BlockSpec((tm, tn), lambda i, j: (i, j)),
        out_shape=jax.ShapeDtypeStruct((n, nout), F32),
        scratch_shapes=[pltpu.VMEM((tm, k), BF16)],
        compiler_params=_cparams(("arbitrary", "arbitrary")),
        name="norm_matmul",
    )(x, g.reshape(1, k), w)


def _outproj_body(a_ref, b_ref, wa_ref, wb_ref, r_ref, g_ref, o_ref):
    acc = jnp.dot(a_ref[...].astype(BF16), wa_ref[...], preferred_element_type=F32)
    acc = acc + jnp.dot(b_ref[...].astype(BF16), wb_ref[...], preferred_element_type=F32)
    o_ref[...] = r_ref[...] + _rms(acc, g_ref[...])


def outproj_norm_resid(a, b, w, resid, g, tm=512):
    n, ka = a.shape
    kb = b.shape[1]
    d = w.shape[1]
    tm = min(tm, n)
    return pl.pallas_call(
        _outproj_body,
        grid=(n // tm,),
        in_specs=[pl.BlockSpec((tm, ka), lambda i: (i, 0)),
                  pl.BlockSpec((tm, kb), lambda i: (i, 0)),
                  pl.BlockSpec((ka, d), lambda i: (0, 0)),
                  pl.BlockSpec((kb, d), lambda i: (1, 0)),
                  pl.BlockSpec((tm, d), lambda i: (i, 0)),
                  pl.BlockSpec((1, d), lambda i: (0, 0))],
        out_specs=pl.BlockSpec((tm, d), lambda i: (i, 0)),
        out_shape=jax.ShapeDtypeStruct((n, d), F32),
        compiler_params=_cparams(("arbitrary",)),
        name="outproj_norm_resid",
    )(a, b, w, w, resid, g.reshape(1, d))


def _gdn_body(qkv_ref, z_ref, bd_ref, conv0_ref, s0_ref, cw_ref, pvec_ref, ng_ref,
              o_ref, sfin_ref, cfin_ref, s_scr, xbuf):
    c = pl.program_id(1)
    last = pl.num_programs(1) - 1
    lo = CONV_PAD - (CONV_W - 1)

    @pl.when(c == 0)
    def _():
        s_scr[...] = s0_ref[0]
        xbuf[lo:CONV_PAD, :] = conv0_ref[0]

    xbuf[CONV_PAD:CONV_PAD + CHUNK, :] = qkv_ref[...]
    y = xbuf[lo:lo + CHUNK, :] * cw_ref[0:1, :]
    for j in range(1, CONV_W):
        y = y + xbuf[lo + j:lo + j + CHUNK, :] * cw_ref[j:j + 1, :]
    tail = xbuf[CHUNK + lo:CHUNK + CONV_PAD, :]
    xbuf[lo:CONV_PAD, :] = tail

    @pl.when(c == last)
    def _():
        cfin_ref[0] = tail

    act = _silu(y)
    bd = bd_ref[...]
    beta = _sigmoid(bd)
    g_all = -jnp.exp(pvec_ref[0:1, :]) * _softplus(bd + pvec_ref[1:2, :])

    row = lax.broadcasted_iota(jnp.int32, (CHUNK, CHUNK), 0)
    col = lax.broadcasted_iota(jnp.int32, (CHUNK, CHUNK), 1)
    causal = row >= col
    strict = row > col
    blk_xor = row ^ col
    g_cum3 = jnp.dot(causal.astype(BF16), jnp.concatenate(_split3(g_all), axis=-1), preferred_element_type=F32)
    g_cum = g_cum3[:, :LANES] + g_cum3[:, LANES:2 * LANES] + g_cum3[:, 2 * LANES:]
    g_cum_t = g_cum.T

    heads = range(GDN_HEADS)
    gc = [g_cum[:, GDN_HEADS + h:GDN_HEADS + h + 1] for h in heads]
    gr = [g_cum_t[GDN_HEADS + h:GDN_HEADS + h + 1, :] for h in heads]
    decay = [jnp.where(causal, jnp.exp(jnp.where(causal, gc[h] - gr[h], 0.0)), 0.0) for h in heads]
    bcol = [beta[:, h:h + 1] for h in heads]
    hw = GDN_HEADS * GDN_D
    q = [act[:, GDN_D * h:GDN_D * (h + 1)] for h in heads]
    k = [act[:, hw + GDN_D * h:hw + GDN_D * (h + 1)] for h in heads]
    v = [act[:, 2 * hw + GDN_D * h:2 * hw + GDN_D * (h + 1)] for h in heads]
    q = [x * lax.rsqrt(jnp.sum(x * x, axis=-1, keepdims=True) + EPS) * (GDN_D ** -0.5) for x in q]
    k = [x * lax.rsqrt(jnp.sum(x * x, axis=-1, keepdims=True) + EPS) for x in k]
    kb = [k[h] * bcol[h] for h in heads]
    eg = [jnp.exp(gc[h]) for h in heads]
    qa = [_dot_nt(jnp.concatenate([q[h], kb[h]], axis=0), k[h]) for h in heads]
    qk = [qa[h][:CHUNK] * decay[h] for h in heads]
    a_low = [jnp.where(strict, qa[h][CHUNK:] * decay[h], 0.0) for h in heads]
    m = [jnp.where((blk_xor >> 2) == 0, -a_low[h], 0.0) for h in heads]
    m2 = [_dot(m[h], m[h]) for h in heads]
    n = [m[h] + m2[h] + _dot(m[h], m2[h]) for h in heads]
    for lg in range(2, 6):
        low = [jnp.where((blk_xor >> lg) == 1, a_low[h], 0.0) for h in heads]
        tl = [low[h] + _dot(n[h], low[h]) for h in heads]
        n = [n[h] - (tl[h] + _dot(tl[h], n[h])) for h in heads]
    rhs = [jnp.concatenate([v[h] * bcol[h], kb[h] * eg[h]], axis=-1) for h in heads]
    sol = [rhs[h] + _dot(n[h], rhs[h]) for h in heads]
    s = [s_scr[h] for h in heads]
    ws = [_dot(jnp.concatenate([sol[h][:, GDN_D:], q[h] * eg[h]], axis=0), s[h]) for h in heads]
    v_new = [sol[h][:, :GDN_D] - ws[h][:CHUNK] for h in heads]
    o = [ws[h][CHUNK:] + _dot(qk[h], v_new[h]) for h in heads]
    g_last = [g_cum[CHUNK - 1:CHUNK, GDN_HEADS + h:GDN_HEADS + h + 1] for h in heads]
    for h in heads:
        s_scr[h] = s[h] * jnp.exp(g_last[h]) + _dot_tn(k[h] * jnp.exp(g_last[h] - gc[h]), v_new[h])
    for h in heads:
        zh = z_ref[:, GDN_D * h:GDN_D * (h + 1)]
        o_ref[:, GDN_D * h:GDN_D * (h + 1)] = _rms(o[h], ng_ref[...]) * _silu(zh)

    @pl.when(c == last)
    def _():
        sfin_ref[0] = s_scr[...]


def gdn_mixer(p0, bsz, t, conv0, s0, conv_w, a_log, dt_bias, norm_g):
    nc = t // CHUNK
    n = bsz * t
    pvec = jnp.zeros((2, LANES), F32)
    pvec = pvec.at[0, GDN_HEADS:2 * GDN_HEADS].set(a_log).at[1, GDN_HEADS:2 * GDN_HEADS].set(dt_bias)
    vw = GDN_HEADS * GDN_D
    return pl.pallas_call(
        _gdn_body,
        grid=(bsz, nc),
        in_specs=[pl.BlockSpec((CHUNK, GDN_QKV_W), lambda b, c: (b * nc + c, 0)),
                  pl.BlockSpec((CHUNK, vw), lambda b, c: (b * nc + c, GDN_QKV_W // vw)),
                  pl.BlockSpec((CHUNK, LANES), lambda b, c: (b * nc + c, BD_COL_BLOCK)),
                  pl.BlockSpec((1, CONV_W - 1, GDN_QKV_W), lambda b, c: (b, 0, 0)),
                  pl.BlockSpec((1, GDN_HEADS, GDN_D, GDN_D), lambda b, c: (b, 0, 0, 0)),
                  pl.BlockSpec((CONV_W, GDN_QKV_W), lambda b, c: (0, 0)),
                  pl.BlockSpec((2, LANES), lambda b, c: (0, 0)),
                  pl.BlockSpec((1, GDN_D), lambda b, c: (0, 0))],
        out_specs=[pl.BlockSpec((CHUNK, vw), lambda b, c: (b * nc + c, 0)),
                   pl.BlockSpec((1, GDN_HEADS, GDN_D, GDN_D), lambda b, c: (b, 0, 0, 0)),
                   pl.BlockSpec((1, CONV_W - 1, GDN_QKV_W), lambda b, c: (b, 0, 0))],
        out_shape=[jax.ShapeDtypeStruct((n, vw), F32),
                   jax.ShapeDtypeStruct((bsz, GDN_HEADS, GDN_D, GDN_D), F32),
                   jax.ShapeDtypeStruct((bsz, CONV_W - 1, GDN_QKV_W), F32)],
        scratch_shapes=[pltpu.VMEM((GDN_HEADS, GDN_D, GDN_D), F32),
                        pltpu.VMEM((CONV_PAD + CHUNK, GDN_QKV_W), F32)],
        compiler_params=_cparams(("arbitrary", "arbitrary")),
        name="gdn_mixer",
    )(p0, p0, p0, conv0, s0, conv_w, pvec, norm_g.reshape(1, GDN_D))


def _lru_body(tl, x_ref, gate_ref, conv0_ref, h0_ref, cw_ref, cb_ref, wa_ref, ba_ref, wx_ref, bx_ref,
              lam_ref, y_ref, hfin_ref, cfin_ref, h_scr, xbuf, abuf, bbuf):
    c = pl.program_id(1)
    last = pl.num_programs(1) - 1
    lo = CONV_PAD - (CONV_W - 1)
    pad = tl // 2

    @pl.when(c == 0)
    def _():
        h_scr[...] = h0_ref[0]
        xbuf[lo:CONV_PAD, :] = conv0_ref[0]
        abuf[0:pad, :] = jnp.ones((pad, LRU_W), F32)
        bbuf[0:pad, :] = jnp.zeros((pad, LRU_W), F32)

    xbuf[CONV_PAD:CONV_PAD + tl, :] = x_ref[...]
    xr = xbuf[lo:lo + tl, :] * cw_ref[0:1, :]
    for j in range(1, CONV_W):
        xr = xr + xbuf[lo + j:lo + j + tl, :] * cw_ref[j:j + 1, :]
    tail = xbuf[tl + lo:tl + CONV_PAD, :]
    xbuf[lo:CONV_PAD, :] = tail

    @pl.when(c == last)
    def _():
        cfin_ref[0] = tail

    xr = xr + cb_ref[...]
    ga = jnp.concatenate([_dot(xr[:, LRU_BLOCK_W * n:LRU_BLOCK_W * (n + 1)], wa_ref[n])
                          for n in range(LRU_BLOCKS)], axis=-1)
    gx = jnp.concatenate([_dot(xr[:, LRU_BLOCK_W * n:LRU_BLOCK_W * (n + 1)], wx_ref[n])
                          for n in range(LRU_BLOCKS)], axis=-1)
    gate_a = _sigmoid(ga + ba_ref[...])
    gate_x = _sigmoid(gx + bx_ref[...])
    log_a = -LRU_C * gate_a * _softplus(-lam_ref[...])
    a = jnp.exp(log_a)
    b = jnp.sqrt(1.0 - jnp.exp(2.0 * log_a)) * gate_x * xr
    d = 1
    while d < tl:
        abuf[pad:pad + tl, :] = a
        bbuf[pad:pad + tl, :] = b
        a_sh = abuf[pad - d:pad - d + tl, :]
        b_sh = bbuf[pad - d:pad - d + tl, :]
        b = a * b_sh + b
        a = a * a_sh
        d *= 2
    h = a * h_scr[...] + b
    h_last = h[tl - 1:tl, :]
    h_scr[...] = h_last
    y_ref[...] = h * _gelu(gate_ref[...])

    @pl.when(c == last)
    def _():
        hfin_ref[0] = h_last


def lru_mixer(p0, bsz, t, conv0, h0, conv_w, conv_b, w_a, b_a, w_x, b_x, lam):
    tl = min(t, 256)
    nc = t // tl
    n = bsz * t
    row = lambda v: v.reshape(1, LRU_W)
    return pl.pallas_call(
        functools.partial(_lru_body, tl),
        grid=(bsz, nc),
        in_specs=[pl.BlockSpec((tl, LRU_W), lambda b, c: (b * nc + c, 4)),
                  pl.BlockSpec((tl, LRU_W), lambda b, c: (b * nc + c, 5)),
                  pl.BlockSpec((1, CONV_W - 1, LRU_W), lambda b, c: (b, 0, 0)),
                  pl.BlockSpec((1, 1, LRU_W), lambda b, c: (b, 0, 0)),
                  pl.BlockSpec((CONV_W, LRU_W), lambda b, c: (0, 0)),
                  pl.BlockSpec((1, LRU_W), lambda b, c: (0, 0)),
                  pl.BlockSpec((LRU_BLOCKS, LRU_BLOCK_W, LRU_BLOCK_W), lambda b, c: (0, 0, 0)),
                  pl.BlockSpec((1, LRU_W), lambda b, c: (0, 0)),
                  pl.BlockSpec((LRU_BLOCKS, LRU_BLOCK_W, LRU_BLOCK_W), lambda b, c: (0, 0, 0)),
                  pl.BlockSpec((1, LRU_W), lambda b, c: (0, 0)),
                  pl.BlockSpec((1, LRU_W), lambda b, c: (0, 0))],
        out_specs=[pl.BlockSpec((tl, LRU_W), lambda b, c: (b * nc + c, 0)),
                   pl.BlockSpec((1, 1, LRU_W), lambda b, c: (b, 0, 0)),
                   pl.BlockSpec((1, CONV_W - 1, LRU_W), lambda b, c: (b, 0, 0))],
        out_shape=[jax.ShapeDtypeStruct((n, LRU_W), F32),
                   jax.ShapeDtypeStruct((bsz, 1, LRU_W), F32),
                   jax.ShapeDtypeStruct((bsz, CONV_W - 1, LRU_W), F32)],
        scratch_shapes=[pltpu.VMEM((1, LRU_W), F32),
                        pltpu.VMEM((CONV_PAD + tl, LRU_W), F32),
                        pltpu.VMEM((tl // 2 + tl, LRU_W), F32),
                        pltpu.VMEM((tl // 2 + tl, LRU_W), F32)],
        compiler_params=_cparams(("arbitrary", "arbitrary")),
        name="lru_mixer",
    )(p0, p0, conv0, h0.reshape(bsz, 1, LRU_W), conv_w, row(conv_b), w_a, row(b_a), w_x, row(b_x), row(lam))


def _xattn_body(x_ref, mk_ref, mv_ref, wq_ref, wo_ref, g_in_ref, g_out_ref, o_ref):
    x = x_ref[...]
    q = jnp.dot(_rms(x, g_in_ref[...]).astype(BF16), wq_ref[...], preferred_element_type=F32)
    mk = mk_ref[0].astype(BF16)
    mv = mv_ref[0].astype(BF16)
    outs = []
    for h in range(MEM_HEADS):
        sl = slice(MEM_HD * h, MEM_HD * (h + 1))
        s = _dot_nt(q[:, sl], mk[:, sl]) * (MEM_HD ** -0.5)
        m = jnp.max(s, axis=-1, keepdims=True)
        p = jnp.exp(s - m)
        outs.append(_dot(p, mv[:, sl]) / jnp.sum(p, axis=-1, keepdims=True))
    o = jnp.concatenate(outs, axis=-1)
    y = jnp.dot(o.astype(BF16), wo_ref[...], preferred_element_type=F32)
    o_ref[...] = x + _rms(y, g_out_ref[...])


def cross_attention(x, bsz, t, mem_k, mem_v, wq, wo, g_in, g_out):
    tm = min(t, 512)
    nt = t // tm
    n, d = x.shape
    return pl.pallas_call(
        _xattn_body,
        grid=(bsz, nt),
        in_specs=[pl.BlockSpec((tm, d), lambda b, i: (b * nt + i, 0)),
                  pl.BlockSpec((1, MEM_LEN, MEM_W), lambda b, i: (b, 0, 0)),
                  pl.BlockSpec((1, MEM_LEN, MEM_W), lambda b, i: (b, 0, 0)),
                  pl.BlockSpec((d, MEM_W), lambda b, i: (0, 0)),
                  pl.BlockSpec((MEM_W, d), lambda b, i: (0, 0)),
                  pl.BlockSpec((1, d), lambda b, i: (0, 0)),
                  pl.BlockSpec((1, d), lambda b, i: (0, 0))],
        out_specs=pl.BlockSpec((tm, d), lambda b, i: (b * nt + i, 0)),
        out_shape=jax.ShapeDtypeStruct((n, d), F32),
        compiler_params=_cparams(("arbitrary", "arbitrary")),
        name="cross_attention",
    )(x, mem_k, mem_v, wq, wo, g_in.reshape(1, d), g_out.reshape(1, d))


def _ffn_body(x_ref, g_in_ref, wg_ref, wu_ref, wd_ref, g_out_ref, o_ref, xn_ref, acc_ref):
    f = pl.program_id(1)

    @pl.when(f == 0)
    def _():
        xn_ref[...] = _rms(x_ref[...], g_in_ref[...]).astype(BF16)
        acc_ref[...] = jnp.zeros_like(acc_ref)

    xn = xn_ref[...]
    gate = jnp.dot(xn, wg_ref[...], preferred_element_type=F32)
    up = jnp.dot(xn, wu_ref[...], preferred_element_type=F32)
    acc_ref[...] += jnp.dot((_silu(gate) * up).astype(BF16), wd_ref[...], preferred_element_type=F32)

    @pl.when(f == pl.num_programs(1) - 1)
    def _():
        o_ref[...] = x_ref[...] + _rms(acc_ref[...], g_out_ref[...])


def dense_ffn(x, g_in, wg, wu, wd, g_out, tm=512, tf=512):
    n, d = x.shape
    ff = wg.shape[1]
    tm = min(tm, n)
    return pl.pallas_call(
        _ffn_body,
        grid=(n // tm, ff // tf),
        in_specs=[pl.BlockSpec((tm, d), lambda i, f: (i, 0)),
                  pl.BlockSpec((1, d), lambda i, f: (0, 0)),
                  pl.BlockSpec((d, tf), lambda i, f: (0, f)),
                  pl.BlockSpec((d, tf), lambda i, f: (0, f)),
                  pl.BlockSpec((tf, d), lambda i, f: (f, 0)),
                  pl.BlockSpec((1, d), lambda i, f: (0, 0))],
        out_specs=pl.BlockSpec((tm, d), lambda i, f: (i, 0)),
        out_shape=jax.ShapeDtypeStruct((n, d), F32),
        scratch_shapes=[pltpu.VMEM((tm, d), BF16), pltpu.VMEM((tm, d), F32)],
        compiler_params=_cparams(("arbitrary", "arbitrary")),
        name="dense_ffn",
    )(x, g_in.reshape(1, d), wg, wu, wd, g_out.reshape(1, d))


def _swa_body(start, q_ref, kv_ref, cos_ref, sin_ref, kprev_ref, vprev_ref, sink_ref,
              o_ref, krot_ref, kbuf, vbuf):
    c = pl.program_id(1)

    @pl.when(c == 0)
    def _():
        kbuf[0:WINDOW, :] = kprev_ref[0]
        vbuf[0:WINDOW, :] = vprev_ref[0]

    cos = cos_ref[...]
    sin = sin_ref[...]
    lane = lax.broadcasted_iota(jnp.int32, (CHUNK, LANES), 1)
    first_half = (lane % SWA_HD) < (SWA_HD // 2)

    def rope(x):
        outs = []
        for j in range(x.shape[1] // LANES):
            xb = x[:, LANES * j:LANES * (j + 1)]
            fwd = pltpu.roll(xb, LANES - SWA_HD // 2, 1)
            bwd = pltpu.roll(xb, SWA_HD // 2, 1)
            outs.append(xb * cos + jnp.where(first_half, fwd, bwd) * sin)
        return jnp.concatenate(outs, axis=-1)

    q = rope(q_ref[...])
    kv = kv_ref[...]
    k = rope(kv[:, :SWA_KV_W])
    krot_ref[...] = k
    kbuf[WINDOW:WINDOW + CHUNK, :] = k
    vbuf[WINDOW:WINDOW + CHUNK, :] = kv[:, SWA_KV_W:]

    nk = WINDOW + CHUNK
    rows = SWA_GROUP * CHUNK
    key_pos = start + c * CHUNK - WINDOW + lax.broadcasted_iota(jnp.int32, (rows, nk), 1)
    valid = key_pos >= 0
    row_head = lax.broadcasted_iota(jnp.int32, (rows, 1), 0) // CHUNK
    kvh = range(SWA_KV_HEADS)
    qg = [jnp.concatenate([q[:, SWA_HD * (hk * SWA_GROUP + gi):SWA_HD * (hk * SWA_GROUP + gi + 1)]
                           for gi in range(SWA_GROUP)], axis=0) for hk in kvh]
    kh = [kbuf[:, SWA_HD * hk:SWA_HD * (hk + 1)] for hk in kvh]
    vh = [vbuf[:, SWA_HD * hk:SWA_HD * (hk + 1)] for hk in kvh]
    sink = []
    for hk in kvh:
        col = jnp.full((rows, 1), sink_ref[hk * SWA_GROUP], F32)
        for gi in range(1, SWA_GROUP):
            col = jnp.where(row_head == gi, sink_ref[hk * SWA_GROUP + gi], col)
        sink.append(col)
    s = [jnp.where(valid, _dot_nt(qg[hk], kh[hk]) * (SWA_HD ** -0.5), -jnp.inf) for hk in kvh]
    m = [jnp.maximum(jnp.max(s[hk], axis=-1, keepdims=True), sink[hk]) for hk in kvh]
    p = [jnp.exp(s[hk] - m[hk]) for hk in kvh]
    denom = [jnp.sum(p[hk], axis=-1, keepdims=True) + jnp.exp(sink[hk] - m[hk]) for hk in kvh]
    og = [_dot(p[hk], vh[hk]) / denom[hk] for hk in kvh]
    for hk in kvh:
        for pair in range(SWA_GROUP // 2):
            lo_rows = og[hk][CHUNK * 2 * pair:CHUNK * (2 * pair + 1)]
            hi_rows = og[hk][CHUNK * (2 * pair + 1):CHUNK * (2 * pair + 2)]
            lane0 = SWA_HD * (hk * SWA_GROUP + 2 * pair)
            o_ref[:, lane0:lane0 + 2 * SWA_HD] = jnp.concatenate([lo_rows, hi_rows], axis=-1)

    kbuf[0:CHUNK, :] = kbuf[CHUNK:2 * CHUNK, :]
    kbuf[CHUNK:2 * CHUNK, :] = kbuf[2 * CHUNK:3 * CHUNK, :]
    vbuf[0:CHUNK, :] = vbuf[CHUNK:2 * CHUNK, :]
    vbuf[CHUNK:2 * CHUNK, :] = vbuf[2 * CHUNK:3 * CHUNK, :]


def _rope_tables(start, t):
    half = SWA_HD // 2
    inv_freq = jnp.exp(-math.log(ROPE_THETA) * jnp.arange(half, dtype=F32) / half)
    ang = (start + jnp.arange(t)).astype(F32)[:, None] * inv_freq[None, :]
    cos = jnp.cos(ang)
    sin = jnp.sin(ang)
    return jnp.tile(cos, (1, LANES // half)), jnp.tile(jnp.concatenate([-sin, sin], axis=-1), (1, LANES // SWA_HD))


def swa_mixer(p1, bsz, t, start, k_prev, v_prev, sinks):
    nc = t // CHUNK
    n = bsz * t
    qw = SWA_Q_HEADS * SWA_HD
    cos, sin = _rope_tables(start, t)
    return pl.pallas_call(
        functools.partial(_swa_body, start),
        grid=(bsz, nc),
        in_specs=[pl.BlockSpec((CHUNK, qw), lambda b, c: (b * nc + c, 0)),
                  pl.BlockSpec((CHUNK, 2 * SWA_KV_W), lambda b, c: (b * nc + c, 3 * qw // (2 * SWA_KV_W))),
                  pl.BlockSpec((CHUNK, LANES), lambda b, c: (c, 0)),
                  pl.BlockSpec((CHUNK, LANES), lambda b, c: (c, 0)),
                  pl.BlockSpec((1, WINDOW, SWA_KV_W), lambda b, c: (b, 0, 0)),
                  pl.BlockSpec((1, WINDOW, SWA_KV_W), lambda b, c: (b, 0, 0)),
                  pl.BlockSpec(memory_space=pltpu.SMEM)],
        out_specs=[pl.BlockSpec((CHUNK, qw), lambda b, c: (b * nc + c, 0)),
                   pl.BlockSpec((CHUNK, SWA_KV_W), lambda b, c: (b * nc + c, 0))],
        out_shape=[jax.ShapeDtypeStruct((n, qw), F32),
                   jax.ShapeDtypeStruct((n, SWA_KV_W), F32)],
        scratch_shapes=[pltpu.VMEM((WINDOW + CHUNK, SWA_KV_W), F32),
                        pltpu.VMEM((WINDOW + CHUNK, SWA_KV_W), F32)],
        compiler_params=_cparams(("arbitrary", "arbitrary")),
        name="swa_mixer",
    )(p1, p1, cos, sin, k_prev, v_prev, sinks)


def _smlp_body(lc, u_ref, v_ref, lg_ref, lb_ref, ws_ref, bs_ref, y_ref, vn_ref):
    v = _gelu(v_ref[...])
    mu = jnp.mean(v, axis=-1, keepdims=True)
    vc = v - mu
    vn = vc * lax.rsqrt(jnp.mean(vc * vc, axis=-1, keepdims=True) + EPS) * lg_ref[...] + lb_ref[...]
    vn_ref[...] = vn
    u = _gelu(u_ref[...])
    row = lax.broadcasted_iota(jnp.int32, (lc, lc), 0)
    col = lax.broadcasted_iota(jnp.int32, (lc, lc), 1)
    for g in range(SMLP_GROUPS):
        sl = slice(SMLP_GROUP_W * g, SMLP_GROUP_W * (g + 1))
        w = jnp.where(row >= col, ws_ref[g, 0:lc, 0:lc], 0.0)
        s = _dot(w, vn[:, sl]) + bs_ref[0:lc, g:g + 1]
        y_ref[:, sl] = u[:, sl] * s


def smlp_mixer(p1, bsz, t, ln_g, ln_b, w_spatial, b_spatial):
    lc = min(SMLP_CHUNK, t)
    n = bsz * t
    row = lambda v: v.reshape(1, SMLP_W)
    return pl.pallas_call(
        functools.partial(_smlp_body, lc),
        grid=(n // lc,),
        in_specs=[pl.BlockSpec((lc, SMLP_W), lambda i: (i, 1)),
                  pl.BlockSpec((lc, SMLP_W), lambda i: (i, 2)),
                  pl.BlockSpec((1, SMLP_W), lambda i: (0, 0)),
                  pl.BlockSpec((1, SMLP_W), lambda i: (0, 0)),
                  pl.BlockSpec((SMLP_GROUPS, SMLP_CHUNK, SMLP_CHUNK), lambda i: (0, 0, 0)),
                  pl.BlockSpec((SMLP_CHUNK, SMLP_GROUPS), lambda i: (0, 0))],
        out_specs=[pl.BlockSpec((lc, SMLP_W), lambda i: (i, 0)),
                   pl.BlockSpec((lc, SMLP_W), lambda i: (i, 0))],
        out_shape=[jax.ShapeDtypeStruct((n, SMLP_W), F32),
                   jax.ShapeDtypeStruct((n, SMLP_W), F32)],
        compiler_params=_cparams(("arbitrary",)),
        name="smlp_mixer",
    )(p1, p1, row(ln_g), row(ln_b), w_spatial, b_spatial.T)


def _router_body(x_ref, g_ref, wr_ref, idx_ref, gate_ref):
    hn = _rms(x_ref[...], g_ref[...])
    logits = lax.dot_general(wr_ref[...], hn, (((1,), (1,)), ((), ())), precision=HIGHEST,
                             preferred_element_type=F32)
    e_iota = lax.broadcasted_iota(jnp.int32, logits.shape, 0)
    m1 = jnp.max(logits, axis=0, keepdims=True)
    i1 = jnp.min(jnp.where(logits == m1, e_iota, N_EXPERTS), axis=0, keepdims=True)
    rest = jnp.where(e_iota == i1, -jnp.inf, logits)
    m2 = jnp.max(rest, axis=0, keepdims=True)
    i2 = jnp.min(jnp.where(rest == m2, e_iota, N_EXPERTS), axis=0, keepdims=True)
    e2 = jnp.exp(m2 - m1)
    den = 1.0 + e2
    idx_ref[...] = jnp.concatenate([i1, i2], axis=0)
    tm = logits.shape[1]
    gates = jnp.concatenate([1.0 / den, e2 / den, jnp.zeros((LANES - 2, tm), F32)], axis=0)
    gate_ref[...] = gates.T


def moe_router(x, g, w_router, tm=512):
    n, d = x.shape
    tm = min(tm, n)
    return pl.pallas_call(
        _router_body,
        grid=(n // tm,),
        in_specs=[pl.BlockSpec((tm, d), lambda i: (i, 0)),
                  pl.BlockSpec((1, d), lambda i: (0, 0)),
                  pl.BlockSpec((N_EXPERTS, d), lambda i: (0, 0))],
        out_specs=[pl.BlockSpec((2, tm), lambda i: (0, i)),
                   pl.BlockSpec((tm, LANES), lambda i: (i, 0))],
        out_shape=[jax.ShapeDtypeStruct((2, n), jnp.int32),
                   jax.ShapeDtypeStruct((n, LANES), F32)],
        compiler_params=_cparams(("arbitrary",)),
        name="moe_router",
    )(x, g.reshape(1, d), w_router.T)


def _moe_plan(top_idx):
    n = top_idx.shape[1]
    flat_e = top_idx.reshape(-1)
    onehot = (flat_e[:, None] == jnp.arange(N_EXPERTS, dtype=jnp.int32)[None, :]).astype(jnp.int32)
    rank = jnp.sum(jnp.cumsum(onehot, axis=0) * onehot, axis=1) - 1
    counts = jnp.sum(onehot, axis=0)
    padded = (counts + MOE_BM - 1) // MOE_BM * MOE_BM
    pad_end = jnp.cumsum(padded)
    pad_start = pad_end - padded
    dest = jnp.sum(onehot * pad_start[None, :], axis=1) + rank
    n_blk = -(-2 * n // MOE_BM) + N_EXPERTS
    blk_start = jnp.arange(n_blk, dtype=jnp.int32) * MOE_BM
    blk_e = jnp.minimum(jnp.sum((blk_start[:, None] >= pad_end[None, :]).astype(jnp.int32), axis=1),
                        N_EXPERTS - 1)
    blk_valid = jnp.clip((pad_start + counts)[blk_e] - blk_start, 0, MOE_BM).astype(jnp.int32)
    n_active = (pad_end[-1] // MOE_BM).astype(jnp.int32).reshape(1)
    return dest.reshape(2, n).astype(jnp.int32), blk_e.astype(jnp.int32), blk_valid, n_active, n_blk


def _moe_dispatch_body(tc, dest_ref, x_ref, g_ref, xs_hbm, hn_scr, sem):
    hn_scr[...] = _rms(x_ref[...], g_ref[...])

    def row_copy(r, slot):
        return pltpu.make_async_copy(hn_scr.at[pl.ds(r, 1), :], xs_hbm.at[pl.ds(dest_ref[0, slot, r], 1), :], sem)

    def start(r, carry):
        row_copy(r, 0).start()
        row_copy(r, 1).start()
        return carry

    lax.fori_loop(0, tc, start, 0, unroll=8)
    for _ in range(2):
        pltpu.make_async_copy(hn_scr, xs_hbm.at[pl.ds(0, tc), :], sem).wait()


def moe_dispatch(x, g, dest_blocks, n_rows, tc):
    n, d = x.shape
    return pl.pallas_call(
        functools.partial(_moe_dispatch_body, tc),
        grid=(n // tc,),
        in_specs=[pl.BlockSpec((1, 2, tc), lambda i: (i, 0, 0), memory_space=pltpu.SMEM),
                  pl.BlockSpec((tc, d), lambda i: (i, 0)),
                  pl.BlockSpec((1, d), lambda i: (0, 0))],
        out_specs=pl.BlockSpec(memory_space=pl.ANY),
        out_shape=jax.ShapeDtypeStruct((n_rows, d), F32),
        scratch_shapes=[pltpu.VMEM((tc, d), F32), pltpu.SemaphoreType.DMA(())],
        compiler_params=_cparams(("arbitrary",)),
        name="moe_dispatch",
    )(dest_blocks, x, g.reshape(1, d))


def _moe_ffn_body(blk_e_ref, blk_valid_ref, nact_ref, xs_ref, wg_hbm, wu_hbm, wd_hbm, ys_ref,
                  wg_scr, wu_scr, wd_scr, sem):
    i = pl.program_id(0)
    active = i < nact_ref[0]
    e = blk_e_ref[i]
    new_expert = jnp.logical_or(i == 0, e != blk_e_ref[jnp.maximum(i - 1, 0)])

    @pl.when(jnp.logical_and(active, new_expert))
    def _():
        copies = [pltpu.make_async_copy(w_hbm.at[e], w_scr, sem.at[j])
                  for j, (w_hbm, w_scr) in enumerate(((wg_hbm, wg_scr), (wu_hbm, wu_scr), (wd_hbm, wd_scr)))]
        for cp in copies:
            cp.start()
        for cp in copies:
            cp.wait()

    @pl.when(active)
    def _():
        row = lax.broadcasted_iota(jnp.int32, (MOE_BM, 1), 0)
        xb = jnp.where(row < blk_valid_ref[i], xs_ref[...], 0.0).astype(BF16)
        acc = None
        for j in range(FF_EXPERT // MOE_TF):
            sl = slice(MOE_TF * j, MOE_TF * (j + 1))
            gate = jnp.dot(xb, wg_scr[:, sl], preferred_element_type=F32)
            up = jnp.dot(xb, wu_scr[:, sl], preferred_element_type=F32)
            part = jnp.dot((_silu(gate) * up).astype(BF16), wd_scr[sl, :], preferred_element_type=F32)
            acc = part if acc is None else acc + part
        ys_ref[...] = acc

    @pl.when(jnp.logical_not(active))
    def _():
        ys_ref[...] = jnp.zeros_like(ys_ref)


def moe_expert_ffn(xs, blk_e, blk_valid, n_active, wg, wu, wd):
    n_rows, d = xs.shape
    n_blk = n_rows // MOE_BM
    return pl.pallas_call(
        _moe_ffn_body,
        grid_spec=pltpu.PrefetchScalarGridSpec(
            num_scalar_prefetch=3,
            grid=(n_blk,),
            in_specs=[pl.BlockSpec((MOE_BM, d), lambda i, be, bv, na: (jnp.minimum(i, na[0] - 1), 0)),
                      pl.BlockSpec(memory_space=pl.ANY),
                      pl.BlockSpec(memory_space=pl.ANY),
                      pl.BlockSpec(memory_space=pl.ANY)],
            out_specs=pl.BlockSpec((MOE_BM, d), lambda i, be, bv, na: (i, 0)),
            scratch_shapes=[pltpu.VMEM((d, FF_EXPERT), BF16), pltpu.VMEM((d, FF_EXPERT), BF16),
                            pltpu.VMEM((FF_EXPERT, d), BF16), pltpu.SemaphoreType.DMA((3,))]),
        out_shape=jax.ShapeDtypeStruct((n_rows, d), F32),
        compiler_params=_cparams(("arbitrary",)),
        name="moe_expert_ffn",
    )(blk_e, blk_valid, n_active, xs, wg, wu, wd)


def _moe_combine_body(tc, pos_ref, pos_next_ref, ys_hbm, gate_ref, x_ref, g_ref, o_ref, buf, sem):
    i = pl.program_id(0)
    nb = pl.num_programs(0)
    slot = i % 2

    def issue(p_ref, s):
        def start(r, carry):
            for choice in range(2):
                pltpu.make_async_copy(ys_hbm.at[pl.ds(p_ref[0, choice, r], 1), :],
                                      buf.at[s, choice, pl.ds(r, 1), :], sem.at[s]).start()
            return carry

        lax.fori_loop(0, tc, start, 0, unroll=8)

    @pl.when(i == 0)
    def _():
        issue(pos_ref, 0)

    @pl.when(i + 1 < nb)
    def _():
        issue(pos_next_ref, 1 - slot)

    for choice in range(2):
        pltpu.make_async_copy(ys_hbm.at[pl.ds(0, tc), :], buf.at[slot, choice], sem.at[slot]).wait()
    gates = gate_ref[...]
    y = gates[:, 0:1] * buf[slot, 0] + gates[:, 1:2] * buf[slot, 1]
    o_ref[...] = x_ref[...] + _rms(y, g_ref[...])


def moe_combine(ys, pos_blocks, gates, x, g, tc):
    n, d = x.shape
    nb = n // tc
    return pl.pallas_call(
        functools.partial(_moe_combine_body, tc),
        grid=(nb,),
        in_specs=[pl.BlockSpec((1, 2, tc), lambda i: (i, 0, 0), memory_space=pltpu.SMEM),
                  pl.BlockSpec((1, 2, tc), lambda i: (jnp.minimum(i + 1, nb - 1), 0, 0), memory_space=pltpu.SMEM),
                  pl.BlockSpec(memory_space=pl.ANY),
                  pl.BlockSpec((tc, LANES), lambda i: (i, 0)),
                  pl.BlockSpec((tc, d), lambda i: (i, 0)),
                  pl.BlockSpec((1, d), lambda i: (0, 0))],
        out_specs=pl.BlockSpec((tc, d), lambda i: (i, 0)),
        out_shape=jax.ShapeDtypeStruct((n, d), F32),
        scratch_shapes=[pltpu.VMEM((2, 2, tc, d), F32), pltpu.SemaphoreType.DMA((2,))],
        compiler_params=_cparams(("arbitrary",)),
        name="moe_combine",
    )(pos_blocks, pos_blocks, ys, gates, x, g.reshape(1, d))


def moe_block(x, g_in, w_router, wg, wu, wd, g_out, tc=256):
    n = x.shape[0]
    tc = min(tc, n)
    top_idx, gates = moe_router(x, g_in, w_router)
    dest, blk_e, blk_valid, n_active, n_blk = _moe_plan(top_idx)
    dest_blocks = dest.reshape(2, n // tc, tc).transpose(1, 0, 2)
    xs = moe_dispatch(x, g_in, dest_blocks, n_blk * MOE_BM, tc)
    ys = moe_expert_ffn(xs, blk_e, blk_valid, n_active, wg, wu, wd)
    return moe_combine(ys, dest_blocks, gates, x, g_out, tc)


def _forward(x, start, keep, mem_k, mem_v, gdn_conv0, gdn_s0, lru_conv0, lru_h0, swa_k0, swa_v0, p):
    bsz, t, d = x.shape
    x = x.reshape(bsz * t, d)
    ng = p['norm_g']
    p0 = norm_matmul(x, ng[0, 0], p['w_in0'], 1024, 1280)
    o_gdn, gdn_s, gdn_conv = gdn_mixer(p0, bsz, t, gdn_conv0, gdn_s0, p['gdn_conv_w'], p['gdn_a_log'],
                                       p['gdn_dt_bias'], p['gdn_norm_g'])
    y_lru, lru_h, lru_conv = lru_mixer(p0, bsz, t, lru_conv0, lru_h0, p['lru_conv_w'], p['lru_conv_b'],
                                       p['lru_w_a'], p['lru_b_a'], p['lru_w_x'], p['lru_b_x'], p['lru_lambda'])
    x = outproj_norm_resid(o_gdn, y_lru, p['w_out0'], x, ng[0, 1])
    x = cross_attention(x, bsz, t, mem_k[0], mem_v[0], p['w_xq'][0], p['w_xo'][0], ng[0, 2], ng[0, 3])
    x = dense_ffn(x, ng[0, 4], p['w_ff_gate'], p['w_ff_up'], p['w_ff_down'], ng[0, 5])
    p1 = norm_matmul(x, ng[1, 0], p['w_in1'], 1024, 1792)
    attn, k_rot = swa_mixer(p1, bsz, t, start, swa_k0, swa_v0, p['swa_sinks'])
    y_smlp, smlp_v = smlp_mixer(p1, bsz, t, p['smlp_ln_g'], p['smlp_ln_b'], p['w_spatial'], p['b_spatial'])
    x = outproj_norm_resid(attn, y_smlp, p['w_out1'], x, ng[1, 1])
    x = cross_attention(x, bsz, t, mem_k[1], mem_v[1], p['w_xq'][1], p['w_xo'][1], ng[1, 2], ng[1, 3])
    x = moe_block(x, ng[1, 4], p['w_router'], p['w_moe_gate'], p['w_moe_up'], p['w_moe_down'], ng[1, 5])
    k_rows = k_rot.reshape(bsz, t, SWA_KV_W)[:, t - keep:].reshape(bsz, keep, SWA_KV_HEADS, SWA_HD)
    v_rows = p1.reshape(bsz, t, IN1_W)[:, t - keep:, IN1_W - SWA_KV_W:].reshape(bsz, keep, SWA_KV_HEADS, SWA_HD)
    return (x.reshape(bsz, t, d), gdn_conv, gdn_s, lru_conv, lru_h.reshape(bsz, LRU_W),
            k_rows, v_rows, smlp_v.reshape(bsz, t, SMLP_W))


def _prepare_weights(norm_g, w_in0, gdn_conv_w, gdn_a_log, gdn_dt_bias, gdn_norm_g, lru_conv_w, lru_conv_b,
                     lru_w_a, lru_b_a, lru_w_x, lru_b_x, lru_lambda, w_out0, w_in1, swa_sinks, smlp_ln_g,
                     smlp_ln_b, w_spatial, b_spatial, w_out1, w_xq, w_xo, w_ff_gate, w_ff_up, w_ff_down,
                     w_router, w_moe_gate, w_moe_up, w_moe_down):
    qkvz_w = GDN_QKV_W + GDN_HEADS * GDN_D
    bd_w = 2 * GDN_HEADS
    w0 = jnp.concatenate([w_in0[:, :qkvz_w], w_in0[:, qkvz_w + bd_w:], w_in0[:, qkvz_w:qkvz_w + bd_w],
                          jnp.zeros((D_MODEL, IN0_PAD_W - w_in0.shape[1]), w_in0.dtype)], axis=1)
    qw = SWA_Q_HEADS * SWA_HD
    w1 = jnp.concatenate([w_in1[:, :qw], w_in1[:, qw + 2 * SWA_KV_W:], w_in1[:, qw:qw + 2 * SWA_KV_W]], axis=1)
    return dict(
        norm_g=norm_g, w_in0=w0.astype(BF16), gdn_conv_w=gdn_conv_w, gdn_a_log=gdn_a_log,
        gdn_dt_bias=gdn_dt_bias, gdn_norm_g=gdn_norm_g, lru_conv_w=lru_conv_w, lru_conv_b=lru_conv_b,
        lru_w_a=lru_w_a, lru_b_a=lru_b_a, lru_w_x=lru_w_x, lru_b_x=lru_b_x, lru_lambda=lru_lambda,
        w_out0=w_out0.astype(BF16), w_in1=w1.astype(BF16), swa_sinks=swa_sinks, smlp_ln_g=smlp_ln_g,
        smlp_ln_b=smlp_ln_b, w_spatial=w_spatial, b_spatial=b_spatial, w_out1=w_out1.astype(BF16),
        w_xq=w_xq.astype(BF16), w_xo=w_xo.astype(BF16), w_ff_gate=w_ff_gate.astype(BF16),
        w_ff_up=w_ff_up.astype(BF16), w_ff_down=w_ff_down.astype(BF16), w_router=w_router,
        w_moe_gate=w_moe_gate.astype(BF16), w_moe_up=w_moe_up.astype(BF16), w_moe_down=w_moe_down.astype(BF16))


def kernel(x_prompt, x_sample, mem_prompt, cache_mem_k, cache_mem_v, state_gdn, state_gdn_conv, state_rglru_h, state_rglru_conv, cache_swa_k, cache_swa_v, norm_g, mem_norm_g, w_in0, gdn_conv_w, gdn_a_log, gdn_dt_bias, gdn_norm_g, lru_conv_w, lru_conv_b, lru_w_a, lru_b_a, lru_w_x, lru_b_x, lru_lambda, w_out0, w_in1, swa_sinks, smlp_ln_g, smlp_ln_b, w_spatial, b_spatial, w_out1, w_xq, w_xk, w_xv, w_xo, w_ff_gate, w_ff_up, w_ff_down, w_router, w_moe_gate, w_moe_up, w_moe_down):
    p = _prepare_weights(norm_g, w_in0, gdn_conv_w, gdn_a_log, gdn_dt_bias, gdn_norm_g, lru_conv_w, lru_conv_b,
                         lru_w_a, lru_b_a, lru_w_x, lru_b_x, lru_lambda, w_out0, w_in1, swa_sinks, smlp_ln_g,
                         smlp_ln_b, w_spatial, b_spatial, w_out1, w_xq, w_xo, w_ff_gate, w_ff_up, w_ff_down,
                         w_router, w_moe_gate, w_moe_up, w_moe_down)
    bsz, t, d = x_prompt.shape
    depth = w_xk.shape[0]
    mem_flat = mem_prompt.reshape(bsz * MEM_LEN, d)
    mem_k_p = jnp.stack([norm_matmul(mem_flat, mem_norm_g[l], w_xk[l].astype(BF16), 512, MEM_W)
                         for l in range(depth)]).reshape(depth, bsz, MEM_LEN, MEM_W)
    mem_v_p = jnp.stack([norm_matmul(mem_flat, mem_norm_g[l], w_xv[l].astype(BF16), 512, MEM_W)
                         for l in range(depth)]).reshape(depth, bsz, MEM_LEN, MEM_W)
    keep = min(WINDOW, t)
    (y_p, gdn_conv_p, gdn_s_p, lru_conv_p, lru_h_p, k_rows_p, v_rows_p, _) = _forward(
        x_prompt, 0, keep, mem_k_p, mem_v_p,
        jnp.zeros((bsz, CONV_W - 1, GDN_QKV_W), F32), jnp.zeros((bsz, GDN_HEADS, GDN_D, GDN_D), F32),
        jnp.zeros((bsz, CONV_W - 1, LRU_W), F32), jnp.zeros((bsz, LRU_W), F32),
        jnp.zeros((bsz, WINDOW, SWA_KV_W), F32), jnp.zeros((bsz, WINDOW, SWA_KV_W), F32), p)
    dbs, dec_t = x_sample.shape[:2]
    n_prev = cache_swa_k.shape[1]
    assert n_prev == WINDOW
    (y_s, gdn_conv_s, gdn_s_s, lru_conv_s, lru_h_s, k_rows_s, v_rows_s, smlp_v_s) = _forward(
        x_sample, PAST_LEN, dec_t, cache_mem_k.reshape(depth, dbs, MEM_LEN, MEM_W),
        cache_mem_v.reshape(depth, dbs, MEM_LEN, MEM_W), state_gdn_conv, state_gdn, state_rglru_conv,
        state_rglru_h, cache_swa_k.reshape(dbs, n_prev, SWA_KV_W), cache_swa_v.reshape(dbs, n_prev, SWA_KV_W), p)
    shape5 = (depth, bsz, MEM_LEN, MEM_HEADS, MEM_HD)
    return (y_p, y_s, mem_k_p.reshape(shape5), mem_v_p.reshape(shape5), gdn_s_p, gdn_conv_p, lru_h_p, lru_conv_p,
            k_rows_p, v_rows_p, gdn_s_s, gdn_conv_s, lru_h_s, lru_conv_s,
            k_rows_s, v_rows_s, smlp_v_s)
```

```python
import functools
import math

import jax
import jax.numpy as jnp
from jax import lax
from jax.experimental import pallas as pl
from jax.experimental.pallas import tpu as pltpu

F32 = jnp.float32
BF16 = jnp.bfloat16
HIGHEST = lax.Precision.HIGHEST

D_MODEL = 2048
EPS = 1e-6
CHUNK = 64
CONV_W = 4
CONV_PAD = 8
GDN_HEADS = 8
GDN_D = 128
GDN_QKV_W = 3 * GDN_HEADS * GDN_D
GDN_CHUNKS_PER_STEP = 2
LRU_W = 1024
LRU_BLOCKS = 8
LRU_BLOCK_W = LRU_W // LRU_BLOCKS
LRU_C = 8.0
IN0_PAD_W = 6400
BD_COL_BLOCK = 6144 // 128
SWA_Q_HEADS = 16
SWA_KV_HEADS = 4
SWA_GROUP = SWA_Q_HEADS // SWA_KV_HEADS
SWA_HD = 64
SWA_KV_W = SWA_KV_HEADS * SWA_HD
WINDOW = 128
SWA_CHUNKS_PER_STEP = 2
ROPE_THETA = 10000.0
PAST_LEN = 4096
SMLP_GROUPS = 8
SMLP_GROUP_W = 128
SMLP_W = SMLP_GROUPS * SMLP_GROUP_W
SMLP_CHUNK = 128
IN1_W = 3584
MEM_LEN = 256
MEM_HEADS = 4
MEM_HD = 128
MEM_W = MEM_HEADS * MEM_HD
FF_DENSE = 5632
N_EXPERTS = 8
FF_EXPERT = 2816
MOE_BM = 256
MOE_TF = FF_EXPERT // 2
LANES = 128
SUBLANES = 8

VMEM_LIMIT_MB = 56


def _cparams(semantics, vmem_mb=VMEM_LIMIT_MB):
    return pltpu.CompilerParams(dimension_semantics=semantics, vmem_limit_bytes=vmem_mb * 2 ** 20)


def _rms(x, g):
    return x * lax.rsqrt(jnp.mean(x * x, axis=-1, keepdims=True) + EPS) * g


def _sigmoid(x):
    return 1.0 / (1.0 + jnp.exp(-x))


def _silu(x):
    return x * _sigmoid(x)


def _softplus(x):
    return jnp.maximum(x, 0.0) + jnp.log(1.0 + jnp.exp(-jnp.abs(x)))


def _gelu(x):
    c = math.sqrt(2.0 / math.pi)
    return 0.5 * x * (1.0 + jnp.tanh(c * (x + 0.044715 * (x * x * x))))


def _dot(a, b):
    return jnp.dot(a.astype(BF16), b.astype(BF16), preferred_element_type=F32)


def _dot_nt(a, b):
    return lax.dot_general(a.astype(BF16), b.astype(BF16), (((1,), (1,)), ((), ())),
                           preferred_element_type=F32)


def _dot_tn(a, b):
    return lax.dot_general(a.astype(BF16), b.astype(BF16), (((0,), (0,)), ((), ())),
                           preferred_element_type=F32)


def _split3(x):
    x1 = x.astype(BF16)
    r1 = x - x1.astype(F32)
    x2 = r1.astype(BF16)
    x3 = (r1 - x2.astype(F32)).astype(BF16)
    return x1, x2, x3


def _norm_matmul_body(x_ref, g_ref, w_ref, o_ref, xn_ref):
    @pl.when(pl.program_id(1) == 0)
    def _():
        xn_ref[...] = _rms(x_ref[...], g_ref[...]).astype(BF16)

    o_ref[...] = jnp.dot(xn_ref[...], w_ref[...], preferred_element_type=F32)


def norm_matmul(x, g, w, tm, tn):
    n, k = x.shape
    nout = w.shape[1]
    tm = min(tm, n)
    return pl.pallas_call(
        _norm_matmul_body,
        grid=(n // tm, nout // tn),
        in_specs=[pl.BlockSpec((tm, k), lambda i, j: (i, 0)),
                  pl.BlockSpec((1, k), lambda i, j: (0, 0)),
                  pl.BlockSpec((k, tn), lambda i, j: (0, j))],
        out_specs=pl.BlockSpec((tm, tn), lambda i, j: (i, j)),
        out_shape=jax.ShapeDtypeStruct((n, nout), F32),
        scratch_shapes=[pltpu.VMEM((tm, k), BF16)],
        compiler_params=_cparams(("arbitrary", "arbitrary")),
        name="norm_matmul",
    )(x, g.reshape(1, k), w)


def _outproj_body(a_ref, b_ref, wa_ref, wb_ref, r_ref, g_ref, o_ref):
    acc = jnp.dot(a_ref[...].astype(BF16), wa_ref[...], preferred_element_type=F32)
    acc = acc + jnp.dot(b_ref[...].astype(BF16), wb_ref[...], preferred_element_type=F32)
    o_ref[...] = r_ref[...] + _rms(acc, g_ref[...])


def outproj_norm_resid(a, b, w, resid, g, tm=512):
    n, ka = a.shape
    kb = b.shape[1]
    d = w.shape[1]
    tm = min(tm, n)
    return pl.pallas_call(
        _outproj_body,
        grid=(n // tm,),
        in_specs=[pl.BlockSpec((tm, ka), lambda i: (i, 0)),
                  pl.BlockSpec((tm, kb), lambda i: (i, 0)),
                  pl.BlockSpec((ka, d), lambda i: (0, 0)),
                  pl.BlockSpec((kb, d), lambda i: (1, 0)),
                  pl.BlockSpec((tm, d), lambda i: (i, 0)),
                  pl.BlockSpec((1, d), lambda i: (0, 0))],
        out_specs=pl.BlockSpec((tm, d), lambda i: (i, 0)),
        out_shape=jax.ShapeDtypeStruct((n, d), F32),
        compiler_params=_cparams(("arbitrary",)),
        name="outproj_norm_resid",
    )(a, b, w, w, resid, g.reshape(1, d))


def _gdn_body(nsub, qkv_ref, z_ref, bd_ref, conv0_ref, s0_ref, cw_ref, pvec_ref, ng_ref,
              o_ref, sfin_ref, cfin_ref, s_scr, xbuf):
    c = pl.program_id(1)
    last = pl.num_programs(1) - 1
    lo = CONV_PAD - (CONV_W - 1)
    rows = nsub * CHUNK

    @pl.when(c == 0)
    def _():
        s_scr[...] = s0_ref[0]
        xbuf[lo:CONV_PAD, :] = conv0_ref[0]

    xbuf[CONV_PAD:CONV_PAD + rows, :] = qkv_ref[...]
    y = xbuf[lo:lo + rows, :] * cw_ref[0:1, :]
    for j in range(1, CONV_W):
        y = y + xbuf[lo + j:lo + j + rows, :] * cw_ref[j:j + 1, :]
    tail = xbuf[rows + lo:rows + CONV_PAD, :]
    xbuf[lo:CONV_PAD, :] = tail

    @pl.when(c == last)
    def _():
        cfin_ref[0] = tail

    act = _silu(y)
    bd = bd_ref[...]
    beta = _sigmoid(bd)
    g_all = -jnp.exp(pvec_ref[0:1, :]) * _softplus(bd + pvec_ref[1:2, :])

    row = lax.broadcasted_iota(jnp.int32, (CHUNK, CHUNK), 0)
    col = lax.broadcasted_iota(jnp.int32, (CHUNK, CHUNK), 1)
    causal = row >= col
    strict = row > col
    blk_xor = row ^ col
    rr = lax.broadcasted_iota(jnp.int32, (rows, rows), 0)
    cc = lax.broadcasted_iota(jnp.int32, (rows, rows), 1)
    tri = jnp.logical_and(rr >= cc, ((rr ^ cc) >> (CHUNK.bit_length() - 1)) == 0)
    g_cum3 = jnp.dot(tri.astype(BF16), jnp.concatenate(_split3(g_all), axis=-1), preferred_element_type=F32)
    g_cum = g_cum3[:, :LANES] + g_cum3[:, LANES:2 * LANES] + g_cum3[:, 2 * LANES:]
    g_cum_t = g_cum.T

    heads = range(GDN_HEADS)
    units = [(ci, h) for ci in range(nsub) for h in heads]
    hw = GDN_HEADS * GDN_D

    def rsl(ci):
        return slice(ci * CHUNK, (ci + 1) * CHUNK)

    gc = [g_cum[rsl(ci), GDN_HEADS + h:GDN_HEADS + h + 1] for ci, h in units]
    gr = [g_cum_t[GDN_HEADS + h:GDN_HEADS + h + 1, rsl(ci)] for ci, h in units]
    un = range(len(units))
    decay = [jnp.where(causal, jnp.exp(jnp.where(causal, gc[u] - gr[u], 0.0)), 0.0) for u in un]
    bcol = [beta[rsl(ci), h:h + 1] for ci, h in units]
    q = [act[rsl(ci), GDN_D * h:GDN_D * (h + 1)] for ci, h in units]
    k = [act[rsl(ci), hw + GDN_D * h:hw + GDN_D * (h + 1)] for ci, h in units]
    v = [act[rsl(ci), 2 * hw + GDN_D * h:2 * hw + GDN_D * (h + 1)] for ci, h in units]
    q = [x * lax.rsqrt(jnp.sum(x * x, axis=-1, keepdims=True) + EPS) * (GDN_D ** -0.5) for x in q]
    k = [x * lax.rsqrt(jnp.sum(x * x, axis=-1, keepdims=True) + EPS) for x in k]
    kb = [k[u] * bcol[u] for u in un]
    eg = [jnp.exp(gc[u]) for u in un]
    qa = [_dot_nt(jnp.concatenate([q[u], kb[u]], axis=0), k[u]) for u in un]
    qk = [qa[u][:CHUNK] * decay[u] for u in un]
    a_low = [jnp.where(strict, qa[u][CHUNK:] * decay[u], 0.0) for u in un]
    m = [jnp.where((blk_xor >> 2) == 0, -a_low[u], 0.0) for u in un]
    m2 = [_dot(m[u], m[u]) for u in un]
    n = [m[u] + m2[u] + _dot(m[u], m2[u]) for u in un]
    for lg in range(2, 6):
        low = [jnp.where((blk_xor >> lg) == 1, a_low[u], 0.0) for u in un]
        tl = [low[u] + _dot(n[u], low[u]) for u in un]
        n = [n[u] - (tl[u] + _dot(tl[u], n[u])) for u in un]
    rhs = [jnp.concatenate([v[u] * bcol[u], kb[u] * eg[u]], axis=-1) for u in un]
    sol = [rhs[u] + _dot(n[u], rhs[u]) for u in un]
    wq = [jnp.concatenate([sol[u][:, GDN_D:], q[u] * eg[u]], axis=0) for u in un]
    g_last = [g_cum[(ci + 1) * CHUNK - 1:(ci + 1) * CHUNK, GDN_HEADS + h:GDN_HEADS + h + 1] for ci, h in units]
    k_dec = [k[u] * jnp.exp(g_last[u] - gc[u]) for u in un]
    s = [s_scr[h] for h in heads]
    for ci in range(nsub):
        us = [ci * GDN_HEADS + h for h in heads]
        ws = [_dot(wq[us[h]], s[h]) for h in heads]
        v_new = [sol[us[h]][:, :GDN_D] - ws[h][:CHUNK] for h in heads]
        o = [ws[h][CHUNK:] + _dot(qk[us[h]], v_new[h]) for h in heads]
        s = [s[h] * jnp.exp(g_last[us[h]]) + _dot_tn(k_dec[us[h]], v_new[h]) for h in heads]
        for h in heads:
            zh = z_ref[rsl(ci), GDN_D * h:GDN_D * (h + 1)]
            o_ref[rsl(ci), GDN_D * h:GDN_D * (h + 1)] = _rms(o[h], ng_ref[...]) * _silu(zh)
    for h in heads:
        s_scr[h] = s[h]

    @pl.when(c == last)
    def _():
        sfin_ref[0] = s_scr[...]


def gdn_mixer(p0, bsz, t, conv0, s0, conv_w, a_log, dt_bias, norm_g):
    nsub = min(GDN_CHUNKS_PER_STEP, t // CHUNK)
    rows = nsub * CHUNK
    nc = t // rows
    n = bsz * t
    pvec = jnp.zeros((2, LANES), F32)
    pvec = pvec.at[0, GDN_HEADS:2 * GDN_HEADS].set(a_log).at[1, GDN_HEADS:2 * GDN_HEADS].set(dt_bias)
    vw = GDN_HEADS * GDN_D
    return pl.pallas_call(
        functools.partial(_gdn_body, nsub),
        grid=(bsz, nc),
        in_specs=[pl.BlockSpec((rows, GDN_QKV_W), lambda b, c: (b * nc + c, 0)),
                  pl.BlockSpec((rows, vw), lambda b, c: (b * nc + c, GDN_QKV_W // vw)),
                  pl.BlockSpec((rows, LANES), lambda b, c: (b * nc + c, BD_COL_BLOCK)),
                  pl.BlockSpec((1, CONV_W - 1, GDN_QKV_W), lambda b, c: (b, 0, 0)),
                  pl.BlockSpec((1, GDN_HEADS, GDN_D, GDN_D), lambda b, c: (b, 0, 0, 0)),
                  pl.BlockSpec((CONV_W, GDN_QKV_W), lambda b, c: (0, 0)),
                  pl.BlockSpec((2, LANES), lambda b, c: (0, 0)),
                  pl.BlockSpec((1, GDN_D), lambda b, c: (0, 0))],
        out_specs=[pl.BlockSpec((rows, vw), lambda b, c: (b * nc + c, 0)),
                   pl.BlockSpec((1, GDN_HEADS, GDN_D, GDN_D), lambda b, c: (b, 0, 0, 0)),
                   pl.BlockSpec((1, CONV_W - 1, GDN_QKV_W), lambda b, c: (b, 0, 0))],
        out_shape=[jax.ShapeDtypeStruct((n, vw), F32),
                   jax.ShapeDtypeStruct((bsz, GDN_HEADS, GDN_D, GDN_D), F32),
                   jax.ShapeDtypeStruct((bsz, CONV_W - 1, GDN_QKV_W), F32)],
        scratch_shapes=[pltpu.VMEM((GDN_HEADS, GDN_D, GDN_D), F32),
                        pltpu.VMEM((CONV_PAD + rows, GDN_QKV_W), F32)],
        compiler_params=_cparams(("arbitrary", "arbitrary")),
        name="gdn_mixer",
    )(p0, p0, p0, conv0, s0, conv_w, pvec, norm_g.reshape(1, GDN_D))


def _lru_body(tl, x_ref, gate_ref, conv0_ref, h0_ref, cw_ref, cb_ref, wa_ref, ba_ref, wx_ref, bx_ref,
              lam_ref, y_ref, hfin_ref, cfin_ref, h_scr, xbuf, abuf, bbuf):
    c = pl.program_id(1)
    last = pl.num_programs(1) - 1
    lo = CONV_PAD - (CONV_W - 1)
    pad = tl // 2

    @pl.when(c == 0)
    def _():
        h_scr[...] = h0_ref[0]
        xbuf[lo:CONV_PAD, :] = conv0_ref[0]
        abuf[0:pad, :] = jnp.ones((pad, LRU_W), F32)
        bbuf[0:pad, :] = jnp.zeros((pad, LRU_W), F32)

    xbuf[CONV_PAD:CONV_PAD + tl, :] = x_ref[...]
    xr = xbuf[lo:lo + tl, :] * cw_ref[0:1, :]
    for j in range(1, CONV_W):
        xr = xr + xbuf[lo + j:lo + j + tl, :] * cw_ref[j:j + 1, :]
    tail = xbuf[tl + lo:tl + CONV_PAD, :]
    xbuf[lo:CONV_PAD, :] = tail

    @pl.when(c == last)
    def _():
        cfin_ref[0] = tail

    xr = xr + cb_ref[...]
    ga = jnp.concatenate([_dot(xr[:, LRU_BLOCK_W * n:LRU_BLOCK_W * (n + 1)], wa_ref[n])
                          for n in range(LRU_BLOCKS)], axis=-1)
    gx = jnp.concatenate([_dot(xr[:, LRU_BLOCK_W * n:LRU_BLOCK_W * (n + 1)], wx_ref[n])
                          for n in range(LRU_BLOCKS)], axis=-1)
    gate_a = _sigmoid(ga + ba_ref[...])
    gate_x = _sigmoid(gx + bx_ref[...])
    log_a = -LRU_C * gate_a * _softplus(-lam_ref[...])
    a = jnp.exp(log_a)
    b = jnp.sqrt(1.0 - jnp.exp(2.0 * log_a)) * gate_x * xr
    d = 1
    while d < tl:
        abuf[pad:pad + tl, :] = a
        bbuf[pad:pad + tl, :] = b
        a_sh = abuf[pad - d:pad - d + tl, :]
        b_sh = bbuf[pad - d:pad - d + tl, :]
        b = a * b_sh + b
        a = a * a_sh
        d *= 2
    h = a * h_scr[...] + b
    h_last = h[tl - 1:tl, :]
    h_scr[...] = h_last
    y_ref[...] = h * _gelu(gate_ref[...])

    @pl.when(c == last)
    def _():
        hfin_ref[0] = h_last


def lru_mixer(p0, bsz, t, conv0, h0, conv_w, conv_b, w_a, b_a, w_x, b_x, lam):
    tl = min(t, 256)
    nc = t // tl
    n = bsz * t
    row = lambda v: v.reshape(1, LRU_W)
    return pl.pallas_call(
        functools.partial(_lru_body, tl),
        grid=(bsz, nc),
        in_specs=[pl.BlockSpec((tl, LRU_W), lambda b, c: (b * nc + c, 4)),
                  pl.BlockSpec((tl, LRU_W), lambda b, c: (b * nc + c, 5)),
                  pl.BlockSpec((1, CONV_W - 1, LRU_W), lambda b, c: (b, 0, 0)),
                  pl.BlockSpec((1, 1, LRU_W), lambda b, c: (b, 0, 0)),
                  pl.BlockSpec((CONV_W, LRU_W), lambda b, c: (0, 0)),
                  pl.BlockSpec((1, LRU_W), lambda b, c: (0, 0)),
                  pl.BlockSpec((LRU_BLOCKS, LRU_BLOCK_W, LRU_BLOCK_W), lambda b, c: (0, 0, 0)),
                  pl.BlockSpec((1, LRU_W), lambda b, c: (0, 0)),
                  pl.BlockSpec((LRU_BLOCKS, LRU_BLOCK_W, LRU_BLOCK_W), lambda b, c: (0, 0, 0)),
                  pl.BlockSpec((1, LRU_W), lambda b, c: (0, 0)),
                  pl.BlockSpec((1, LRU_W), lambda b, c: (0, 0))],
        out_specs=[pl.BlockSpec((tl, LRU_W), lambda b, c: (b * nc + c, 0)),
                   pl.BlockSpec((1, 1, LRU_W), lambda b, c: (b, 0, 0)),
                   pl.BlockSpec((1, CONV_W - 1, LRU_W), lambda b, c: (b, 0, 0))],
        out_shape=[jax.ShapeDtypeStruct((n, LRU_W), F32),
                   jax.ShapeDtypeStruct((bsz, 1, LRU_W), F32),
                   jax.ShapeDtypeStruct((bsz, CONV_W - 1, LRU_W), F32)],
        scratch_shapes=[pltpu.VMEM((1, LRU_W), F32),
                        pltpu.VMEM((CONV_PAD + tl, LRU_W), F32),
                        pltpu.VMEM((tl // 2 + tl, LRU_W), F32),
                        pltpu.VMEM((tl // 2 + tl, LRU_W), F32)],
        compiler_params=_cparams(("arbitrary", "arbitrary")),
        name="lru_mixer",
    )(p0, p0, conv0, h0.reshape(bsz, 1, LRU_W), conv_w, row(conv_b), w_a, row(b_a), w_x, row(b_x), row(lam))


def _xattn_body(x_ref, mk_ref, mv_ref, wq_ref, wo_ref, g_in_ref, g_out_ref, o_ref):
    x = x_ref[...]
    q = jnp.dot(_rms(x, g_in_ref[...]).astype(BF16), wq_ref[...], preferred_element_type=F32)
    mk = mk_ref[0].astype(BF16)
    mv = mv_ref[0].astype(BF16)
    outs = []
    for h in range(MEM_HEADS):
        sl = slice(MEM_HD * h, MEM_HD * (h + 1))
        s = _dot_nt(q[:, sl], mk[:, sl]) * (MEM_HD ** -0.5)
        m = jnp.max(s, axis=-1, keepdims=True)
        p = jnp.exp(s - m)
        outs.append(_dot(p, mv[:, sl]) / jnp.sum(p, axis=-1, keepdims=True))
    o = jnp.concatenate(outs, axis=-1)
    y = jnp.dot(o.astype(BF16), wo_ref[...], preferred_element_type=F32)
    o_ref[...] = x + _rms(y, g_out_ref[...])


def cross_attention(x, bsz, t, mem_k, mem_v, wq, wo, g_in, g_out):
    tm = min(t, 512)
    nt = t // tm
    n, d = x.shape
    return pl.pallas_call(
        _xattn_body,
        grid=(bsz, nt),
        in_specs=[pl.BlockSpec((tm, d), lambda b, i: (b * nt + i, 0)),
                  pl.BlockSpec((1, MEM_LEN, MEM_W), lambda b, i: (b, 0, 0)),
                  pl.BlockSpec((1, MEM_LEN, MEM_W), lambda b, i: (b, 0, 0)),
                  pl.BlockSpec((d, MEM_W), lambda b, i: (0, 0)),
                  pl.BlockSpec((MEM_W, d), lambda b, i: (0, 0)),
                  pl.BlockSpec((1, d), lambda b, i: (0, 0)),
                  pl.BlockSpec((1, d), lambda b, i: (0, 0))],
        out_specs=pl.BlockSpec((tm, d), lambda b, i: (b * nt + i, 0)),
        out_shape=jax.ShapeDtypeStruct((n, d), F32),
        compiler_params=_cparams(("arbitrary", "arbitrary")),
        name="cross_attention",
    )(x, mem_k, mem_v, wq, wo, g_in.reshape(1, d), g_out.reshape(1, d))


def _ffn_body(x_ref, g_in_ref, wg_ref, wu_ref, wd_ref, g_out_ref, o_ref, xn_ref, acc_ref):
    f = pl.program_id(1)

    @pl.when(f == 0)
    def _():
        xn_ref[...] = _rms(x_ref[...], g_in_ref[...]).astype(BF16)
        acc_ref[...] = jnp.zeros_like(acc_ref)

    xn = xn_ref[...]
    gate = jnp.dot(xn, wg_ref[...], preferred_element_type=F32)
    up = jnp.dot(xn, wu_ref[...], preferred_element_type=F32)
    acc_ref[...] += jnp.dot((_silu(gate) * up).astype(BF16), wd_ref[...], preferred_element_type=F32)

    @pl.when(f == pl.num_programs(1) - 1)
    def _():
        o_ref[...] = x_ref[...] + _rms(acc_ref[...], g_out_ref[...])


def dense_ffn(x, g_in, wg, wu, wd, g_out, tm=512, tf=512):
    n, d = x.shape
    ff = wg.shape[1]
    tm = min(tm, n)
    return pl.pallas_call(
        _ffn_body,
        grid=(n // tm, ff // tf),
        in_specs=[pl.BlockSpec((tm, d), lambda i, f: (i, 0)),
                  pl.BlockSpec((1, d), lambda i, f: (0, 0)),
                  pl.BlockSpec((d, tf), lambda i, f: (0, f)),
                  pl.BlockSpec((d, tf), lambda i, f: (0, f)),
                  pl.BlockSpec((tf, d), lambda i, f: (f, 0)),
                  pl.BlockSpec((1, d), lambda i, f: (0, 0))],
        out_specs=pl.BlockSpec((tm, d), lambda i, f: (i, 0)),
        out_shape=jax.ShapeDtypeStruct((n, d), F32),
        scratch_shapes=[pltpu.VMEM((tm, d), BF16), pltpu.VMEM((tm, d), F32)],
        compiler_params=_cparams(("arbitrary", "arbitrary")),
        name="dense_ffn",
    )(x, g_in.reshape(1, d), wg, wu, wd, g_out.reshape(1, d))


def _swa_body(start, nsub, q_ref, kv_ref, cos_ref, sin_ref, kprev_ref, vprev_ref, sink_ref,
              o_ref, krot_ref, kbuf, vbuf):
    c = pl.program_id(1)
    rows_t = nsub * CHUNK

    @pl.when(c == 0)
    def _():
        kbuf[0:WINDOW, :] = kprev_ref[0]
        vbuf[0:WINDOW, :] = vprev_ref[0]

    cos = cos_ref[...]
    sin = sin_ref[...]
    lane = lax.broadcasted_iota(jnp.int32, (rows_t, LANES), 1)
    first_half = (lane % SWA_HD) < (SWA_HD // 2)

    def rope(x):
        outs = []
        for j in range(x.shape[1] // LANES):
            xb = x[:, LANES * j:LANES * (j + 1)]
            fwd = pltpu.roll(xb, LANES - SWA_HD // 2, 1)
            bwd = pltpu.roll(xb, SWA_HD // 2, 1)
            outs.append(xb * cos + jnp.where(first_half, fwd, bwd) * sin)
        return jnp.concatenate(outs, axis=-1)

    q = rope(q_ref[...])
    kv = kv_ref[...]
    k = rope(kv[:, :SWA_KV_W])
    krot_ref[...] = k
    kbuf[WINDOW:WINDOW + rows_t, :] = k
    vbuf[WINDOW:WINDOW + rows_t, :] = kv[:, SWA_KV_W:]

    nk = WINDOW + CHUNK
    rows = SWA_GROUP * CHUNK
    key_off = lax.broadcasted_iota(jnp.int32, (rows, nk), 1) - WINDOW
    row_head = lax.broadcasted_iota(jnp.int32, (rows, 1), 0) // CHUNK
    kvh = range(SWA_KV_HEADS)
    units = [(ci, hk) for ci in range(nsub) for hk in kvh]
    un = range(len(units))
    valid = [start + (c * nsub + ci) * CHUNK + key_off >= 0 for ci in range(nsub)]
    qg = [jnp.concatenate([q[ci * CHUNK:(ci + 1) * CHUNK,
                             SWA_HD * (hk * SWA_GROUP + gi):SWA_HD * (hk * SWA_GROUP + gi + 1)]
                           for gi in range(SWA_GROUP)], axis=0) for ci, hk in units]
    kh = [kbuf[ci * CHUNK:ci * CHUNK + nk, SWA_HD * hk:SWA_HD * (hk + 1)] for ci, hk in units]
    vh = [vbuf[ci * CHUNK:ci * CHUNK + nk, SWA_HD * hk:SWA_HD * (hk + 1)] for ci, hk in units]
    sink_h = []
    for hk in kvh:
        col = jnp.full((rows, 1), sink_ref[hk * SWA_GROUP], F32)
        for gi in range(1, SWA_GROUP):
            col = jnp.where(row_head == gi, sink_ref[hk * SWA_GROUP + gi], col)
        sink_h.append(col)
    sink = [sink_h[hk] for _, hk in units]
    s = [jnp.where(valid[units[u][0]], _dot_nt(qg[u], kh[u]) * (SWA_HD ** -0.5), -jnp.inf) for u in un]
    m = [jnp.maximum(jnp.max(s[u], axis=-1, keepdims=True), sink[u]) for u in un]
    p = [jnp.exp(s[u] - m[u]) for u in un]
    denom = [jnp.sum(p[u], axis=-1, keepdims=True) + jnp.exp(sink[u] - m[u]) for u in un]
    og = [_dot(p[u], vh[u]) / denom[u] for u in un]
    for u, (ci, hk) in enumerate(units):
        for pair in range(SWA_GROUP // 2):
            lo_rows = og[u][CHUNK * 2 * pair:CHUNK * (2 * pair + 1)]
            hi_rows = og[u][CHUNK * (2 * pair + 1):CHUNK * (2 * pair + 2)]
            lane0 = SWA_HD * (hk * SWA_GROUP + 2 * pair)
            o_ref[ci * CHUNK:(ci + 1) * CHUNK, lane0:lane0 + 2 * SWA_HD] = jnp.concatenate(
                [lo_rows, hi_rows], axis=-1)

    k_keep = kbuf[rows_t:rows_t + WINDOW, :]
    v_keep = vbuf[rows_t:rows_t + WINDOW, :]
    kbuf[0:WINDOW, :] = k_keep
    vbuf[0:WINDOW, :] = v_keep


def _rope_tables(start, t):
    half = SWA_HD // 2
    inv_freq = jnp.exp(-math.log(ROPE_THETA) * jnp.arange(half, dtype=F32) / half)
    ang = (start + jnp.arange(t)).astype(F32)[:, None] * inv_freq[None, :]
    cos = jnp.cos(ang)
    sin = jnp.sin(ang)
    return jnp.tile(cos, (1, LANES // half)), jnp.tile(jnp.concatenate([-sin, sin], axis=-1), (1, LANES // SWA_HD))


def swa_mixer(p1, bsz, t, start, k_prev, v_prev, sinks):
    nsub = min(SWA_CHUNKS_PER_STEP, t // CHUNK)
    rows = nsub * CHUNK
    nc = t // rows
    n = bsz * t
    qw = SWA_Q_HEADS * SWA_HD
    cos, sin = _rope_tables(start, t)
    return pl.pallas_call(
        functools.partial(_swa_body, start, nsub),
        grid=(bsz, nc),
        in_specs=[pl.BlockSpec((rows, qw), lambda b, c: (b * nc + c, 0)),
                  pl.BlockSpec((rows, 2 * SWA_KV_W), lambda b, c: (b * nc + c, 3 * qw // (2 * SWA_KV_W))),
                  pl.BlockSpec((rows, LANES), lambda b, c: (c, 0)),
                  pl.BlockSpec((rows, LANES), lambda b, c: (c, 0)),
                  pl.BlockSpec((1, WINDOW, SWA_KV_W), lambda b, c: (b, 0, 0)),
                  pl.BlockSpec((1, WINDOW, SWA_KV_W), lambda b, c: (b, 0, 0)),
                  pl.BlockSpec(memory_space=pltpu.SMEM)],
        out_specs=[pl.BlockSpec((rows, qw), lambda b, c: (b * nc + c, 0)),
                   pl.BlockSpec((rows, SWA_KV_W), lambda b, c: (b * nc + c, 0))],
        out_shape=[jax.ShapeDtypeStruct((n, qw), F32),
                   jax.ShapeDtypeStruct((n, SWA_KV_W), F32)],
        scratch_shapes=[pltpu.VMEM((WINDOW + rows, SWA_KV_W), F32),
                        pltpu.VMEM((WINDOW + rows, SWA_KV_W), F32)],
        compiler_params=_cparams(("arbitrary", "arbitrary")),
        name="swa_mixer",
    )(p1, p1, cos, sin, k_prev, v_prev, sinks)


def _smlp_body(lc, u_ref, v_ref, lg_ref, lb_ref, ws_ref, bs_ref, y_ref, vn_ref):
    v = _gelu(v_ref[...])
    mu = jnp.mean(v, axis=-1, keepdims=True)
    vc = v - mu
    vn = vc * lax.rsqrt(jnp.mean(vc * vc, axis=-1, keepdims=True) + EPS) * lg_ref[...] + lb_ref[...]
    vn_ref[...] = vn
    u = _gelu(u_ref[...])
    row = lax.broadcasted_iota(jnp.int32, (lc, lc), 0)
    col = lax.broadcasted_iota(jnp.int32, (lc, lc), 1)
    for g in range(SMLP_GROUPS):
        sl = slice(SMLP_GROUP_W * g, SMLP_GROUP_W * (g + 1))
        w = jnp.where(row >= col, ws_ref[g, 0:lc, 0:lc], 0.0)
        s = _dot(w, vn[:, sl]) + bs_ref[0:lc, g:g + 1]
        y_ref[:, sl] = u[:, sl] * s


def smlp_mixer(p1, bsz, t, ln_g, ln_b, w_spatial, b_spatial):
    lc = min(SMLP_CHUNK, t)
    n = bsz * t
    row = lambda v: v.reshape(1, SMLP_W)
    return pl.pallas_call(
        functools.partial(_smlp_body, lc),
        grid=(n // lc,),
        in_specs=[pl.BlockSpec((lc, SMLP_W), lambda i: (i, 1)),
                  pl.BlockSpec((lc, SMLP_W), lambda i: (i, 2)),
                  pl.BlockSpec((1, SMLP_W), lambda i: (0, 0)),
                  pl.BlockSpec((1, SMLP_W), lambda i: (0, 0)),
                  pl.BlockSpec((SMLP_GROUPS, SMLP_CHUNK, SMLP_CHUNK), lambda i: (0, 0, 0)),
                  pl.BlockSpec((SMLP_CHUNK, SMLP_GROUPS), lambda i: (0, 0))],
        out_specs=[pl.BlockSpec((lc, SMLP_W), lambda i: (i, 0)),
                   pl.BlockSpec((lc, SMLP_W), lambda i: (i, 0))],
        out_shape=[jax.ShapeDtypeStruct((n, SMLP_W), F32),
                   jax.ShapeDtypeStruct((n, SMLP_W), F32)],
        compiler_params=_cparams(("arbitrary",)),
        name="smlp_mixer",
    )(p1, p1, row(ln_g), row(ln_b), w_spatial, b_spatial.T)


def _router_body(x_ref, g_ref, wr_ref, idx_ref, gate_ref):
    hn = _rms(x_ref[...], g_ref[...])
    logits = lax.dot_general(wr_ref[...], hn, (((1,), (1,)), ((), ())), precision=HIGHEST,
                             preferred_element_type=F32)
    e_iota = lax.broadcasted_iota(jnp.int32, logits.shape, 0)
    m1 = jnp.max(logits, axis=0, keepdims=True)
    i1 = jnp.min(jnp.where(logits == m1, e_iota, N_EXPERTS), axis=0, keepdims=True)
    rest = jnp.where(e_iota == i1, -jnp.inf, logits)
    m2 = jnp.max(rest, axis=0, keepdims=True)
    i2 = jnp.min(jnp.where(rest == m2, e_iota, N_EXPERTS), axis=0, keepdims=True)
    e2 = jnp.exp(m2 - m1)
    den = 1.0 + e2
    idx_ref[...] = jnp.concatenate([i1, i2], axis=0)
    tm = logits.shape[1]
    gates = jnp.concatenate([1.0 / den, e2 / den, jnp.zeros((LANES - 2, tm), F32)], axis=0)
    gate_ref[...] = gates.T


def moe_router(x, g, w_router, tm=512):
    n, d = x.shape
    tm = min(tm, n)
    return pl.pallas_call(
        _router_body,
        grid=(n // tm,),
        in_specs=[pl.BlockSpec((tm, d), lambda i: (i, 0)),
                  pl.BlockSpec((1, d), lambda i: (0, 0)),
                  pl.BlockSpec((N_EXPERTS, d), lambda i: (0, 0))],
        out_specs=[pl.BlockSpec((2, tm), lambda i: (0, i)),
                   pl.BlockSpec((tm, LANES), lambda i: (i, 0))],
        out_shape=[jax.ShapeDtypeStruct((2, n), jnp.int32),
                   jax.ShapeDtypeStruct((n, LANES), F32)],
        compiler_params=_cparams(("arbitrary",)),
        name="moe_router",
    )(x, g.reshape(1, d), w_router.T)


def _moe_plan(top_idx):
    n = top_idx.shape[1]
    flat_e = top_idx.reshape(-1)
    onehot = (flat_e[:, None] == jnp.arange(N_EXPERTS, dtype=jnp.int32)[None, :]).astype(jnp.int32)
    rank = jnp.sum(jnp.cumsum(onehot, axis=0) * onehot, axis=1) - 1
    counts = jnp.sum(onehot, axis=0)
    padded = (counts + MOE_BM - 1) // MOE_BM * MOE_BM
    pad_end = jnp.cumsum(padded)
    pad_start = pad_end - padded
    dest = jnp.sum(onehot * pad_start[None, :], axis=1) + rank
    n_blk = -(-2 * n // MOE_BM) + N_EXPERTS
    blk_start = jnp.arange(n_blk, dtype=jnp.int32) * MOE_BM
    blk_e = jnp.minimum(jnp.sum((blk_start[:, None] >= pad_end[None, :]).astype(jnp.int32), axis=1),
                        N_EXPERTS - 1)
    blk_valid = jnp.clip((pad_start + counts)[blk_e] - blk_start, 0, MOE_BM).astype(jnp.int32)
    n_active = (pad_end[-1] // MOE_BM).astype(jnp.int32).reshape(1)
    return dest.reshape(2, n).astype(jnp.int32), blk_e.astype(jnp.int32), blk_valid, n_active, n_blk


def _moe_dispatch_body(tc, dest_ref, x_ref, g_ref, xs_hbm, hn_scr, sem):
    hn_scr[...] = _rms(x_ref[...], g_ref[...])

    def row_copy(r, slot):
        return pltpu.make_async_copy(hn_scr.at[pl.ds(r, 1), :], xs_hbm.at[pl.ds(dest_ref[0, slot, r], 1), :], sem)

    def start(r, carry):
        row_copy(r, 0).start()
        row_copy(r, 1).start()
        return carry

    lax.fori_loop(0, tc, start, 0, unroll=8)
    for _ in range(2):
        pltpu.make_async_copy(hn_scr, xs_hbm.at[pl.ds(0, tc), :], sem).wait()


def moe_dispatch(x, g, dest_blocks, n_rows, tc):
    n, d = x.shape
    return pl.pallas_call(
        functools.partial(_moe_dispatch_body, tc),
        grid=(n // tc,),
        in_specs=[pl.BlockSpec((1, 2, tc), lambda i: (i, 0, 0), memory_space=pltpu.SMEM),
                  pl.BlockSpec((tc, d), lambda i: (i, 0)),
                  pl.BlockSpec((1, d), lambda i: (0, 0))],
        out_specs=pl.BlockSpec(memory_space=pl.ANY),
        out_shape=jax.ShapeDtypeStruct((n_rows, d), F32),
        scratch_shapes=[pltpu.VMEM((tc, d), F32), pltpu.SemaphoreType.DMA(())],
        compiler_params=_cparams(("arbitrary",)),
        name="moe_dispatch",
    )(dest_blocks, x, g.reshape(1, d))


def _moe_ffn_body(blk_e_ref, blk_valid_ref, nact_ref, xs_ref, wg_hbm, wu_hbm, wd_hbm, ys_ref,
                  wg_scr, wu_scr, wd_scr, sem):
    i = pl.program_id(0)
    active = i < nact_ref[0]
    e = blk_e_ref[i]
    new_expert = jnp.logical_or(i == 0, e != blk_e_ref[jnp.maximum(i - 1, 0)])

    @pl.when(jnp.logical_and(active, new_expert))
    def _():
        copies = [pltpu.make_async_copy(w_hbm.at[e], w_scr, sem.at[j])
                  for j, (w_hbm, w_scr) in enumerate(((wg_hbm, wg_scr), (wu_hbm, wu_scr), (wd_hbm, wd_scr)))]
        for cp in copies:
            cp.start()
        for cp in copies:
            cp.wait()

    @pl.when(active)
    def _():
        row = lax.broadcasted_iota(jnp.int32, (MOE_BM, 1), 0)
        xb = jnp.where(row < blk_valid_ref[i], xs_ref[...], 0.0).astype(BF16)
        acc = None
        for j in range(FF_EXPERT // MOE_TF):
            sl = slice(MOE_TF * j, MOE_TF * (j + 1))
            gate = jnp.dot(xb, wg_scr[:, sl], preferred_element_type=F32)
            up = jnp.dot(xb, wu_scr[:, sl], preferred_element_type=F32)
            part = jnp.dot((_silu(gate) * up).astype(BF16), wd_scr[sl, :], preferred_element_type=F32)
            acc = part if acc is None else acc + part
        ys_ref[...] = acc

    @pl.when(jnp.logical_not(active))
    def _():
        ys_ref[...] = jnp.zeros_like(ys_ref)


def moe_expert_ffn(xs, blk_e, blk_valid, n_active, wg, wu, wd):
    n_rows, d = xs.shape
    n_blk = n_rows // MOE_BM
    return pl.pallas_call(
        _moe_ffn_body,
        grid_spec=pltpu.PrefetchScalarGridSpec(
            num_scalar_prefetch=3,
            grid=(n_blk,),
            in_specs=[pl.BlockSpec((MOE_BM, d), lambda i, be, bv, na: (jnp.minimum(i, na[0] - 1), 0)),
                      pl.BlockSpec(memory_space=pl.ANY),
                      pl.BlockSpec(memory_space=pl.ANY),
                      pl.BlockSpec(memory_space=pl.ANY)],
            out_specs=pl.BlockSpec((MOE_BM, d), lambda i, be, bv, na: (i, 0)),
            scratch_shapes=[pltpu.VMEM((d, FF_EXPERT), BF16), pltpu.VMEM((d, FF_EXPERT), BF16),
                            pltpu.VMEM((FF_EXPERT, d), BF16), pltpu.SemaphoreType.DMA((3,))]),
        out_shape=jax.ShapeDtypeStruct((n_rows, d), F32),
        compiler_params=_cparams(("arbitrary",)),
        name="moe_expert_ffn",
    )(blk_e, blk_valid, n_active, xs, wg, wu, wd)


def _moe_combine_body(tc, pos_ref, pos_next_ref, ys_hbm, gate_ref, x_ref, g_ref, o_ref, buf, sem):
    i = pl.program_id(0)
    nb = pl.num_programs(0)
    slot = i % 2

    def issue(p_ref, s):
        def start(r, carry):
            for choice in range(2):
                pltpu.make_async_copy(ys_hbm.at[pl.ds(p_ref[0, choice, r], 1), :],
                                      buf.at[s, choice, pl.ds(r, 1), :], sem.at[s]).start()
            return carry

        lax.fori_loop(0, tc, start, 0, unroll=8)

    @pl.when(i == 0)
    def _():
        issue(pos_ref, 0)

    @pl.when(i + 1 < nb)
    def _():
        issue(pos_next_ref, 1 - slot)

    for choice in range(2):
        pltpu.make_async_copy(ys_hbm.at[pl.ds(0, tc), :], buf.at[slot, choice], sem.at[slot]).wait()
    gates = gate_ref[...]
    y = gates[:, 0:1] * buf[slot, 0] + gates[:, 1:2] * buf[slot, 1]
    o_ref[...] = x_ref[...] + _rms(y, g_ref[...])


def moe_combine(ys, pos_blocks, gates, x, g, tc):
    n, d = x.shape
    nb = n // tc
    return pl.pallas_call(
        functools.partial(_moe_combine_body, tc),
        grid=(nb,),
        in_specs=[pl.BlockSpec((1, 2, tc), lambda i: (i, 0, 0), memory_space=pltpu.SMEM),
                  pl.BlockSpec((1, 2, tc), lambda i: (jnp.minimum(i + 1, nb - 1), 0, 0), memory_space=pltpu.SMEM),
                  pl.BlockSpec(memory_space=pl.ANY),
                  pl.BlockSpec((tc, LANES), lambda i: (i, 0)),
                  pl.BlockSpec((tc, d), lambda i: (i, 0)),
                  pl.BlockSpec((1, d), lambda i: (0, 0))],
        out_specs=pl.BlockSpec((tc, d), lambda i: (i, 0)),
        out_shape=jax.ShapeDtypeStruct((n, d), F32),
        scratch_shapes=[pltpu.VMEM((2, 2, tc, d), F32), pltpu.SemaphoreType.DMA((2,))],
        compiler_params=_cparams(("arbitrary",)),
        name="moe_combine",
    )(pos_blocks, pos_blocks, ys, gates, x, g.reshape(1, d))


def moe_block(x, g_in, w_router, wg, wu, wd, g_out, tc=256):
    n = x.shape[0]
    tc = min(tc, n)
    top_idx, gates = moe_router(x, g_in, w_router)
    dest, blk_e, blk_valid, n_active, n_blk = _moe_plan(top_idx)
    dest_blocks = dest.reshape(2, n // tc, tc).transpose(1, 0, 2)
    xs = moe_dispatch(x, g_in, dest_blocks, n_blk * MOE_BM, tc)
    ys = moe_expert_ffn(xs, blk_e, blk_valid, n_active, wg, wu, wd)
    return moe_combine(ys, dest_blocks, gates, x, g_out, tc)


def _forward(x, start, keep, mem_k, mem_v, gdn_conv0, gdn_s0, lru_conv0, lru_h0, swa_k0, swa_v0, p):
    bsz, t, d = x.shape
    x = x.reshape(bsz * t, d)
    ng = p['norm_g']
    p0 = norm_matmul(x, ng[0, 0], p['w_in0'], 1024, 1280)
    o_gdn, gdn_s, gdn_conv = gdn_mixer(p0, bsz, t, gdn_conv0, gdn_s0, p['gdn_conv_w'], p['gdn_a_log'],
                                       p['gdn_dt_bias'], p['gdn_norm_g'])
    y_lru, lru_h, lru_conv = lru_mixer(p0, bsz, t, lru_conv0, lru_h0, p['lru_conv_w'], p['lru_conv_b'],
                                       p['lru_w_a'], p['lru_b_a'], p['lru_w_x'], p['lru_b_x'], p['lru_lambda'])
    x = outproj_norm_resid(o_gdn, y_lru, p['w_out0'], x, ng[0, 1])
    x = cross_attention(x, bsz, t, mem_k[0], mem_v[0], p['w_xq'][0], p['w_xo'][0], ng[0, 2], ng[0, 3])
    x = dense_ffn(x, ng[0, 4], p['w_ff_gate'], p['w_ff_up'], p['w_ff_down'], ng[0, 5])
    p1 = norm_matmul(x, ng[1, 0], p['w_in1'], 1024, 1792)
    attn, k_rot = swa_mixer(p1, bsz, t, start, swa_k0, swa_v0, p['swa_sinks'])
    y_smlp, smlp_v = smlp_mixer(p1, bsz, t, p['smlp_ln_g'], p['smlp_ln_b'], p['w_spatial'], p['b_spatial'])
    x = outproj_norm_resid(attn, y_smlp, p['w_out1'], x, ng[1, 1])
    x = cross_attention(x, bsz, t, mem_k[1], mem_v[1], p['w_xq'][1], p['w_xo'][1], ng[1, 2], ng[1, 3])
    x = moe_block(x, ng[1, 4], p['w_router'], p['w_moe_gate'], p['w_moe_up'], p['w_moe_down'], ng[1, 5])
    k_rows = k_rot.reshape(bsz, t, SWA_KV_W)[:, t - keep:].reshape(bsz, keep, SWA_KV_HEADS, SWA_HD)
    v_rows = p1.reshape(bsz, t, IN1_W)[:, t - keep:, IN1_W - SWA_KV_W:].reshape(bsz, keep, SWA_KV_HEADS, SWA_HD)
    return (x.reshape(bsz, t, d), gdn_conv, gdn_s, lru_conv, lru_h.reshape(bsz, LRU_W),
            k_rows, v_rows, smlp_v.reshape(bsz, t, SMLP_W))


def _prepare_weights(norm_g, w_in0, gdn_conv_w, gdn_a_log, gdn_dt_bias, gdn_norm_g, lru_conv_w, lru_conv_b,
                     lru_w_a, lru_b_a, lru_w_x, lru_b_x, lru_lambda, w_out0, w_in1, swa_sinks, smlp_ln_g,
                     smlp_ln_b, w_spatial, b_spatial, w_out1, w_xq, w_xo, w_ff_gate, w_ff_up, w_ff_down,
                     w_router, w_moe_gate, w_moe_up, w_moe_down):
    qkvz_w = GDN_QKV_W + GDN_HEADS * GDN_D
    bd_w = 2 * GDN_HEADS
    w0 = jnp.concatenate([w_in0[:, :qkvz_w], w_in0[:, qkvz_w + bd_w:], w_in0[:, qkvz_w:qkvz_w + bd_w],
                          jnp.zeros((D_MODEL, IN0_PAD_W - w_in0.shape[1]), w_in0.dtype)], axis=1)
    qw = SWA_Q_HEADS * SWA_HD
    w1 = jnp.concatenate([w_in1[:, :qw], w_in1[:, qw + 2 * SWA_KV_W:], w_in1[:, qw:qw + 2 * SWA_KV_W]], axis=1)
    return dict(
        norm_g=norm_g, w_in0=w0.astype(BF16), gdn_conv_w=gdn_conv_w, gdn_a_log=gdn_a_log,
        gdn_dt_bias=gdn_dt_bias, gdn_norm_g=gdn_norm_g, lru_conv_w=lru_conv_w, lru_conv_b=lru_conv_b,
        lru_w_a=lru_w_a, lru_b_a=lru_b_a, lru_w_x=lru_w_x, lru_b_x=lru_b_x, lru_lambda=lru_lambda,
        w_out0=w_out0.astype(BF16), w_in1=w1.astype(BF16), swa_sinks=swa_sinks, smlp_ln_g=smlp_ln_g,
        smlp_ln_b=smlp_ln_b, w_spatial=w_spatial, b_spatial=b_spatial, w_out1=w_out1.astype(BF16),
        w_xq=w_xq.astype(BF16), w_xo=w_xo.astype(BF16), w_ff_gate=w_ff_gate.astype(BF16),
        w_ff_up=w_ff_up.astype(BF16), w_ff_down=w_ff_down.astype(BF16), w_router=w_router,
        w_moe_gate=w_moe_gate.astype(BF16), w_moe_up=w_moe_up.astype(BF16), w_moe_down=w_moe_down.astype(BF16))


def kernel(x_prompt, x_sample, mem_prompt, cache_mem_k, cache_mem_v, state_gdn, state_gdn_conv, state_rglru_h, state_rglru_conv, cache_swa_k, cache_swa_v, norm_g, mem_norm_g, w_in0, gdn_conv_w, gdn_a_log, gdn_dt_bias, gdn_norm_g, lru_conv_w, lru_conv_b, lru_w_a, lru_b_a, lru_w_x, lru_b_x, lru_lambda, w_out0, w_in1, swa_sinks, smlp_ln_g, smlp_ln_b, w_spatial, b_spatial, w_out1, w_xq, w_xk, w_xv, w_xo, w_ff_gate, w_ff_up, w_ff_down, w_router, w_moe_gate, w_moe_up, w_moe_down):
    p = _prepare_weights(norm_g, w_in0, gdn_conv_w, gdn_a_log, gdn_dt_bias, gdn_norm_g, lru_conv_w, lru_conv_b,
                         lru_w_a, lru_b_a, lru_w_x, lru_b_x, lru_lambda, w_out0, w_in1, swa_sinks, smlp_ln_g,
                         smlp_ln_b, w_spatial, b_spatial, w_out1, w_xq, w_xo, w_ff_gate, w_ff_up, w_ff_down,
                         w_router, w_moe_gate, w_moe_up, w_moe_down)
    bsz, t, d = x_prompt.shape
    depth = w_xk.shape[0]
    mem_flat = mem_prompt.reshape(bsz * MEM_LEN, d)
    mem_k_p = jnp.stack([norm_matmul(mem_flat, mem_norm_g[l], w_xk[l].astype(BF16), 512, MEM_W)
                         for l in range(depth)]).reshape(depth, bsz, MEM_LEN, MEM_W)
    mem_v_p = jnp.stack([norm_matmul(mem_flat, mem_norm_g[l], w_xv[l].astype(BF16), 512, MEM_W)
                         for l in range(depth)]).reshape(depth, bsz, MEM_LEN, MEM_W)
    keep = min(WINDOW, t)
    (y_p, gdn_conv_p, gdn_s_p, lru_conv_p, lru_h_p, k_rows_p, v_rows_p, _) = _forward(
        x_prompt, 0, keep, mem_k_p, mem_v_p,
        jnp.zeros((bsz, CONV_W - 1, GDN_QKV_W), F32), jnp.zeros((bsz, GDN_HEADS, GDN_D, GDN_D), F32),
        jnp.zeros((bsz, CONV_W - 1, LRU_W), F32), jnp.zeros((bsz, LRU_W), F32),
        jnp.zeros((bsz, WINDOW, SWA_KV_W), F32), jnp.zeros((bsz, WINDOW, SWA_KV_W), F32), p)
    dbs, dec_t = x_sample.shape[:2]
    n_prev = cache_swa_k.shape[1]
    assert n_prev == WINDOW
    (y_s, gdn_conv_s, gdn_s_s, lru_conv_s, lru_h_s, k_rows_s, v_rows_s, smlp_v_s) = _forward(
        x_sample, PAST_LEN, dec_t, cache_mem_k.reshape(depth, dbs, MEM_LEN, MEM_W),
        cache_mem_v.reshape(depth, dbs, MEM_LEN, MEM_W), state_gdn_conv, state_gdn, state_rglru_conv,
        state_rglru_h, cache_swa_k.reshape(dbs, n_prev, SWA_KV_W), cache_swa_v.reshape(dbs, n_prev, SWA_KV_W), p)
    shape5 = (depth, bsz, MEM_LEN, MEM_HEADS, MEM_HD)
    return (y_p, y_s, mem_k_p.reshape(shape5), mem_v_p.reshape(shape5), gdn_s_p, gdn_conv_p, lru_h_p, lru_conv_p,
            k_rows_p, v_rows_p, gdn_s_s, gdn_conv_s, lru_h_s, lru_conv_s,
            k_rows_s, v_rows_s, smlp_v_s)
```

```python
import functools
import math

import jax
import jax.numpy as jnp
from jax import lax
from jax.experimental import pallas as pl
from jax.experimental.pallas import tpu as pltpu

F32 = jnp.float32
BF16 = jnp.bfloat16
HIGHEST = lax.Precision.HIGHEST

D_MODEL = 2048
EPS = 1e-6
CHUNK = 64
CONV_W = 4
CONV_PAD = 8
GDN_HEADS = 8
GDN_D = 128
GDN_QKV_W = 3 * GDN_HEADS * GDN_D
GDN_CHUNKS_PER_STEP = 2
LRU_W = 1024
LRU_BLOCKS = 8
LRU_BLOCK_W = LRU_W // LRU_BLOCKS
LRU_C = 8.0
IN0_PAD_W = 6400
BD_COL_BLOCK = 6144 // 128
SWA_Q_HEADS = 16
SWA_KV_HEADS = 4
SWA_GROUP = SWA_Q_HEADS // SWA_KV_HEADS
SWA_HD = 64
SWA_KV_W = SWA_KV_HEADS * SWA_HD
WINDOW = 128
SWA_CHUNKS_PER_STEP = 4
ROPE_THETA = 10000.0
PAST_LEN = 4096
SMLP_GROUPS = 8
SMLP_GROUP_W = 128
SMLP_W = SMLP_GROUPS * SMLP_GROUP_W
SMLP_CHUNK = 128
IN1_W = 3584
MEM_LEN = 256
MEM_HEADS = 4
MEM_HD = 128
MEM_W = MEM_HEADS * MEM_HD
FF_DENSE = 5632
N_EXPERTS = 8
FF_EXPERT = 2816
MOE_BM = 512
MOE_TF = FF_EXPERT // 11
LANES = 128
SUBLANES = 8

VMEM_LIMIT_MB = 56


def _cparams(semantics, vmem_mb=VMEM_LIMIT_MB):
    return pltpu.CompilerParams(dimension_semantics=semantics, vmem_limit_bytes=vmem_mb * 2 ** 20)


def _rms(x, g):
    return x * lax.rsqrt(jnp.mean(x * x, axis=-1, keepdims=True) + EPS) * g


def _sigmoid(x):
    return 1.0 / (1.0 + jnp.exp(-x))


def _silu(x):
    return x * _sigmoid(x)


def _softplus(x):
    return jnp.maximum(x, 0.0) + jnp.log(1.0 + jnp.exp(-jnp.abs(x)))


def _gelu(x):
    c = math.sqrt(2.0 / math.pi)
    return 0.5 * x * (1.0 + jnp.tanh(c * (x + 0.044715 * (x * x * x))))


def _dot(a, b):
    return jnp.dot(a.astype(BF16), b.astype(BF16), preferred_element_type=F32)


def _dot_nt(a, b):
    return lax.dot_general(a.astype(BF16), b.astype(BF16), (((1,), (1,)), ((), ())),
                           preferred_element_type=F32)


def _dot_tn(a, b):
    return lax.dot_general(a.astype(BF16), b.astype(BF16), (((0,), (0,)), ((), ())),
                           preferred_element_type=F32)


def _split3(x):
    x1 = x.astype(BF16)
    r1 = x - x1.astype(F32)
    x2 = r1.astype(BF16)
    x3 = (r1 - x2.astype(F32)).astype(BF16)
    return x1, x2, x3


def _norm_matmul_body(x_ref, g_ref, w_ref, o_ref, xn_ref):
    @pl.when(pl.program_id(1) == 0)
    def _():
        xn_ref[...] = _rms(x_ref[...], g_ref[...]).astype(BF16)

    o_ref[...] = jnp.dot(xn_ref[...], w_ref[...], preferred_element_type=F32)


def norm_matmul(x, g, w, tm, tn):
    n, k = x.shape
    nout = w.shape[1]
    tm = min(tm, n)
    return pl.pallas_call(
        _norm_matmul_body,
        grid=(n // tm, nout // tn),
        in_specs=[pl.BlockSpec((tm, k), lambda i, j: (i, 0)),
                  pl.BlockSpec((1, k), lambda i, j: (0, 0)),
                  pl.BlockSpec((k, tn), lambda i, j: (0, j))],
        out_specs=pl.BlockSpec((tm, tn), lambda i, j: (i, j)),
        out_shape=jax.ShapeDtypeStruct((n, nout), F32),
        scratch_shapes=[pltpu.VMEM((tm, k), BF16)],
        compiler_params=_cparams(("arbitrary", "arbitrary")),
        name="norm_matmul",
    )(x, g.reshape(1, k), w)


def _outproj_body(a_ref, b_ref, wa_ref, wb_ref, r_ref, g_ref, o_ref):
    acc = jnp.dot(a_ref[...].astype(BF16), wa_ref[...], preferred_element_type=F32)
    acc = acc + jnp.dot(b_ref[...].astype(BF16), wb_ref[...], preferred_element_type=F32)
    o_ref[...] = r_ref[...] + _rms(acc, g_ref[...])


def outproj_norm_resid(a, b, w, resid, g, tm=512):
    n, ka = a.shape
    kb = b.shape[1]
    d = w.shape[1]
    tm = min(tm, n)
    return pl.pallas_call(
        _outproj_body,
        grid=(n // tm,),
        in_specs=[pl.BlockSpec((tm, ka), lambda i: (i, 0)),
                  pl.BlockSpec((tm, kb), lambda i: (i, 0)),
                  pl.BlockSpec((ka, d), lambda i: (0, 0)),
                  pl.BlockSpec((kb, d), lambda i: (1, 0)),
                  pl.BlockSpec((tm, d), lambda i: (i, 0)),
                  pl.BlockSpec((1, d), lambda i: (0, 0))],
        out_specs=pl.BlockSpec((tm, d), lambda i: (i, 0)),
        out_shape=jax.ShapeDtypeStruct((n, d), F32),
        compiler_params=_cparams(("arbitrary",)),
        name="outproj_norm_resid",
    )(a, b, w, w, resid, g.reshape(1, d))


def _gdn_body(nsub, qkv_ref, z_ref, bd_ref, conv0_ref, s0_ref, cw_ref, pvec_ref, ng_ref,
              o_ref, sfin_ref, cfin_ref, s_scr, xbuf):
    c = pl.program_id(1)
    last = pl.num_programs(1) - 1
    lo = CONV_PAD - (CONV_W - 1)
    rows = nsub * CHUNK

    @pl.when(c == 0)
    def _():
        s_scr[...] = s0_ref[0]
        xbuf[lo:CONV_PAD, :] = conv0_ref[0]

    xbuf[CONV_PAD:CONV_PAD + rows, :] = qkv_ref[...]
    y = xbuf[lo:lo + rows, :] * cw_ref[0:1, :]
    for j in range(1, CONV_W):
        y = y + xbuf[lo + j:lo + j + rows, :] * cw_ref[j:j + 1, :]
    tail = xbuf[rows + lo:rows + CONV_PAD, :]
    xbuf[lo:CONV_PAD, :] = tail

    @pl.when(c == last)
    def _():
        cfin_ref[0] = tail

    act = _silu(y)
    bd = bd_ref[...]
    beta = _sigmoid(bd)
    g_all = -jnp.exp(pvec_ref[0:1, :]) * _softplus(bd + pvec_ref[1:2, :])

    row = lax.broadcasted_iota(jnp.int32, (CHUNK, CHUNK), 0)
    col = lax.broadcasted_iota(jnp.int32, (CHUNK, CHUNK), 1)
    causal = row >= col
    strict = row > col
    blk_xor = row ^ col
    rr = lax.broadcasted_iota(jnp.int32, (rows, rows), 0)
    cc = lax.broadcasted_iota(jnp.int32, (rows, rows), 1)
    tri = jnp.logical_and(rr >= cc, ((rr ^ cc) >> (CHUNK.bit_length() - 1)) == 0)
    g_cum3 = jnp.dot(tri.astype(BF16), jnp.concatenate(_split3(g_all), axis=-1), preferred_element_type=F32)
    g_cum = g_cum3[:, :LANES] + g_cum3[:, LANES:2 * LANES] + g_cum3[:, 2 * LANES:]
    g_cum_t = g_cum.T

    heads = range(GDN_HEADS)
    units = [(ci, h) for ci in range(nsub) for h in heads]
    hw = GDN_HEADS * GDN_D

    def rsl(ci):
        return slice(ci * CHUNK, (ci + 1) * CHUNK)

    gc = [g_cum[rsl(ci), GDN_HEADS + h:GDN_HEADS + h + 1] for ci, h in units]
    gr = [g_cum_t[GDN_HEADS + h:GDN_HEADS + h + 1, rsl(ci)] for ci, h in units]
    un = range(len(units))
    decay = [jnp.where(causal, jnp.exp(jnp.where(causal, gc[u] - gr[u], 0.0)), 0.0) for u in un]
    bcol = [beta[rsl(ci), h:h + 1] for ci, h in units]
    q = [act[rsl(ci), GDN_D * h:GDN_D * (h + 1)] for ci, h in units]
    k = [act[rsl(ci), hw + GDN_D * h:hw + GDN_D * (h + 1)] for ci, h in units]
    v = [act[rsl(ci), 2 * hw + GDN_D * h:2 * hw + GDN_D * (h + 1)] for ci, h in units]
    q = [x * lax.rsqrt(jnp.sum(x * x, axis=-1, keepdims=True) + EPS) * (GDN_D ** -0.5) for x in q]
    k = [x * lax.rsqrt(jnp.sum(x * x, axis=-1, keepdims=True) + EPS) for x in k]
    kb = [k[u] * bcol[u] for u in un]
    eg = [jnp.exp(gc[u]) for u in un]
    qa = [_dot_nt(jnp.concatenate([q[u], kb[u]], axis=0), k[u]) for u in un]
    qk = [qa[u][:CHUNK] * decay[u] for u in un]
    a_low = [jnp.where(strict, qa[u][CHUNK:] * decay[u], 0.0) for u in un]
    m = [jnp.where((blk_xor >> 2) == 0, -a_low[u], 0.0) for u in un]
    m2 = [_dot(m[u], m[u]) for u in un]
    n = [m[u] + m2[u] + _dot(m[u], m2[u]) for u in un]
    for lg in range(2, 6):
        low = [jnp.where((blk_xor >> lg) == 1, a_low[u], 0.0) for u in un]
        tl = [low[u] + _dot(n[u], low[u]) for u in un]
        n = [n[u] - (tl[u] + _dot(tl[u], n[u])) for u in un]
    rhs = [jnp.concatenate([v[u] * bcol[u], kb[u] * eg[u]], axis=-1) for u in un]
    sol = [rhs[u] + _dot(n[u], rhs[u]) for u in un]
    wq = [jnp.concatenate([sol[u][:, GDN_D:], q[u] * eg[u]], axis=0) for u in un]
    g_last = [g_cum[(ci + 1) * CHUNK - 1:(ci + 1) * CHUNK, GDN_HEADS + h:GDN_HEADS + h + 1] for ci, h in units]
    k_dec = [k[u] * jnp.exp(g_last[u] - gc[u]) for u in un]
    s = [s_scr[h] for h in heads]
    for ci in range(nsub):
        us = [ci * GDN_HEADS + h for h in heads]
        ws = [_dot(wq[us[h]], s[h]) for h in heads]
        v_new = [sol[us[h]][:, :GDN_D] - ws[h][:CHUNK] for h in heads]
        o = [ws[h][CHUNK:] + _dot(qk[us[h]], v_new[h]) for h in heads]
        s = [s[h] * jnp.exp(g_last[us[h]]) + _dot_tn(k_dec[us[h]], v_new[h]) for h in heads]
        for h in heads:
            zh = z_ref[rsl(ci), GDN_D * h:GDN_D * (h + 1)]
            o_ref[rsl(ci), GDN_D * h:GDN_D * (h + 1)] = _rms(o[h], ng_ref[...]) * _silu(zh)
    for h in heads:
        s_scr[h] = s[h]

    @pl.when(c == last)
    def _():
        sfin_ref[0] = s_scr[...]


def gdn_mixer(p0, bsz, t, conv0, s0, conv_w, a_log, dt_bias, norm_g):
    nsub = min(GDN_CHUNKS_PER_STEP, t // CHUNK)
    rows = nsub * CHUNK
    nc = t // rows
    n = bsz * t
    pvec = jnp.zeros((2, LANES), F32)
    pvec = pvec.at[0, GDN_HEADS:2 * GDN_HEADS].set(a_log).at[1, GDN_HEADS:2 * GDN_HEADS].set(dt_bias)
    vw = GDN_HEADS * GDN_D
    return pl.pallas_call(
        functools.partial(_gdn_body, nsub),
        grid=(bsz, nc),
        in_specs=[pl.BlockSpec((rows, GDN_QKV_W), lambda b, c: (b * nc + c, 0)),
                  pl.BlockSpec((rows, vw), lambda b, c: (b * nc + c, GDN_QKV_W // vw)),
                  pl.BlockSpec((rows, LANES), lambda b, c: (b * nc + c, BD_COL_BLOCK)),
                  pl.BlockSpec((1, CONV_W - 1, GDN_QKV_W), lambda b, c: (b, 0, 0)),
                  pl.BlockSpec((1, GDN_HEADS, GDN_D, GDN_D), lambda b, c: (b, 0, 0, 0)),
                  pl.BlockSpec((CONV_W, GDN_QKV_W), lambda b, c: (0, 0)),
                  pl.BlockSpec((2, LANES), lambda b, c: (0, 0)),
                  pl.BlockSpec((1, GDN_D), lambda b, c: (0, 0))],
        out_specs=[pl.BlockSpec((rows, vw), lambda b, c: (b * nc + c, 0)),
                   pl.BlockSpec((1, GDN_HEADS, GDN_D, GDN_D), lambda b, c: (b, 0, 0, 0)),
                   pl.BlockSpec((1, CONV_W - 1, GDN_QKV_W), lambda b, c: (b, 0, 0))],
        out_shape=[jax.ShapeDtypeStruct((n, vw), F32),
                   jax.ShapeDtypeStruct((bsz, GDN_HEADS, GDN_D, GDN_D), F32),
                   jax.ShapeDtypeStruct((bsz, CONV_W - 1, GDN_QKV_W), F32)],
        scratch_shapes=[pltpu.VMEM((GDN_HEADS, GDN_D, GDN_D), F32),
                        pltpu.VMEM((CONV_PAD + rows, GDN_QKV_W), F32)],
        compiler_params=_cparams(("arbitrary", "arbitrary")),
        name="gdn_mixer",
    )(p0, p0, p0, conv0, s0, conv_w, pvec, norm_g.reshape(1, GDN_D))


def _lru_body(tl, x_ref, gate_ref, conv0_ref, h0_ref, cw_ref, cb_ref, wa_ref, ba_ref, wx_ref, bx_ref,
              lam_ref, y_ref, hfin_ref, cfin_ref, h_scr, xbuf, abuf, bbuf):
    c = pl.program_id(1)
    last = pl.num_programs(1) - 1
    lo = CONV_PAD - (CONV_W - 1)
    pad = tl // 2

    @pl.when(c == 0)
    def _():
        h_scr[...] = h0_ref[0]
        xbuf[lo:CONV_PAD, :] = conv0_ref[0]
        abuf[0:pad, :] = jnp.ones((pad, LRU_W), F32)
        bbuf[0:pad, :] = jnp.zeros((pad, LRU_W), F32)

    xbuf[CONV_PAD:CONV_PAD + tl, :] = x_ref[...]
    xr = xbuf[lo:lo + tl, :] * cw_ref[0:1, :]
    for j in range(1, CONV_W):
        xr = xr + xbuf[lo + j:lo + j + tl, :] * cw_ref[j:j + 1, :]
    tail = xbuf[tl + lo:tl + CONV_PAD, :]
    xbuf[lo:CONV_PAD, :] = tail

    @pl.when(c == last)
    def _():
        cfin_ref[0] = tail

    xr = xr + cb_ref[...]
    ga = jnp.concatenate([_dot(xr[:, LRU_BLOCK_W * n:LRU_BLOCK_W * (n + 1)], wa_ref[n])
                          for n in range(LRU_BLOCKS)], axis=-1)
    gx = jnp.concatenate([_dot(xr[:, LRU_BLOCK_W * n:LRU_BLOCK_W * (n + 1)], wx_ref[n])
                          for n in range(LRU_BLOCKS)], axis=-1)
    gate_a = _sigmoid(ga + ba_ref[...])
    gate_x = _sigmoid(gx + bx_ref[...])
    log_a = -LRU_C * gate_a * _softplus(-lam_ref[...])
    a = jnp.exp(log_a)
    b = jnp.sqrt(1.0 - jnp.exp(2.0 * log_a)) * gate_x * xr
    d = 1
    while d < tl:
        abuf[pad:pad + tl, :] = a
        bbuf[pad:pad + tl, :] = b
        a_sh = abuf[pad - d:pad - d + tl, :]
        b_sh = bbuf[pad - d:pad - d + tl, :]
        b = a * b_sh + b
        a = a * a_sh
        d *= 2
    h = a * h_scr[...] + b
    h_last = h[tl - 1:tl, :]
    h_scr[...] = h_last
    y_ref[...] = h * _gelu(gate_ref[...])

    @pl.when(c == last)
    def _():
        hfin_ref[0] = h_last


def lru_mixer(p0, bsz, t, conv0, h0, conv_w, conv_b, w_a, b_a, w_x, b_x, lam):
    tl = min(t, 256)
    nc = t // tl
    n = bsz * t
    row = lambda v: v.reshape(1, LRU_W)
    return pl.pallas_call(
        functools.partial(_lru_body, tl),
        grid=(bsz, nc),
        in_specs=[pl.BlockSpec((tl, LRU_W), lambda b, c: (b * nc + c, 4)),
                  pl.BlockSpec((tl, LRU_W), lambda b, c: (b * nc + c, 5)),
                  pl.BlockSpec((1, CONV_W - 1, LRU_W), lambda b, c: (b, 0, 0)),
                  pl.BlockSpec((1, 1, LRU_W), lambda b, c: (b, 0, 0)),
                  pl.BlockSpec((CONV_W, LRU_W), lambda b, c: (0, 0)),
                  pl.BlockSpec((1, LRU_W), lambda b, c: (0, 0)),
                  pl.BlockSpec((LRU_BLOCKS, LRU_BLOCK_W, LRU_BLOCK_W), lambda b, c: (0, 0, 0)),
                  pl.BlockSpec((1, LRU_W), lambda b, c: (0, 0)),
                  pl.BlockSpec((LRU_BLOCKS, LRU_BLOCK_W, LRU_BLOCK_W), lambda b, c: (0, 0, 0)),
                  pl.BlockSpec((1, LRU_W), lambda b, c: (0, 0)),
                  pl.BlockSpec((1, LRU_W), lambda b, c: (0, 0))],
        out_specs=[pl.BlockSpec((tl, LRU_W), lambda b, c: (b * nc + c, 0)),
                   pl.BlockSpec((1, 1, LRU_W), lambda b, c: (b, 0, 0)),
                   pl.BlockSpec((1, CONV_W - 1, LRU_W), lambda b, c: (b, 0, 0))],
        out_shape=[jax.ShapeDtypeStruct((n, LRU_W), F32),
                   jax.ShapeDtypeStruct((bsz, 1, LRU_W), F32),
                   jax.ShapeDtypeStruct((bsz, CONV_W - 1, LRU_W), F32)],
        scratch_shapes=[pltpu.VMEM((1, LRU_W), F32),
                        pltpu.VMEM((CONV_PAD + tl, LRU_W), F32),
                        pltpu.VMEM((tl // 2 + tl, LRU_W), F32),
                        pltpu.VMEM((tl // 2 + tl, LRU_W), F32)],
        compiler_params=_cparams(("arbitrary", "arbitrary")),
        name="lru_mixer",
    )(p0, p0, conv0, h0.reshape(bsz, 1, LRU_W), conv_w, row(conv_b), w_a, row(b_a), w_x, row(b_x), row(lam))


def _xattn_body(x_ref, mk_ref, mv_ref, wq_ref, wo_ref, g_in_ref, g_out_ref, o_ref):
    x = x_ref[...]
    q = jnp.dot(_rms(x, g_in_ref[...]).astype(BF16), wq_ref[...], preferred_element_type=F32)
    mk = mk_ref[0].astype(BF16)
    mv = mv_ref[0].astype(BF16)
    outs = []
    for h in range(MEM_HEADS):
        sl = slice(MEM_HD * h, MEM_HD * (h + 1))
        s = _dot_nt(q[:, sl], mk[:, sl]) * (MEM_HD ** -0.5)
        m = jnp.max(s, axis=-1, keepdims=True)
        p = jnp.exp(s - m)
        outs.append(_dot(p, mv[:, sl]) / jnp.sum(p, axis=-1, keepdims=True))
    o = jnp.concatenate(outs, axis=-1)
    y = jnp.dot(o.astype(BF16), wo_ref[...], preferred_element_type=F32)
    o_ref[...] = x + _rms(y, g_out_ref[...])


def cross_attention(x, bsz, t, mem_k, mem_v, wq, wo, g_in, g_out):
    tm = min(t, 512)
    nt = t // tm
    n, d = x.shape
    return pl.pallas_call(
        _xattn_body,
        grid=(bsz, nt),
        in_specs=[pl.BlockSpec((tm, d), lambda b, i: (b * nt + i, 0)),
                  pl.BlockSpec((1, MEM_LEN, MEM_W), lambda b, i: (b, 0, 0)),
                  pl.BlockSpec((1, MEM_LEN, MEM_W), lambda b, i: (b, 0, 0)),
                  pl.BlockSpec((d, MEM_W), lambda b, i: (0, 0)),
                  pl.BlockSpec((MEM_W, d), lambda b, i: (0, 0)),
                  pl.BlockSpec((1, d), lambda b, i: (0, 0)),
                  pl.BlockSpec((1, d), lambda b, i: (0, 0))],
        out_specs=pl.BlockSpec((tm, d), lambda b, i: (b * nt + i, 0)),
        out_shape=jax.ShapeDtypeStruct((n, d), F32),
        compiler_params=_cparams(("arbitrary", "arbitrary")),
        name="cross_attention",
    )(x, mem_k, mem_v, wq, wo, g_in.reshape(1, d), g_out.reshape(1, d))


def _ffn_body(x_ref, g_in_ref, wg_ref, wu_ref, wd_ref, g_out_ref, o_ref, xn_ref, acc_ref):
    f = pl.program_id(1)

    @pl.when(f == 0)
    def _():
        xn_ref[...] = _rms(x_ref[...], g_in_ref[...]).astype(BF16)
        acc_ref[...] = jnp.zeros_like(acc_ref)

    xn = xn_ref[...]
    gate = jnp.dot(xn, wg_ref[...], preferred_element_type=F32)
    up = jnp.dot(xn, wu_ref[...], preferred_element_type=F32)
    acc_ref[...] += jnp.dot((_silu(gate) * up).astype(BF16), wd_ref[...], preferred_element_type=F32)

    @pl.when(f == pl.num_programs(1) - 1)
    def _():
        o_ref[...] = x_ref[...] + _rms(acc_ref[...], g_out_ref[...])


def dense_ffn(x, g_in, wg, wu, wd, g_out, tm=512, tf=512):
    n, d = x.shape
    ff = wg.shape[1]
    tm = min(tm, n)
    return pl.pallas_call(
        _ffn_body,
        grid=(n // tm, ff // tf),
        in_specs=[pl.BlockSpec((tm, d), lambda i, f: (i, 0)),
                  pl.BlockSpec((1, d), lambda i, f: (0, 0)),
                  pl.BlockSpec((d, tf), lambda i, f: (0, f)),
                  pl.BlockSpec((d, tf), lambda i, f: (0, f)),
                  pl.BlockSpec((tf, d), lambda i, f: (f, 0)),
                  pl.BlockSpec((1, d), lambda i, f: (0, 0))],
        out_specs=pl.BlockSpec((tm, d), lambda i, f: (i, 0)),
        out_shape=jax.ShapeDtypeStruct((n, d), F32),
        scratch_shapes=[pltpu.VMEM((tm, d), BF16), pltpu.VMEM((tm, d), F32)],
        compiler_params=_cparams(("arbitrary", "arbitrary")),
        name="dense_ffn",
    )(x, g_in.reshape(1, d), wg, wu, wd, g_out.reshape(1, d))


def _swa_body(start, nsub, q_ref, kv_ref, cos_ref, sin_ref, kprev_ref, vprev_ref, sink_ref,
              o_ref, krot_ref, kbuf, vbuf):
    c = pl.program_id(1)
    rows_t = nsub * CHUNK

    @pl.when(c == 0)
    def _():
        kbuf[0:WINDOW, :] = kprev_ref[0]
        vbuf[0:WINDOW, :] = vprev_ref[0]

    cos = cos_ref[...]
    sin = sin_ref[...]
    lane = lax.broadcasted_iota(jnp.int32, (rows_t, LANES), 1)
    first_half = (lane % SWA_HD) < (SWA_HD // 2)

    def rope(x):
        outs = []
        for j in range(x.shape[1] // LANES):
            xb = x[:, LANES * j:LANES * (j + 1)]
            fwd = pltpu.roll(xb, LANES - SWA_HD // 2, 1)
            bwd = pltpu.roll(xb, SWA_HD // 2, 1)
            outs.append(xb * cos + jnp.where(first_half, fwd, bwd) * sin)
        return jnp.concatenate(outs, axis=-1)

    q = rope(q_ref[...])
    kv = kv_ref[...]
    k = rope(kv[:, :SWA_KV_W])
    krot_ref[...] = k
    kbuf[WINDOW:WINDOW + rows_t, :] = k
    vbuf[WINDOW:WINDOW + rows_t, :] = kv[:, SWA_KV_W:]

    nk = WINDOW + CHUNK
    rows = SWA_GROUP * CHUNK
    key_off = lax.broadcasted_iota(jnp.int32, (rows, nk), 1) - WINDOW
    row_head = lax.broadcasted_iota(jnp.int32, (rows, 1), 0) // CHUNK
    kvh = range(SWA_KV_HEADS)
    units = [(ci, hk) for ci in range(nsub) for hk in kvh]
    un = range(len(units))
    valid = [start + (c * nsub + ci) * CHUNK + key_off >= 0 for ci in range(nsub)]
    qg = [jnp.concatenate([q[ci * CHUNK:(ci + 1) * CHUNK,
                             SWA_HD * (hk * SWA_GROUP + gi):SWA_HD * (hk * SWA_GROUP + gi + 1)]
                           for gi in range(SWA_GROUP)], axis=0) for ci, hk in units]
    kh = [kbuf[ci * CHUNK:ci * CHUNK + nk, SWA_HD * hk:SWA_HD * (hk + 1)] for ci, hk in units]
    vh = [vbuf[ci * CHUNK:ci * CHUNK + nk, SWA_HD * hk:SWA_HD * (hk + 1)] for ci, hk in units]
    sink_h = []
    for hk in kvh:
        col = jnp.full((rows, 1), sink_ref[hk * SWA_GROUP], F32)
        for gi in range(1, SWA_GROUP):
            col = jnp.where(row_head == gi, sink_ref[hk * SWA_GROUP + gi], col)
        sink_h.append(col)
    sink = [sink_h[hk] for _, hk in units]
    s = [jnp.where(valid[units[u][0]], _dot_nt(qg[u], kh[u]) * (SWA_HD ** -0.5), -jnp.inf) for u in un]
    m = [jnp.maximum(jnp.max(s[u], axis=-1, keepdims=True), sink[u]) for u in un]
    p = [jnp.exp(s[u] - m[u]) for u in un]
    denom = [jnp.sum(p[u], axis=-1, keepdims=True) + jnp.exp(sink[u] - m[u]) for u in un]
    og = [_dot(p[u], vh[u]) / denom[u] for u in un]
    for u, (ci, hk) in enumerate(units):
        for pair in range(SWA_GROUP // 2):
            lo_rows = og[u][CHUNK * 2 * pair:CHUNK * (2 * pair + 1)]
            hi_rows = og[u][CHUNK * (2 * pair + 1):CHUNK * (2 * pair + 2)]
            lane0 = SWA_HD * (hk * SWA_GROUP + 2 * pair)
            o_ref[ci * CHUNK:(ci + 1) * CHUNK, lane0:lane0 + 2 * SWA_HD] = jnp.concatenate(
                [lo_rows, hi_rows], axis=-1)

    k_keep = kbuf[rows_t:rows_t + WINDOW, :]
    v_keep = vbuf[rows_t:rows_t + WINDOW, :]
    kbuf[0:WINDOW, :] = k_keep
    vbuf[0:WINDOW, :] = v_keep


def _rope_tables(start, t):
    half = SWA_HD // 2
    inv_freq = jnp.exp(-math.log(ROPE_THETA) * jnp.arange(half, dtype=F32) / half)
    ang = (start + jnp.arange(t)).astype(F32)[:, None] * inv_freq[None, :]
    cos = jnp.cos(ang)
    sin = jnp.sin(ang)
    return jnp.tile(cos, (1, LANES // half)), jnp.tile(jnp.concatenate([-sin, sin], axis=-1), (1, LANES // SWA_HD))


def swa_mixer(p1, bsz, t, start, k_prev, v_prev, sinks):
    nsub = min(SWA_CHUNKS_PER_STEP, t // CHUNK)
    rows = nsub * CHUNK
    nc = t // rows
    n = bsz * t
    qw = SWA_Q_HEADS * SWA_HD
    cos, sin = _rope_tables(start, t)
    return pl.pallas_call(
        functools.partial(_swa_body, start, nsub),
        grid=(bsz, nc),
        in_specs=[pl.BlockSpec((rows, qw), lambda b, c: (b * nc + c, 0)),
                  pl.BlockSpec((rows, 2 * SWA_KV_W), lambda b, c: (b * nc + c, 3 * qw // (2 * SWA_KV_W))),
                  pl.BlockSpec((rows, LANES), lambda b, c: (c, 0)),
                  pl.BlockSpec((rows, LANES), lambda b, c: (c, 0)),
                  pl.BlockSpec((1, WINDOW, SWA_KV_W), lambda b, c: (b, 0, 0)),
                  pl.BlockSpec((1, WINDOW, SWA_KV_W), lambda b, c: (b, 0, 0)),
                  pl.BlockSpec(memory_space=pltpu.SMEM)],
        out_specs=[pl.BlockSpec((rows, qw), lambda b, c: (b * nc + c, 0)),
                   pl.BlockSpec((rows, SWA_KV_W), lambda b, c: (b * nc + c, 0))],
        out_shape=[jax.ShapeDtypeStruct((n, qw), F32),
                   jax.ShapeDtypeStruct((n, SWA_KV_W), F32)],
        scratch_shapes=[pltpu.VMEM((WINDOW + rows, SWA_KV_W), F32),
                        pltpu.VMEM((WINDOW + rows, SWA_KV_W), F32)],
        compiler_params=_cparams(("arbitrary", "arbitrary")),
        name="swa_mixer",
    )(p1, p1, cos, sin, k_prev, v_prev, sinks)


def _smlp_body(lc, u_ref, v_ref, lg_ref, lb_ref, ws_ref, bs_ref, y_ref, vn_ref):
    v = _gelu(v_ref[...])
    mu = jnp.mean(v, axis=-1, keepdims=True)
    vc = v - mu
    vn = vc * lax.rsqrt(jnp.mean(vc * vc, axis=-1, keepdims=True) + EPS) * lg_ref[...] + lb_ref[...]
    vn_ref[...] = vn
    u = _gelu(u_ref[...])
    row = lax.broadcasted_iota(jnp.int32, (lc, lc), 0)
    col = lax.broadcasted_iota(jnp.int32, (lc, lc), 1)
    for g in range(SMLP_GROUPS):
        sl = slice(SMLP_GROUP_W * g, SMLP_GROUP_W * (g + 1))
        w = jnp.where(row >= col, ws_ref[g, 0:lc, 0:lc], 0.0)
        s = _dot(w, vn[:, sl]) + bs_ref[0:lc, g:g + 1]
        y_ref[:, sl] = u[:, sl] * s


def smlp_mixer(p1, bsz, t, ln_g, ln_b, w_spatial, b_spatial):
    lc = min(SMLP_CHUNK, t)
    n = bsz * t
    row = lambda v: v.reshape(1, SMLP_W)
    return pl.pallas_call(
        functools.partial(_smlp_body, lc),
        grid=(n // lc,),
        in_specs=[pl.BlockSpec((lc, SMLP_W), lambda i: (i, 1)),
                  pl.BlockSpec((lc, SMLP_W), lambda i: (i, 2)),
                  pl.BlockSpec((1, SMLP_W), lambda i: (0, 0)),
                  pl.BlockSpec((1, SMLP_W), lambda i: (0, 0)),
                  pl.BlockSpec((SMLP_GROUPS, SMLP_CHUNK, SMLP_CHUNK), lambda i: (0, 0, 0)),
                  pl.BlockSpec((SMLP_CHUNK, SMLP_GROUPS), lambda i: (0, 0))],
        out_specs=[pl.BlockSpec((lc, SMLP_W), lambda i: (i, 0)),
                   pl.BlockSpec((lc, SMLP_W), lambda i: (i, 0))],
        out_shape=[jax.ShapeDtypeStruct((n, SMLP_W), F32),
                   jax.ShapeDtypeStruct((n, SMLP_W), F32)],
        compiler_params=_cparams(("arbitrary",)),
        name="smlp_mixer",
    )(p1, p1, row(ln_g), row(ln_b), w_spatial, b_spatial.T)


def _router_body(x_ref, g_ref, wr_ref, idx_ref, gate_ref):
    hn = _rms(x_ref[...], g_ref[...])
    logits = lax.dot_general(wr_ref[...], hn, (((1,), (1,)), ((), ())), precision=HIGHEST,
                             preferred_element_type=F32)
    e_iota = lax.broadcasted_iota(jnp.int32, logits.shape, 0)
    m1 = jnp.max(logits, axis=0, keepdims=True)
    i1 = jnp.min(jnp.where(logits == m1, e_iota, N_EXPERTS), axis=0, keepdims=True)
    rest = jnp.where(e_iota == i1, -jnp.inf, logits)
    m2 = jnp.max(rest, axis=0, keepdims=True)
    i2 = jnp.min(jnp.where(rest == m2, e_iota, N_EXPERTS), axis=0, keepdims=True)
    e2 = jnp.exp(m2 - m1)
    den = 1.0 + e2
    idx_ref[...] = jnp.concatenate([i1, i2], axis=0)
    tm = logits.shape[1]
    gates = jnp.concatenate([1.0 / den, e2 / den, jnp.zeros((LANES - 2, tm), F32)], axis=0)
    gate_ref[...] = gates.T


def moe_router(x, g, w_router, tm=512):
    n, d = x.shape
    tm = min(tm, n)
    return pl.pallas_call(
        _router_body,
        grid=(n // tm,),
        in_specs=[pl.BlockSpec((tm, d), lambda i: (i, 0)),
                  pl.BlockSpec((1, d), lambda i: (0, 0)),
                  pl.BlockSpec((N_EXPERTS, d), lambda i: (0, 0))],
        out_specs=[pl.BlockSpec((2, tm), lambda i: (0, i)),
                   pl.BlockSpec((tm, LANES), lambda i: (i, 0))],
        out_shape=[jax.ShapeDtypeStruct((2, n), jnp.int32),
                   jax.ShapeDtypeStruct((n, LANES), F32)],
        compiler_params=_cparams(("arbitrary",)),
        name="moe_router",
    )(x, g.reshape(1, d), w_router.T)


def _moe_plan(top_idx):
    n = top_idx.shape[1]
    flat_e = top_idx.reshape(-1)
    onehot = (flat_e[:, None] == jnp.arange(N_EXPERTS, dtype=jnp.int32)[None, :]).astype(jnp.int32)
    rank = jnp.sum(jnp.cumsum(onehot, axis=0) * onehot, axis=1) - 1
    counts = jnp.sum(onehot, axis=0)
    padded = (counts + MOE_BM - 1) // MOE_BM * MOE_BM
    pad_end = jnp.cumsum(padded)
    pad_start = pad_end - padded
    dest = jnp.sum(onehot * pad_start[None, :], axis=1) + rank
    n_blk = -(-2 * n // MOE_BM) + N_EXPERTS
    blk_start = jnp.arange(n_blk, dtype=jnp.int32) * MOE_BM
    blk_e = jnp.minimum(jnp.sum((blk_start[:, None] >= pad_end[None, :]).astype(jnp.int32), axis=1),
                        N_EXPERTS - 1)
    blk_valid = jnp.clip((pad_start + counts)[blk_e] - blk_start, 0, MOE_BM).astype(jnp.int32)
    n_active = (pad_end[-1] // MOE_BM).astype(jnp.int32).reshape(1)
    return dest.reshape(2, n).astype(jnp.int32), blk_e.astype(jnp.int32), blk_valid, n_active, n_blk


def _moe_dispatch_body(tc, dest_ref, x_ref, g_ref, xs_hbm, hn_scr, sem):
    d = x_ref.shape[1]
    hn_scr[...] = _rms(x_ref[...], g_ref[...]).reshape(tc // SUBLANES, SUBLANES, d)

    def start(i8, carry):
        for j in range(SUBLANES):
            for slot in range(2):
                dst_row = dest_ref[0, slot, i8 * SUBLANES + j]
                pltpu.make_async_copy(hn_scr.at[i8, pl.ds(j, 1), :], xs_hbm.at[pl.ds(dst_row, 1), :],
                                      sem).start(priority=slot)
        return carry

    lax.fori_loop(0, tc // SUBLANES, start, 0)
    for _ in range(2):
        pltpu.make_async_copy(x_ref, xs_hbm.at[pl.ds(0, tc), :], sem).wait()


def moe_dispatch(x, g, dest_blocks, n_rows, tc):
    n, d = x.shape
    return pl.pallas_call(
        functools.partial(_moe_dispatch_body, tc),
        grid=(n // tc,),
        in_specs=[pl.BlockSpec((1, 2, tc), lambda i: (i, 0, 0), memory_space=pltpu.SMEM),
                  pl.BlockSpec((tc, d), lambda i: (i, 0)),
                  pl.BlockSpec((1, d), lambda i: (0, 0))],
        out_specs=pl.BlockSpec(memory_space=pl.ANY),
        out_shape=jax.ShapeDtypeStruct((n_rows, d), F32),
        scratch_shapes=[pltpu.VMEM((tc // SUBLANES, SUBLANES, d), F32), pltpu.SemaphoreType.DMA(())],
        compiler_params=_cparams(("arbitrary",)),
        name="moe_dispatch",
    )(dest_blocks, x, g.reshape(1, d))


def _moe_ffn_body(blk_e_ref, blk_valid_ref, nact_ref, xs_ref, wg_hbm, wu_hbm, wd_hbm, ys_ref,
                  wg_scr, wu_scr, wd_scr, sem):
    i = pl.program_id(0)
    active = i < nact_ref[0]
    e = blk_e_ref[i]
    new_expert = jnp.logical_or(i == 0, e != blk_e_ref[jnp.maximum(i - 1, 0)])

    @pl.when(jnp.logical_and(active, new_expert))
    def _():
        copies = [pltpu.make_async_copy(w_hbm.at[e], w_scr, sem.at[j])
                  for j, (w_hbm, w_scr) in enumerate(((wg_hbm, wg_scr), (wu_hbm, wu_scr), (wd_hbm, wd_scr)))]
        for cp in copies:
            cp.start()
        for cp in copies:
            cp.wait()

    @pl.when(active)
    def _():
        row = lax.broadcasted_iota(jnp.int32, (MOE_BM, 1), 0)
        xb = jnp.where(row < blk_valid_ref[i], xs_ref[...], 0.0).astype(BF16)
        acc = None
        for j in range(FF_EXPERT // MOE_TF):
            sl = slice(MOE_TF * j, MOE_TF * (j + 1))
            gate = jnp.dot(xb, wg_scr[:, sl], preferred_element_type=F32)
            up = jnp.dot(xb, wu_scr[:, sl], preferred_element_type=F32)
            part = jnp.dot((_silu(gate) * up).astype(BF16), wd_scr[sl, :], preferred_element_type=F32)
            acc = part if acc is None else acc + part
        ys_ref[...] = acc

    @pl.when(jnp.logical_not(active))
    def _():
        ys_ref[...] = jnp.zeros_like(ys_ref)


def moe_expert_ffn(xs, blk_e, blk_valid, n_active, wg, wu, wd):
    n_rows, d = xs.shape
    n_blk = n_rows // MOE_BM
    return pl.pallas_call(
        _moe_ffn_body,
        grid_spec=pltpu.PrefetchScalarGridSpec(
            num_scalar_prefetch=3,
            grid=(n_blk,),
            in_specs=[pl.BlockSpec((MOE_BM, d), lambda i, be, bv, na: (jnp.minimum(i, na[0] - 1), 0)),
                      pl.BlockSpec(memory_space=pl.ANY),
                      pl.BlockSpec(memory_space=pl.ANY),
                      pl.BlockSpec(memory_space=pl.ANY)],
            out_specs=pl.BlockSpec((MOE_BM, d), lambda i, be, bv, na: (i, 0)),
            scratch_shapes=[pltpu.VMEM((d, FF_EXPERT), BF16), pltpu.VMEM((d, FF_EXPERT), BF16),
                            pltpu.VMEM((FF_EXPERT, d), BF16), pltpu.SemaphoreType.DMA((3,))]),
        out_shape=jax.ShapeDtypeStruct((n_rows, d), F32),
        compiler_params=_cparams(("arbitrary",)),
        name="moe_expert_ffn",
    )(blk_e, blk_valid, n_active, xs, wg, wu, wd)


def _moe_combine_body(tc, pos_ref, pos_next_ref, ys_hbm, gate_ref, x_ref, g_ref, o_ref, buf, sem):
    i = pl.program_id(0)
    nb = pl.num_programs(0)
    slot = i % 2

    def issue(p_ref, s):
        def start(i8, carry):
            for j in range(SUBLANES):
                for choice in range(2):
                    src_row = p_ref[0, choice, i8 * SUBLANES + j]
                    pltpu.make_async_copy(ys_hbm.at[pl.ds(src_row, 1), :], buf.at[s, choice, i8, pl.ds(j, 1), :],
                                          sem.at[s]).start(priority=choice)
            return carry

        lax.fori_loop(0, tc // SUBLANES, start, 0)

    @pl.when(i == 0)
    def _():
        issue(pos_ref, 0)

    @pl.when(i + 1 < nb)
    def _():
        issue(pos_next_ref, 1 - slot)

    d = x_ref.shape[1]
    for choice in range(2):
        pltpu.make_async_copy(ys_hbm.at[pl.ds(0, tc), :], o_ref, sem.at[slot]).wait()
    gates = gate_ref[...]
    y = (gates[:, 0:1] * buf[slot, 0].reshape(tc, d) + gates[:, 1:2] * buf[slot, 1].reshape(tc, d))
    o_ref[...] = x_ref[...] + _rms(y, g_ref[...])


def moe_combine(ys, pos_blocks, gates, x, g, tc):
    n, d = x.shape
    nb = n // tc
    return pl.pallas_call(
        functools.partial(_moe_combine_body, tc),
        grid=(nb,),
        in_specs=[pl.BlockSpec((1, 2, tc), lambda i: (i, 0, 0), memory_space=pltpu.SMEM),
                  pl.BlockSpec((1, 2, tc), lambda i: (jnp.minimum(i + 1, nb - 1), 0, 0), memory_space=pltpu.SMEM),
                  pl.BlockSpec(memory_space=pl.ANY),
                  pl.BlockSpec((tc, LANES), lambda i: (i, 0)),
                  pl.BlockSpec((tc, d), lambda i: (i, 0)),
                  pl.BlockSpec((1, d), lambda i: (0, 0))],
        out_specs=pl.BlockSpec((tc, d), lambda i: (i, 0)),
        out_shape=jax.ShapeDtypeStruct((n, d), F32),
        scratch_shapes=[pltpu.VMEM((2, 2, tc // SUBLANES, SUBLANES, d), F32), pltpu.SemaphoreType.DMA((2,))],
        compiler_params=_cparams(("arbitrary",)),
        name="moe_combine",
    )(pos_blocks, pos_blocks, ys, gates, x, g.reshape(1, d))


def moe_block(x, g_in, w_router, wg, wu, wd, g_out, tc=256):
    n = x.shape[0]
    tc = min(tc, n)
    top_idx, gates = moe_router(x, g_in, w_router)
    dest, blk_e, blk_valid, n_active, n_blk = _moe_plan(top_idx)
    dest_blocks = dest.reshape(2, n // tc, tc).transpose(1, 0, 2)
    xs = moe_dispatch(x, g_in, dest_blocks, n_blk * MOE_BM, tc)
    ys = moe_expert_ffn(xs, blk_e, blk_valid, n_active, wg, wu, wd)
    return moe_combine(ys, dest_blocks, gates, x, g_out, tc)


def _forward(x, start, keep, mem_k, mem_v, gdn_conv0, gdn_s0, lru_conv0, lru_h0, swa_k0, swa_v0, p):
    bsz, t, d = x.shape
    x = x.reshape(bsz * t, d)
    ng = p['norm_g']
    p0 = norm_matmul(x, ng[0, 0], p['w_in0'], 1024, 1280)
    o_gdn, gdn_s, gdn_conv = gdn_mixer(p0, bsz, t, gdn_conv0, gdn_s0, p['gdn_conv_w'], p['gdn_a_log'],
                                       p['gdn_dt_bias'], p['gdn_norm_g'])
    y_lru, lru_h, lru_conv = lru_mixer(p0, bsz, t, lru_conv0, lru_h0, p['lru_conv_w'], p['lru_conv_b'],
                                       p['lru_w_a'], p['lru_b_a'], p['lru_w_x'], p['lru_b_x'], p['lru_lambda'])
    x = outproj_norm_resid(o_gdn, y_lru, p['w_out0'], x, ng[0, 1])
    x = cross_attention(x, bsz, t, mem_k[0], mem_v[0], p['w_xq'][0], p['w_xo'][0], ng[0, 2], ng[0, 3])
    x = dense_ffn(x, ng[0, 4], p['w_ff_gate'], p['w_ff_up'], p['w_ff_down'], ng[0, 5])
    p1 = norm_matmul(x, ng[1, 0], p['w_in1'], 1024, 1792)
    attn, k_rot = swa_mixer(p1, bsz, t, start, swa_k0, swa_v0, p['swa_sinks'])
    y_smlp, smlp_v = smlp_mixer(p1, bsz, t, p['smlp_ln_g'], p['smlp_ln_b'], p['w_spatial'], p['b_spatial'])
    x = outproj_norm_resid(attn, y_smlp, p['w_out1'], x, ng[1, 1])
    x = cross_attention(x, bsz, t, mem_k[1], mem_v[1], p['w_xq'][1], p['w_xo'][1], ng[1, 2], ng[1, 3])
    x = moe_block(x, ng[1, 4], p['w_router'], p['w_moe_gate'], p['w_moe_up'], p['w_moe_down'], ng[1, 5])
    k_rows = k_rot.reshape(bsz, t, SWA_KV_W)[:, t - keep:].reshape(bsz, keep, SWA_KV_HEADS, SWA_HD)
    v_rows = p1.reshape(bsz, t, IN1_W)[:, t - keep:, IN1_W - SWA_KV_W:].reshape(bsz, keep, SWA_KV_HEADS, SWA_HD)
    return (x.reshape(bsz, t, d), gdn_conv, gdn_s, lru_conv, lru_h.reshape(bsz, LRU_W),
            k_rows, v_rows, smlp_v.reshape(bsz, t, SMLP_W))


def _prepare_weights(norm_g, w_in0, gdn_conv_w, gdn_a_log, gdn_dt_bias, gdn_norm_g, lru_conv_w, lru_conv_b,
                     lru_w_a, lru_b_a, lru_w_x, lru_b_x, lru_lambda, w_out0, w_in1, swa_sinks, smlp_ln_g,
                     smlp_ln_b, w_spatial, b_spatial, w_out1, w_xq, w_xo, w_ff_gate, w_ff_up, w_ff_down,
                     w_router, w_moe_gate, w_moe_up, w_moe_down):
    qkvz_w = GDN_QKV_W + GDN_HEADS * GDN_D
    bd_w = 2 * GDN_HEADS
    w0 = jnp.concatenate([w_in0[:, :qkvz_w], w_in0[:, qkvz_w + bd_w:], w_in0[:, qkvz_w:qkvz_w + bd_w],
                          jnp.zeros((D_MODEL, IN0_PAD_W - w_in0.shape[1]), w_in0.dtype)], axis=1)
    qw = SWA_Q_HEADS * SWA_HD
    w1 = jnp.concatenate([w_in1[:, :qw], w_in1[:, qw + 2 * SWA_KV_W:], w_in1[:, qw:qw + 2 * SWA_KV_W]], axis=1)
    return dict(
        norm_g=norm_g, w_in0=w0.astype(BF16), gdn_conv_w=gdn_conv_w, gdn_a_log=gdn_a_log,
        gdn_dt_bias=gdn_dt_bias, gdn_norm_g=gdn_norm_g, lru_conv_w=lru_conv_w, lru_conv_b=lru_conv_b,
        lru_w_a=lru_w_a, lru_b_a=lru_b_a, lru_w_x=lru_w_x, lru_b_x=lru_b_x, lru_lambda=lru_lambda,
        w_out0=w_out0.astype(BF16), w_in1=w1.astype(BF16), swa_sinks=swa_sinks, smlp_ln_g=smlp_ln_g,
        smlp_ln_b=smlp_ln_b, w_spatial=w_spatial, b_spatial=b_spatial, w_out1=w_out1.astype(BF16),
        w_xq=w_xq.astype(BF16), w_xo=w_xo.astype(BF16), w_ff_gate=w_ff_gate.astype(BF16),
        w_ff_up=w_ff_up.astype(BF16), w_ff_down=w_ff_down.astype(BF16), w_router=w_router,
        w_moe_gate=w_moe_gate.astype(BF16), w_moe_up=w_moe_up.astype(BF16), w_moe_down=w_moe_down.astype(BF16))


def kernel(x_prompt, x_sample, mem_prompt, cache_mem_k, cache_mem_v, state_gdn, state_gdn_conv, state_rglru_h, state_rglru_conv, cache_swa_k, cache_swa_v, norm_g, mem_norm_g, w_in0, gdn_conv_w, gdn_a_log, gdn_dt_bias, gdn_norm_g, lru_conv_w, lru_conv_b, lru_w_a, lru_b_a, lru_w_x, lru_b_x, lru_lambda, w_out0, w_in1, swa_sinks, smlp_ln_g, smlp_ln_b, w_spatial, b_spatial, w_out1, w_xq, w_xk, w_xv, w_xo, w_ff_gate, w_ff_up, w_ff_down, w_router, w_moe_gate, w_moe_up, w_moe_down):
    p = _prepare_weights(norm_g, w_in0, gdn_conv_w, gdn_a_log, gdn_dt_bias, gdn_norm_g, lru_conv_w, lru_conv_b,
                         lru_w_a, lru_b_a, lru_w_x, lru_b_x, lru_lambda, w_out0, w_in1, swa_sinks, smlp_ln_g,
                         smlp_ln_b, w_spatial, b_spatial, w_out1, w_xq, w_xo, w_ff_gate, w_ff_up, w_ff_down,
                         w_router, w_moe_gate, w_moe_up, w_moe_down)
    bsz, t, d = x_prompt.shape
    depth = w_xk.shape[0]
    mem_flat = mem_prompt.reshape(bsz * MEM_LEN, d)
    mem_k_p = jnp.stack([norm_matmul(mem_flat, mem_norm_g[l], w_xk[l].astype(BF16), 512, MEM_W)
                         for l in range(depth)]).reshape(depth, bsz, MEM_LEN, MEM_W)
    mem_v_p = jnp.stack([norm_matmul(mem_flat, mem_norm_g[l], w_xv[l].astype(BF16), 512, MEM_W)
                         for l in range(depth)]).reshape(depth, bsz, MEM_LEN, MEM_W)
    keep = min(WINDOW, t)
    (y_p, gdn_conv_p, gdn_s_p, lru_conv_p, lru_h_p, k_rows_p, v_rows_p, _) = _forward(
        x_prompt, 0, keep, mem_k_p, mem_v_p,
        jnp.zeros((bsz, CONV_W - 1, GDN_QKV_W), F32), jnp.zeros((bsz, GDN_HEADS, GDN_D, GDN_D), F32),
        jnp.zeros((bsz, CONV_W - 1, LRU_W), F32), jnp.zeros((bsz, LRU_W), F32),
        jnp.zeros((bsz, WINDOW, SWA_KV_W), F32), jnp.zeros((bsz, WINDOW, SWA_KV_W), F32), p)
    dbs, dec_t = x_sample.shape[:2]
    n_prev = cache_swa_k.shape[1]
    assert n_prev == WINDOW
    (y_s, gdn_conv_s, gdn_s_s, lru_conv_s, lru_h_s, k_rows_s, v_rows_s, smlp_v_s) = _forward(
        x_sample, PAST_LEN, dec_t, cache_mem_k.reshape(depth, dbs, MEM_LEN, MEM_W),
        cache_mem_v.reshape(depth, dbs, MEM_LEN, MEM_W), state_gdn_conv, state_gdn, state_rglru_conv,
        state_rglru_h, cache_swa_k.reshape(dbs, n_prev, SWA_KV_W), cache_swa_v.reshape(dbs, n_prev, SWA_KV_W), p)
    shape5 = (depth, bsz, MEM_LEN, MEM_HEADS, MEM_HD)
    return (y_p, y_s, mem_k_p.reshape(shape5), mem_v_p.reshape(shape5), gdn_s_p, gdn_conv_p, lru_h_p, lru_conv_p,
            k_rows_p, v_rows_p, gdn_s_s, gdn_conv_s, lru_h_s, lru_conv_s,
            k_rows_s, v_rows_s, smlp_v_s)
```

```python
import functools
import math

import jax
import jax.numpy as jnp
from jax import lax
from jax.experimental import pallas as pl
from jax.experimental.pallas import tpu as pltpu

F32 = jnp.float32
BF16 = jnp.bfloat16
HIGHEST = lax.Precision.HIGHEST

D_MODEL = 2048
EPS = 1e-6
CHUNK = 64
CONV_W = 4
CONV_PAD = 8
GDN_HEADS = 8
GDN_D = 128
GDN_QKV_W = 3 * GDN_HEADS * GDN_D
GDN_CHUNKS_PER_STEP = 2
LRU_W = 1024
LRU_BLOCKS = 8
LRU_BLOCK_W = LRU_W // LRU_BLOCKS
LRU_C = 8.0
IN0_PAD_W = 6400
BD_COL_BLOCK = 6144 // 128
SWA_Q_HEADS = 16
SWA_KV_HEADS = 4
SWA_GROUP = SWA_Q_HEADS // SWA_KV_HEADS
SWA_HD = 64
SWA_KV_W = SWA_KV_HEADS * SWA_HD
WINDOW = 128
SWA_CHUNKS_PER_STEP = 4
ROPE_THETA = 10000.0
PAST_LEN = 4096
SMLP_GROUPS = 8
SMLP_GROUP_W = 128
SMLP_W = SMLP_GROUPS * SMLP_GROUP_W
SMLP_CHUNK = 128
IN1_W = 3584
MEM_LEN = 256
MEM_HEADS = 4
MEM_HD = 128
MEM_W = MEM_HEADS * MEM_HD
FF_DENSE = 5632
N_EXPERTS = 8
FF_EXPERT = 2816
MOE_BM = 512
MOE_TF = FF_EXPERT // 11
LANES = 128
SUBLANES = 8

VMEM_LIMIT_MB = 56


def _cparams(semantics, vmem_mb=VMEM_LIMIT_MB):
    return pltpu.CompilerParams(dimension_semantics=semantics, vmem_limit_bytes=vmem_mb * 2 ** 20)


def _rms(x, g):
    return x * lax.rsqrt(jnp.mean(x * x, axis=-1, keepdims=True) + EPS) * g


def _sigmoid(x):
    return 1.0 / (1.0 + jnp.exp(-x))


def _silu(x):
    return x * _sigmoid(x)


def _softplus(x):
    return jnp.maximum(x, 0.0) + jnp.log(1.0 + jnp.exp(-jnp.abs(x)))


def _gelu(x):
    c = math.sqrt(2.0 / math.pi)
    return 0.5 * x * (1.0 + jnp.tanh(c * (x + 0.044715 * (x * x * x))))


def _dot(a, b):
    return jnp.dot(a.astype(BF16), b.astype(BF16), preferred_element_type=F32)


def _dot_nt(a, b):
    return lax.dot_general(a.astype(BF16), b.astype(BF16), (((1,), (1,)), ((), ())),
                           preferred_element_type=F32)


def _dot_tn(a, b):
    return lax.dot_general(a.astype(BF16), b.astype(BF16), (((0,), (0,)), ((), ())),
                           preferred_element_type=F32)


def _split3(x):
    x1 = x.astype(BF16)
    r1 = x - x1.astype(F32)
    x2 = r1.astype(BF16)
    x3 = (r1 - x2.astype(F32)).astype(BF16)
    return x1, x2, x3


def _norm_matmul_body(x_ref, g_ref, w_ref, o_ref, xn_ref):
    @pl.when(pl.program_id(1) == 0)
    def _():
        xn_ref[...] = _rms(x_ref[...], g_ref[...]).astype(BF16)

    o_ref[...] = jnp.dot(xn_ref[...], w_ref[...], preferred_element_type=F32)


def norm_matmul(x, g, w, tm, tn):
    n, k = x.shape
    nout = w.shape[1]
    tm = min(tm, n)
    return pl.pallas_call(
        _norm_matmul_body,
        grid=(n // tm, nout // tn),
        in_specs=[pl.BlockSpec((tm, k), lambda i, j: (i, 0)),
                  pl.BlockSpec((1, k), lambda i, j: (0, 0)),
                  pl.BlockSpec((k, tn), lambda i, j: (0, j))],
        out_specs=pl.BlockSpec((tm, tn), lambda i, j: (i, j)),
        out_shape=jax.ShapeDtypeStruct((n, nout), F32),
        scratch_shapes=[pltpu.VMEM((tm, k), BF16)],
        compiler_params=_cparams(("arbitrary", "arbitrary")),
        name="norm_matmul",
    )(x, g.reshape(1, k), w)


def _outproj_body(a_ref, b_ref, wa_ref, wb_ref, r_ref, g_ref, o_ref):
    acc = jnp.dot(a_ref[...].astype(BF16), wa_ref[...], preferred_element_type=F32)
    acc = acc + jnp.dot(b_ref[...].astype(BF16), wb_ref[...], preferred_element_type=F32)
    o_ref[...] = r_ref[...] + _rms(acc, g_ref[...])


def outproj_norm_resid(a, b, w, resid, g, tm=512):
    n, ka = a.shape
    kb = b.shape[1]
    d = w.shape[1]
    tm = min(tm, n)
    return pl.pallas_call(
        _outproj_body,
        grid=(n // tm,),
        in_specs=[pl.BlockSpec((tm, ka), lambda i: (i, 0)),
                  pl.BlockSpec((tm, kb), lambda i: (i, 0)),
                  pl.BlockSpec((ka, d), lambda i: (0, 0)),
                  pl.BlockSpec((kb, d), lambda i: (1, 0)),
                  pl.BlockSpec((tm, d), lambda i: (i, 0)),
                  pl.BlockSpec((1, d), lambda i: (0, 0))],
        out_specs=pl.BlockSpec((tm, d), lambda i: (i, 0)),
        out_shape=jax.ShapeDtypeStruct((n, d), F32),
        compiler_params=_cparams(("arbitrary",)),
        name="outproj_norm_resid",
    )(a, b, w, w, resid, g.reshape(1, d))


def _gdn_body(nsub, qkv_ref, z_ref, bd_ref, conv0_ref, s0_ref, cw_ref, pvec_ref, ng_ref,
              o_ref, sfin_ref, cfin_ref, s_scr, xbuf):
    c = pl.program_id(1)
    last = pl.num_programs(1) - 1
    lo = CONV_PAD - (CONV_W - 1)
    rows = nsub * CHUNK

    @pl.when(c == 0)
    def _():
        s_scr[...] = s0_ref[0]
        xbuf[lo:CONV_PAD, :] = conv0_ref[0]

    xbuf[CONV_PAD:CONV_PAD + rows, :] = qkv_ref[...]
    y = xbuf[lo:lo + rows, :] * cw_ref[0:1, :]
    for j in range(1, CONV_W):
        y = y + xbuf[lo + j:lo + j + rows, :] * cw_ref[j:j + 1, :]
    tail = xbuf[rows + lo:rows + CONV_PAD, :]
    xbuf[lo:CONV_PAD, :] = tail

    @pl.when(c == last)
    def _():
        cfin_ref[0] = tail

    act = _silu(y)
    bd = bd_ref[...]
    beta = _sigmoid(bd)
    g_all = -jnp.exp(pvec_ref[0:1, :]) * _softplus(bd + pvec_ref[1:2, :])

    row = lax.broadcasted_iota(jnp.int32, (CHUNK, CHUNK), 0)
    col = lax.broadcasted_iota(jnp.int32, (CHUNK, CHUNK), 1)
    causal = row >= col
    strict = row > col
    blk_xor = row ^ col
    rr = lax.broadcasted_iota(jnp.int32, (rows, rows), 0)
    cc = lax.broadcasted_iota(jnp.int32, (rows, rows), 1)
    tri = jnp.logical_and(rr >= cc, ((rr ^ cc) >> (CHUNK.bit_length() - 1)) == 0)
    g_cum3 = jnp.dot(tri.astype(BF16), jnp.concatenate(_split3(g_all), axis=-1), preferred_element_type=F32)
    g_cum = g_cum3[:, :LANES] + g_cum3[:, LANES:2 * LANES] + g_cum3[:, 2 * LANES:]
    g_cum_t = g_cum.T

    heads = range(GDN_HEADS)
    units = [(ci, h) for ci in range(nsub) for h in heads]
    hw = GDN_HEADS * GDN_D

    def rsl(ci):
        return slice(ci * CHUNK, (ci + 1) * CHUNK)

    gc = [g_cum[rsl(ci), GDN_HEADS + h:GDN_HEADS + h + 1] for ci, h in units]
    gr = [g_cum_t[GDN_HEADS + h:GDN_HEADS + h + 1, rsl(ci)] for ci, h in units]
    un = range(len(units))
    decay = [jnp.where(causal, jnp.exp(jnp.where(causal, gc[u] - gr[u], 0.0)), 0.0) for u in un]
    bcol = [beta[rsl(ci), h:h + 1] for ci, h in units]
    q = [act[rsl(ci), GDN_D * h:GDN_D * (h + 1)] for ci, h in units]
    k = [act[rsl(ci), hw + GDN_D * h:hw + GDN_D * (h + 1)] for ci, h in units]
    v = [act[rsl(ci), 2 * hw + GDN_D * h:2 * hw + GDN_D * (h + 1)] for ci, h in units]
    q = [x * lax.rsqrt(jnp.sum(x * x, axis=-1, keepdims=True) + EPS) * (GDN_D ** -0.5) for x in q]
    k = [x * lax.rsqrt(jnp.sum(x * x, axis=-1, keepdims=True) + EPS) for x in k]
    kb = [k[u] * bcol[u] for u in un]
    eg = [jnp.exp(gc[u]) for u in un]
    qa = [_dot_nt(jnp.concatenate([q[u], kb[u]], axis=0), k[u]) for u in un]
    qk = [qa[u][:CHUNK] * decay[u] for u in un]
    a_low = [jnp.where(strict, qa[u][CHUNK:] * decay[u], 0.0) for u in un]
    m = [jnp.where((blk_xor >> 2) == 0, -a_low[u], 0.0) for u in un]
    m2 = [_dot(m[u], m[u]) for u in un]
    n = [m[u] + m2[u] + _dot(m[u], m2[u]) for u in un]
    for lg in range(2, 6):
        low = [jnp.where((blk_xor >> lg) == 1, a_low[u], 0.0) for u in un]
        tl = [low[u] + _dot(n[u], low[u]) for u in un]
        n = [n[u] - (tl[u] + _dot(tl[u], n[u])) for u in un]
    rhs = [jnp.concatenate([v[u] * bcol[u], kb[u] * eg[u]], axis=-1) for u in un]
    sol = [rhs[u] + _dot(n[u], rhs[u]) for u in un]
    wq = [jnp.concatenate([sol[u][:, GDN_D:], q[u] * eg[u]], axis=0) for u in un]
    g_last = [g_cum[(ci + 1) * CHUNK - 1:(ci + 1) * CHUNK, GDN_HEADS + h:GDN_HEADS + h + 1] for ci, h in units]
    k_dec = [k[u] * jnp.exp(g_last[u] - gc[u]) for u in un]
    s = [s_scr[h] for h in heads]
    for ci in range(nsub):
        us = [ci * GDN_HEADS + h for h in heads]
        ws = [_dot(wq[us[h]], s[h]) for h in heads]
        v_new = [sol[us[h]][:, :GDN_D] - ws[h][:CHUNK] for h in heads]
        o = [ws[h][CHUNK:] + _dot(qk[us[h]], v_new[h]) for h in heads]
        s = [s[h] * jnp.exp(g_last[us[h]]) + _dot_tn(k_dec[us[h]], v_new[h]) for h in heads]
        for h in heads:
            zh = z_ref[rsl(ci), GDN_D * h:GDN_D * (h + 1)]
            o_ref[rsl(ci), GDN_D * h:GDN_D * (h + 1)] = _rms(o[h], ng_ref[...]) * _silu(zh)
    for h in heads:
        s_scr[h] = s[h]

    @pl.when(c == last)
    def _():
        sfin_ref[0] = s_scr[...]


def gdn_mixer(p0, bsz, t, conv0, s0, conv_w, a_log, dt_bias, norm_g):
    nsub = min(GDN_CHUNKS_PER_STEP, t // CHUNK)
    rows = nsub * CHUNK
    nc = t // rows
    n = bsz * t
    pvec = jnp.zeros((2, LANES), F32)
    pvec = pvec.at[0, GDN_HEADS:2 * GDN_HEADS].set(a_log).at[1, GDN_HEADS:2 * GDN_HEADS].set(dt_bias)
    vw = GDN_HEADS * GDN_D
    return pl.pallas_call(
        functools.partial(_gdn_body, nsub),
        grid=(bsz, nc),
        in_specs=[pl.BlockSpec((rows, GDN_QKV_W), lambda b, c: (b * nc + c, 0)),
                  pl.BlockSpec((rows, vw), lambda b, c: (b * nc + c, GDN_QKV_W // vw)),
                  pl.BlockSpec((rows, LANES), lambda b, c: (b * nc + c, BD_COL_BLOCK)),
                  pl.BlockSpec((1, CONV_W - 1, GDN_QKV_W), lambda b, c: (b, 0, 0)),
                  pl.BlockSpec((1, GDN_HEADS, GDN_D, GDN_D), lambda b, c: (b, 0, 0, 0)),
                  pl.BlockSpec((CONV_W, GDN_QKV_W), lambda b, c: (0, 0)),
                  pl.BlockSpec((2, LANES), lambda b, c: (0, 0)),
                  pl.BlockSpec((1, GDN_D), lambda b, c: (0, 0))],
        out_specs=[pl.BlockSpec((rows, vw), lambda b, c: (b * nc + c, 0)),
                   pl.BlockSpec((1, GDN_HEADS, GDN_D, GDN_D), lambda b, c: (b, 0, 0, 0)),
                   pl.BlockSpec((1, CONV_W - 1, GDN_QKV_W), lambda b, c: (b, 0, 0))],
        out_shape=[jax.ShapeDtypeStruct((n, vw), F32),
                   jax.ShapeDtypeStruct((bsz, GDN_HEADS, GDN_D, GDN_D), F32),
                   jax.ShapeDtypeStruct((bsz, CONV_W - 1, GDN_QKV_W), F32)],
        scratch_shapes=[pltpu.VMEM((GDN_HEADS, GDN_D, GDN_D), F32),
                        pltpu.VMEM((CONV_PAD + rows, GDN_QKV_W), F32)],
        compiler_params=_cparams(("arbitrary", "arbitrary")),
        name="gdn_mixer",
    )(p0, p0, p0, conv0, s0, conv_w, pvec, norm_g.reshape(1, GDN_D))


def _lru_body(tl, x_ref, gate_ref, conv0_ref, h0_ref, cw_ref, cb_ref, wa_ref, ba_ref, wx_ref, bx_ref,
              lam_ref, y_ref, hfin_ref, cfin_ref, h_scr, xbuf, abuf, bbuf):
    c = pl.program_id(1)
    last = pl.num_programs(1) - 1
    lo = CONV_PAD - (CONV_W - 1)
    pad = tl // 2

    @pl.when(c == 0)
    def _():
        h_scr[...] = h0_ref[0]
        xbuf[lo:CONV_PAD, :] = conv0_ref[0]
        abuf[0:pad, :] = jnp.ones((pad, LRU_W), F32)
        bbuf[0:pad, :] = jnp.zeros((pad, LRU_W), F32)

    xbuf[CONV_PAD:CONV_PAD + tl, :] = x_ref[...]
    xr = xbuf[lo:lo + tl, :] * cw_ref[0:1, :]
    for j in range(1, CONV_W):
        xr = xr + xbuf[lo + j:lo + j + tl, :] * cw_ref[j:j + 1, :]
    tail = xbuf[tl + lo:tl + CONV_PAD, :]
    xbuf[lo:CONV_PAD, :] = tail

    @pl.when(c == last)
    def _():
        cfin_ref[0] = tail

    xr = xr + cb_ref[...]
    ga = jnp.concatenate([_dot(xr[:, LRU_BLOCK_W * n:LRU_BLOCK_W * (n + 1)], wa_ref[n])
                          for n in range(LRU_BLOCKS)], axis=-1)
    gx = jnp.concatenate([_dot(xr[:, LRU_BLOCK_W * n:LRU_BLOCK_W * (n + 1)], wx_ref[n])
                          for n in range(LRU_BLOCKS)], axis=-1)
    gate_a = _sigmoid(ga + ba_ref[...])
    gate_x = _sigmoid(gx + bx_ref[...])
    log_a = -LRU_C * gate_a * _softplus(-lam_ref[...])
    a = jnp.exp(log_a)
    b = jnp.sqrt(1.0 - jnp.exp(2.0 * log_a)) * gate_x * xr
    sub = lax.broadcasted_iota(jnp.int32, (tl, 1), 0) % SUBLANES
    d = 1
    while d < SUBLANES:
        abuf[pad:pad + tl, :] = a
        bbuf[pad:pad + tl, :] = b
        in_group = sub >= d
        a_sh = jnp.where(in_group, abuf[pad - d:pad - d + tl, :], 1.0)
        b_sh = jnp.where(in_group, bbuf[pad - d:pad - d + tl, :], 0.0)
        b = a * b_sh + b
        a = a * a_sh
        d *= 2
    carry = h_scr[...]
    pieces = []
    for r in range(0, tl, SUBLANES):
        h_grp = a[r:r + SUBLANES, :] * carry + b[r:r + SUBLANES, :]
        pieces.append(h_grp)
        carry = h_grp[SUBLANES - 1:SUBLANES, :]
    h = jnp.concatenate(pieces, axis=0)
    h_last = carry
    h_scr[...] = h_last
    y_ref[...] = h * _gelu(gate_ref[...])

    @pl.when(c == last)
    def _():
        hfin_ref[0] = h_last


def lru_mixer(p0, bsz, t, conv0, h0, conv_w, conv_b, w_a, b_a, w_x, b_x, lam):
    tl = min(t, 256)
    nc = t // tl
    n = bsz * t
    row = lambda v: v.reshape(1, LRU_W)
    return pl.pallas_call(
        functools.partial(_lru_body, tl),
        grid=(bsz, nc),
        in_specs=[pl.BlockSpec((tl, LRU_W), lambda b, c: (b * nc + c, 4)),
                  pl.BlockSpec((tl, LRU_W), lambda b, c: (b * nc + c, 5)),
                  pl.BlockSpec((1, CONV_W - 1, LRU_W), lambda b, c: (b, 0, 0)),
                  pl.BlockSpec((1, 1, LRU_W), lambda b, c: (b, 0, 0)),
                  pl.BlockSpec((CONV_W, LRU_W), lambda b, c: (0, 0)),
                  pl.BlockSpec((1, LRU_W), lambda b, c: (0, 0)),
                  pl.BlockSpec((LRU_BLOCKS, LRU_BLOCK_W, LRU_BLOCK_W), lambda b, c: (0, 0, 0)),
                  pl.BlockSpec((1, LRU_W), lambda b, c: (0, 0)),
                  pl.BlockSpec((LRU_BLOCKS, LRU_BLOCK_W, LRU_BLOCK_W), lambda b, c: (0, 0, 0)),
                  pl.BlockSpec((1, LRU_W), lambda b, c: (0, 0)),
                  pl.BlockSpec((1, LRU_W), lambda b, c: (0, 0))],
        out_specs=[pl.BlockSpec((tl, LRU_W), lambda b, c: (b * nc + c, 0)),
                   pl.BlockSpec((1, 1, LRU_W), lambda b, c: (b, 0, 0)),
                   pl.BlockSpec((1, CONV_W - 1, LRU_W), lambda b, c: (b, 0, 0))],
        out_shape=[jax.ShapeDtypeStruct((n, LRU_W), F32),
                   jax.ShapeDtypeStruct((bsz, 1, LRU_W), F32),
                   jax.ShapeDtypeStruct((bsz, CONV_W - 1, LRU_W), F32)],
        scratch_shapes=[pltpu.VMEM((1, LRU_W), F32),
                        pltpu.VMEM((CONV_PAD + tl, LRU_W), F32),
                        pltpu.VMEM((tl // 2 + tl, LRU_W), F32),
                        pltpu.VMEM((tl // 2 + tl, LRU_W), F32)],
        compiler_params=_cparams(("arbitrary", "arbitrary")),
        name="lru_mixer",
    )(p0, p0, conv0, h0.reshape(bsz, 1, LRU_W), conv_w, row(conv_b), w_a, row(b_a), w_x, row(b_x), row(lam))


def _xattn_body(x_ref, mk_ref, mv_ref, wq_ref, wo_ref, g_in_ref, g_out_ref, o_ref):
    x = x_ref[...]
    q = jnp.dot(_rms(x, g_in_ref[...]).astype(BF16), wq_ref[...], preferred_element_type=F32)
    mk = mk_ref[0].astype(BF16)
    mv = mv_ref[0].astype(BF16)
    outs = []
    for h in range(MEM_HEADS):
        sl = slice(MEM_HD * h, MEM_HD * (h + 1))
        s = _dot_nt(q[:, sl], mk[:, sl]) * (MEM_HD ** -0.5)
        m = jnp.max(s, axis=-1, keepdims=True)
        p = jnp.exp(s - m)
        outs.append(_dot(p, mv[:, sl]) / jnp.sum(p, axis=-1, keepdims=True))
    o = jnp.concatenate(outs, axis=-1)
    y = jnp.dot(o.astype(BF16), wo_ref[...], preferred_element_type=F32)
    o_ref[...] = x + _rms(y, g_out_ref[...])


def cross_attention(x, bsz, t, mem_k, mem_v, wq, wo, g_in, g_out):
    tm = min(t, 512)
    nt = t // tm
    n, d = x.shape
    return pl.pallas_call(
        _xattn_body,
        grid=(bsz, nt),
        in_specs=[pl.BlockSpec((tm, d), lambda b, i: (b * nt + i, 0)),
                  pl.BlockSpec((1, MEM_LEN, MEM_W), lambda b, i: (b, 0, 0)),
                  pl.BlockSpec((1, MEM_LEN, MEM_W), lambda b, i: (b, 0, 0)),
                  pl.BlockSpec((d, MEM_W), lambda b, i: (0, 0)),
                  pl.BlockSpec((MEM_W, d), lambda b, i: (0, 0)),
                  pl.BlockSpec((1, d), lambda b, i: (0, 0)),
                  pl.BlockSpec((1, d), lambda b, i: (0, 0))],
        out_specs=pl.BlockSpec((tm, d), lambda b, i: (b * nt + i, 0)),
        out_shape=jax.ShapeDtypeStruct((n, d), F32),
        compiler_params=_cparams(("arbitrary", "arbitrary")),
        name="cross_attention",
    )(x, mem_k, mem_v, wq, wo, g_in.reshape(1, d), g_out.reshape(1, d))


def _ffn_body(x_ref, g_in_ref, wg_ref, wu_ref, wd_ref, g_out_ref, o_ref, xn_ref, acc_ref):
    f = pl.program_id(1)

    @pl.when(f == 0)
    def _():
        xn_ref[...] = _rms(x_ref[...], g_in_ref[...]).astype(BF16)
        acc_ref[...] = jnp.zeros_like(acc_ref)

    xn = xn_ref[...]
    gate = jnp.dot(xn, wg_ref[...], preferred_element_type=F32)
    up = jnp.dot(xn, wu_ref[...], preferred_element_type=F32)
    acc_ref[...] += jnp.dot((_silu(gate) * up).astype(BF16), wd_ref[...], preferred_element_type=F32)

    @pl.when(f == pl.num_programs(1) - 1)
    def _():
        o_ref[...] = x_ref[...] + _rms(acc_ref[...], g_out_ref[...])


def dense_ffn(x, g_in, wg, wu, wd, g_out, tm=512, tf=512):
    n, d = x.shape
    ff = wg.shape[1]
    tm = min(tm, n)
    return pl.pallas_call(
        _ffn_body,
        grid=(n // tm, ff // tf),
        in_specs=[pl.BlockSpec((tm, d), lambda i, f: (i, 0)),
                  pl.BlockSpec((1, d), lambda i, f: (0, 0)),
                  pl.BlockSpec((d, tf), lambda i, f: (0, f)),
                  pl.BlockSpec((d, tf), lambda i, f: (0, f)),
                  pl.BlockSpec((tf, d), lambda i, f: (f, 0)),
                  pl.BlockSpec((1, d), lambda i, f: (0, 0))],
        out_specs=pl.BlockSpec((tm, d), lambda i, f: (i, 0)),
        out_shape=jax.ShapeDtypeStruct((n, d), F32),
        scratch_shapes=[pltpu.VMEM((tm, d), BF16), pltpu.VMEM((tm, d), F32)],
        compiler_params=_cparams(("arbitrary", "arbitrary")),
        name="dense_ffn",
    )(x, g_in.reshape(1, d), wg, wu, wd, g_out.reshape(1, d))


def _swa_body(start, nsub, q_ref, kv_ref, cos_ref, sin_ref, kprev_ref, vprev_ref, sink_ref,
              o_ref, krot_ref, kbuf, vbuf):
    c = pl.program_id(1)
    rows_t = nsub * CHUNK

    @pl.when(c == 0)
    def _():
        kbuf[0:WINDOW, :] = kprev_ref[0]
        vbuf[0:WINDOW, :] = vprev_ref[0]

    cos = cos_ref[...]
    sin = sin_ref[...]
    lane = lax.broadcasted_iota(jnp.int32, (rows_t, LANES), 1)
    first_half = (lane % SWA_HD) < (SWA_HD // 2)

    def rope(x):
        outs = []
        for j in range(x.shape[1] // LANES):
            xb = x[:, LANES * j:LANES * (j + 1)]
            fwd = pltpu.roll(xb, LANES - SWA_HD // 2, 1)
            bwd = pltpu.roll(xb, SWA_HD // 2, 1)
            outs.append(xb * cos + jnp.where(first_half, fwd, bwd) * sin)
        return jnp.concatenate(outs, axis=-1)

    q = rope(q_ref[...])
    kv = kv_ref[...]
    k = rope(kv[:, :SWA_KV_W])
    krot_ref[...] = k
    kbuf[WINDOW:WINDOW + rows_t, :] = k
    vbuf[WINDOW:WINDOW + rows_t, :] = kv[:, SWA_KV_W:]

    nk = WINDOW + CHUNK
    rows = SWA_GROUP * CHUNK
    key_off = lax.broadcasted_iota(jnp.int32, (rows, nk), 1) - WINDOW
    row_head = lax.broadcasted_iota(jnp.int32, (rows, 1), 0) // CHUNK
    kvh = range(SWA_KV_HEADS)
    units = [(ci, hk) for ci in range(nsub) for hk in kvh]
    un = range(len(units))
    valid = [start + (c * nsub + ci) * CHUNK + key_off >= 0 for ci in range(nsub)]
    qg = [jnp.concatenate([q[ci * CHUNK:(ci + 1) * CHUNK,
                             SWA_HD * (hk * SWA_GROUP + gi):SWA_HD * (hk * SWA_GROUP + gi + 1)]
                           for gi in range(SWA_GROUP)], axis=0) for ci, hk in units]
    kh = [kbuf[ci * CHUNK:ci * CHUNK + nk, SWA_HD * hk:SWA_HD * (hk + 1)] for ci, hk in units]
    vh = [vbuf[ci * CHUNK:ci * CHUNK + nk, SWA_HD * hk:SWA_HD * (hk + 1)] for ci, hk in units]
    sink_h = []
    for hk in kvh:
        col = jnp.full((rows, 1), sink_ref[hk * SWA_GROUP], F32)
        for gi in range(1, SWA_GROUP):
            col = jnp.where(row_head == gi, sink_ref[hk * SWA_GROUP + gi], col)
        sink_h.append(col)
    sink = [sink_h[hk] for _, hk in units]
    s = [jnp.where(valid[units[u][0]], _dot_nt(qg[u], kh[u]) * (SWA_HD ** -0.5), -jnp.inf) for u in un]
    m = [jnp.maximum(jnp.max(s[u], axis=-1, keepdims=True), sink[u]) for u in un]
    p = [jnp.exp(s[u] - m[u]) for u in un]
    denom = [jnp.sum(p[u], axis=-1, keepdims=True) + jnp.exp(sink[u] - m[u]) for u in un]
    og = [_dot(p[u], vh[u]) / denom[u] for u in un]
    for u, (ci, hk) in enumerate(units):
        for pair in range(SWA_GROUP // 2):
            lo_rows = og[u][CHUNK * 2 * pair:CHUNK * (2 * pair + 1)]
            hi_rows = og[u][CHUNK * (2 * pair + 1):CHUNK * (2 * pair + 2)]
            lane0 = SWA_HD * (hk * SWA_GROUP + 2 * pair)
            o_ref[ci * CHUNK:(ci + 1) * CHUNK, lane0:lane0 + 2 * SWA_HD] = jnp.concatenate(
                [lo_rows, hi_rows], axis=-1)

    k_keep = kbuf[rows_t:rows_t + WINDOW, :]
    v_keep = vbuf[rows_t:rows_t + WINDOW, :]
    kbuf[0:WINDOW, :] = k_keep
    vbuf[0:WINDOW, :] = v_keep


def _rope_tables(start, t):
    half = SWA_HD // 2
    inv_freq = jnp.exp(-math.log(ROPE_THETA) * jnp.arange(half, dtype=F32) / half)
    ang = (start + jnp.arange(t)).astype(F32)[:, None] * inv_freq[None, :]
    cos = jnp.cos(ang)
    sin = jnp.sin(ang)
    return jnp.tile(cos, (1, LANES // half)), jnp.tile(jnp.concatenate([-sin, sin], axis=-1), (1, LANES // SWA_HD))


def swa_mixer(p1, bsz, t, start, k_prev, v_prev, sinks):
    nsub = min(SWA_CHUNKS_PER_STEP, t // CHUNK)
    rows = nsub * CHUNK
    nc = t // rows
    n = bsz * t
    qw = SWA_Q_HEADS * SWA_HD
    cos, sin = _rope_tables(start, t)
    return pl.pallas_call(
        functools.partial(_swa_body, start, nsub),
        grid=(bsz, nc),
        in_specs=[pl.BlockSpec((rows, qw), lambda b, c: (b * nc + c, 0)),
                  pl.BlockSpec((rows, 2 * SWA_KV_W), lambda b, c: (b * nc + c, 3 * qw // (2 * SWA_KV_W))),
                  pl.BlockSpec((rows, LANES), lambda b, c: (c, 0)),
                  pl.BlockSpec((rows, LANES), lambda b, c: (c, 0)),
                  pl.BlockSpec((1, WINDOW, SWA_KV_W), lambda b, c: (b, 0, 0)),
                  pl.BlockSpec((1, WINDOW, SWA_KV_W), lambda b, c: (b, 0, 0)),
                  pl.BlockSpec(memory_space=pltpu.SMEM)],
        out_specs=[pl.BlockSpec((rows, qw), lambda b, c: (b * nc + c, 0)),
                   pl.BlockSpec((rows, SWA_KV_W), lambda b, c: (b * nc + c, 0))],
        out_shape=[jax.ShapeDtypeStruct((n, qw), F32),
                   jax.ShapeDtypeStruct((n, SWA_KV_W), F32)],
        scratch_shapes=[pltpu.VMEM((WINDOW + rows, SWA_KV_W), F32),
                        pltpu.VMEM((WINDOW + rows, SWA_KV_W), F32)],
        compiler_params=_cparams(("arbitrary", "arbitrary")),
        name="swa_mixer",
    )(p1, p1, cos, sin, k_prev, v_prev, sinks)


def _smlp_body(lc, u_ref, v_ref, lg_ref, lb_ref, ws_ref, bs_ref, y_ref, vn_ref):
    v = _gelu(v_ref[...])
    mu = jnp.mean(v, axis=-1, keepdims=True)
    vc = v - mu
    vn = vc * lax.rsqrt(jnp.mean(vc * vc, axis=-1, keepdims=True) + EPS) * lg_ref[...] + lb_ref[...]
    vn_ref[...] = vn
    u = _gelu(u_ref[...])
    row = lax.broadcasted_iota(jnp.int32, (lc, lc), 0)
    col = lax.broadcasted_iota(jnp.int32, (lc, lc), 1)
    for g in range(SMLP_GROUPS):
        sl = slice(SMLP_GROUP_W * g, SMLP_GROUP_W * (g + 1))
        w = jnp.where(row >= col, ws_ref[g, 0:lc, 0:lc], 0.0)
        s = _dot(w, vn[:, sl]) + bs_ref[0:lc, g:g + 1]
        y_ref[:, sl] = u[:, sl] * s


def smlp_mixer(p1, bsz, t, ln_g, ln_b, w_spatial, b_spatial):
    lc = min(SMLP_CHUNK, t)
    n = bsz * t
    row = lambda v: v.reshape(1, SMLP_W)
    return pl.pallas_call(
        functools.partial(_smlp_body, lc),
        grid=(n // lc,),
        in_specs=[pl.BlockSpec((lc, SMLP_W), lambda i: (i, 1)),
                  pl.BlockSpec((lc, SMLP_W), lambda i: (i, 2)),
                  pl.BlockSpec((1, SMLP_W), lambda i: (0, 0)),
                  pl.BlockSpec((1, SMLP_W), lambda i: (0, 0)),
                  pl.BlockSpec((SMLP_GROUPS, SMLP_CHUNK, SMLP_CHUNK), lambda i: (0, 0, 0)),
                  pl.BlockSpec((SMLP_CHUNK, SMLP_GROUPS), lambda i: (0, 0))],
        out_specs=[pl.BlockSpec((lc, SMLP_W), lambda i: (i, 0)),
                   pl.BlockSpec((lc, SMLP_W), lambda i: (i, 0))],
        out_shape=[jax.ShapeDtypeStruct((n, SMLP_W), F32),
                   jax.ShapeDtypeStruct((n, SMLP_W), F32)],
        compiler_params=_cparams(("arbitrary",)),
        name="smlp_mixer",
    )(p1, p1, row(ln_g), row(ln_b), w_spatial, b_spatial.T)


def _router_body(x_ref, g_ref, wr_ref, idx_ref, gate_ref):
    hn = _rms(x_ref[...], g_ref[...])
    logits = lax.dot_general(wr_ref[...], hn, (((1,), (1,)), ((), ())), precision=HIGHEST,
                             preferred_element_type=F32)
    e_iota = lax.broadcasted_iota(jnp.int32, logits.shape, 0)
    m1 = jnp.max(logits, axis=0, keepdims=True)
    i1 = jnp.min(jnp.where(logits == m1, e_iota, N_EXPERTS), axis=0, keepdims=True)
    rest = jnp.where(e_iota == i1, -jnp.inf, logits)
    m2 = jnp.max(rest, axis=0, keepdims=True)
    i2 = jnp.min(jnp.where(rest == m2, e_iota, N_EXPERTS), axis=0, keepdims=True)
    e2 = jnp.exp(m2 - m1)
    den = 1.0 + e2
    idx_ref[...] = jnp.concatenate([i1, i2], axis=0)
    tm = logits.shape[1]
    gates = jnp.concatenate([1.0 / den, e2 / den, jnp.zeros((LANES - 2, tm), F32)], axis=0)
    gate_ref[...] = gates.T


def moe_router(x, g, w_router, tm=512):
    n, d = x.shape
    tm = min(tm, n)
    return pl.pallas_call(
        _router_body,
        grid=(n // tm,),
        in_specs=[pl.BlockSpec((tm, d), lambda i: (i, 0)),
                  pl.BlockSpec((1, d), lambda i: (0, 0)),
                  pl.BlockSpec((N_EXPERTS, d), lambda i: (0, 0))],
        out_specs=[pl.BlockSpec((2, tm), lambda i: (0, i)),
                   pl.BlockSpec((tm, LANES), lambda i: (i, 0))],
        out_shape=[jax.ShapeDtypeStruct((2, n), jnp.int32),
                   jax.ShapeDtypeStruct((n, LANES), F32)],
        compiler_params=_cparams(("arbitrary",)),
        name="moe_router",
    )(x, g.reshape(1, d), w_router.T)


def _moe_block_rows(n_tokens):
    return MOE_BM if 2 * n_tokens // N_EXPERTS >= 4 * MOE_BM else MOE_BM // 2


def _moe_plan(top_idx, bm):
    n = top_idx.shape[1]
    flat_e = top_idx.reshape(-1)
    onehot = (flat_e[:, None] == jnp.arange(N_EXPERTS, dtype=jnp.int32)[None, :]).astype(jnp.int32)
    rank = jnp.sum(jnp.cumsum(onehot, axis=0) * onehot, axis=1) - 1
    counts = jnp.sum(onehot, axis=0)
    padded = (counts + bm - 1) // bm * bm
    pad_end = jnp.cumsum(padded)
    pad_start = pad_end - padded
    dest = jnp.sum(onehot * pad_start[None, :], axis=1) + rank
    n_blk = -(-2 * n // bm) + N_EXPERTS
    blk_start = jnp.arange(n_blk, dtype=jnp.int32) * bm
    blk_e = jnp.minimum(jnp.sum((blk_start[:, None] >= pad_end[None, :]).astype(jnp.int32), axis=1),
                        N_EXPERTS - 1)
    blk_valid = jnp.clip((pad_start + counts)[blk_e] - blk_start, 0, bm).astype(jnp.int32)
    n_active = (pad_end[-1] // bm).astype(jnp.int32).reshape(1)
    return dest.reshape(2, n).astype(jnp.int32), blk_e.astype(jnp.int32), blk_valid, n_active, n_blk


def _moe_dispatch_body(tc, dest_ref, x_ref, g_ref, xs_hbm, hn_scr, sem):
    d = x_ref.shape[1]
    hn_scr[...] = _rms(x_ref[...], g_ref[...]).reshape(tc // SUBLANES, SUBLANES, d)

    def start(i8, carry):
        for j in range(SUBLANES):
            for slot in range(2):
                dst_row = dest_ref[0, slot, i8 * SUBLANES + j]
                pltpu.make_async_copy(hn_scr.at[i8, pl.ds(j, 1), :], xs_hbm.at[pl.ds(dst_row, 1), :],
                                      sem).start(priority=slot)
        return carry

    lax.fori_loop(0, tc // SUBLANES, start, 0)
    for _ in range(2):
        pltpu.make_async_copy(x_ref, xs_hbm.at[pl.ds(0, tc), :], sem).wait()


def _moe_dispatch_into_body(tc, dest_ref, x_ref, g_ref, xs_prev_hbm, xs_hbm, hn_scr, sem):
    del xs_prev_hbm
    _moe_dispatch_body(tc, dest_ref, x_ref, g_ref, xs_hbm, hn_scr, sem)


def moe_dispatch(x, g, dest_blocks, n_rows, tc, xs_prev=None):
    n, d = x.shape
    in_specs = [pl.BlockSpec((1, 2, tc), lambda i: (i, 0, 0), memory_space=pltpu.SMEM),
                pl.BlockSpec((tc, d), lambda i: (i, 0)),
                pl.BlockSpec((1, d), lambda i: (0, 0))]
    args = [dest_blocks, x, g.reshape(1, d)]
    body, aliases = _moe_dispatch_body, {}
    if xs_prev is not None:
        in_specs.append(pl.BlockSpec(memory_space=pl.ANY))
        args.append(xs_prev)
        body, aliases = _moe_dispatch_into_body, {3: 0}
    return pl.pallas_call(
        functools.partial(body, tc),
        grid=(n // tc,),
        in_specs=in_specs,
        out_specs=pl.BlockSpec(memory_space=pl.ANY),
        out_shape=jax.ShapeDtypeStruct((n_rows, d), F32),
        scratch_shapes=[pltpu.VMEM((tc // SUBLANES, SUBLANES, d), F32), pltpu.SemaphoreType.DMA(())],
        input_output_aliases=aliases,
        compiler_params=_cparams(("arbitrary",)),
        name="moe_dispatch",
    )(*args)


def _moe_ffn_body(blk_e_ref, blk_valid_ref, nact_ref, xs_ref, wg_hbm, wu_hbm, wd_hbm, ys_ref,
                  wg_scr, wu_scr, wd_scr, sem):
    i = pl.program_id(0)
    active = i < nact_ref[0]
    e = blk_e_ref[i]
    new_expert = jnp.logical_or(i == 0, e != blk_e_ref[jnp.maximum(i - 1, 0)])

    @pl.when(jnp.logical_and(active, new_expert))
    def _():
        copies = [pltpu.make_async_copy(w_hbm.at[e], w_scr, sem.at[j])
                  for j, (w_hbm, w_scr) in enumerate(((wg_hbm, wg_scr), (wu_hbm, wu_scr), (wd_hbm, wd_scr)))]
        for cp in copies:
            cp.start()
        for cp in copies:
            cp.wait()

    @pl.when(active)
    def _():
        row = lax.broadcasted_iota(jnp.int32, (xs_ref.shape[0], 1), 0)
        xb = jnp.where(row < blk_valid_ref[i], xs_ref[...], 0.0).astype(BF16)
        acc = None
        for j in range(FF_EXPERT // MOE_TF):
            sl = slice(MOE_TF * j, MOE_TF * (j + 1))
            gate = jnp.dot(xb, wg_scr[:, sl], preferred_element_type=F32)
            up = jnp.dot(xb, wu_scr[:, sl], preferred_element_type=F32)
            part = jnp.dot((_silu(gate) * up).astype(BF16), wd_scr[sl, :], preferred_element_type=F32)
            acc = part if acc is None else acc + part
        ys_ref[...] = acc

    @pl.when(jnp.logical_not(active))
    def _():
        ys_ref[...] = jnp.zeros_like(ys_ref)


def moe_expert_ffn(xs, blk_e, blk_valid, n_active, wg, wu, wd, bm):
    n_rows, d = xs.shape
    n_blk = n_rows // bm
    return pl.pallas_call(
        _moe_ffn_body,
        grid_spec=pltpu.PrefetchScalarGridSpec(
            num_scalar_prefetch=3,
            grid=(n_blk,),
            in_specs=[pl.BlockSpec((bm, d), lambda i, be, bv, na: (jnp.minimum(i, na[0] - 1), 0)),
                      pl.BlockSpec(memory_space=pl.ANY),
                      pl.BlockSpec(memory_space=pl.ANY),
                      pl.BlockSpec(memory_space=pl.ANY)],
            out_specs=pl.BlockSpec((bm, d), lambda i, be, bv, na: (i, 0)),
            scratch_shapes=[pltpu.VMEM((d, FF_EXPERT), BF16), pltpu.VMEM((d, FF_EXPERT), BF16),
                            pltpu.VMEM((FF_EXPERT, d), BF16), pltpu.SemaphoreType.DMA((3,))]),
        out_shape=jax.ShapeDtypeStruct((n_rows, d), F32),
        compiler_params=_cparams(("arbitrary",)),
        name="moe_expert_ffn",
    )(blk_e, blk_valid, n_active, xs, wg, wu, wd)


def _moe_combine_body(tc, pos_ref, pos_next_ref, ys_hbm, gate_ref, x_ref, g_ref, o_ref, buf, sem):
    i = pl.program_id(0)
    nb = pl.num_programs(0)
    slot = i % 2

    def issue(p_ref, s):
        def start(i8, carry):
            for j in range(SUBLANES):
                for choice in range(2):
                    src_row = p_ref[0, choice, i8 * SUBLANES + j]
                    pltpu.make_async_copy(ys_hbm.at[pl.ds(src_row, 1), :], buf.at[s, choice, i8, pl.ds(j, 1), :],
                                          sem.at[s]).start(priority=choice)
            return carry

        lax.fori_loop(0, tc // SUBLANES, start, 0)

    @pl.when(i == 0)
    def _():
        issue(pos_ref, 0)

    @pl.when(i + 1 < nb)
    def _():
        issue(pos_next_ref, 1 - slot)

    d = x_ref.shape[1]
    for choice in range(2):
        pltpu.make_async_copy(ys_hbm.at[pl.ds(0, tc), :], o_ref, sem.at[slot]).wait()
    gates = gate_ref[...]
    y = (gates[:, 0:1] * buf[slot, 0].reshape(tc, d) + gates[:, 1:2] * buf[slot, 1].reshape(tc, d))
    o_ref[...] = x_ref[...] + _rms(y, g_ref[...])


def moe_combine(ys, pos_blocks, gates, x, g, tc):
    n, d = x.shape
    nb = n // tc
    return pl.pallas_call(
        functools.partial(_moe_combine_body, tc),
        grid=(nb,),
        in_specs=[pl.BlockSpec((1, 2, tc), lambda i: (i, 0, 0), memory_space=pltpu.SMEM),
                  pl.BlockSpec((1, 2, tc), lambda i: (jnp.minimum(i + 1, nb - 1), 0, 0), memory_space=pltpu.SMEM),
                  pl.BlockSpec(memory_space=pl.ANY),
                  pl.BlockSpec((tc, LANES), lambda i: (i, 0)),
                  pl.BlockSpec((tc, d), lambda i: (i, 0)),
                  pl.BlockSpec((1, d), lambda i: (0, 0))],
        out_specs=pl.BlockSpec((tc, d), lambda i: (i, 0)),
        out_shape=jax.ShapeDtypeStruct((n, d), F32),
        scratch_shapes=[pltpu.VMEM((2, 2, tc // SUBLANES, SUBLANES, d), F32), pltpu.SemaphoreType.DMA((2,))],
        compiler_params=_cparams(("arbitrary",)),
        name="moe_combine",
    )(pos_blocks, pos_blocks, ys, gates, x, g.reshape(1, d))


def moe_block(xs_in, g_in, w_router, wg, wu, wd, g_out, tc=256):
    routed = [moe_router(x, g_in, w_router) for x in xs_in]
    sizes = [x.shape[0] for x in xs_in]
    n_all = sum(sizes)
    bm = _moe_block_rows(n_all)
    dest, blk_e, blk_valid, n_active, n_blk = _moe_plan(jnp.concatenate([r[0] for r in routed], axis=1), bm)
    xs, dest_blocks, off = None, [], 0
    for x, n in zip(xs_in, sizes):
        t = min(tc, n)
        dest_blocks.append(dest[:, off:off + n].reshape(2, n // t, t).transpose(1, 0, 2))
        xs = moe_dispatch(x, g_in, dest_blocks[-1], n_blk * bm, t, xs_prev=xs)
        off += n
    ys = moe_expert_ffn(xs, blk_e, blk_valid, n_active, wg, wu, wd, bm)
    return [moe_combine(ys, db, r[1], x, g_out, min(tc, x.shape[0]))
            for x, db, r in zip(xs_in, dest_blocks, routed)]


def _forward(x, start, keep, mem_k, mem_v, gdn_conv0, gdn_s0, lru_conv0, lru_h0, swa_k0, swa_v0, p):
    bsz, t, d = x.shape
    x = x.reshape(bsz * t, d)
    ng = p['norm_g']
    p0 = norm_matmul(x, ng[0, 0], p['w_in0'], 1024, 1280)
    o_gdn, gdn_s, gdn_conv = gdn_mixer(p0, bsz, t, gdn_conv0, gdn_s0, p['gdn_conv_w'], p['gdn_a_log'],
                                       p['gdn_dt_bias'], p['gdn_norm_g'])
    y_lru, lru_h, lru_conv = lru_mixer(p0, bsz, t, lru_conv0, lru_h0, p['lru_conv_w'], p['lru_conv_b'],
                                       p['lru_w_a'], p['lru_b_a'], p['lru_w_x'], p['lru_b_x'], p['lru_lambda'])
    x = outproj_norm_resid(o_gdn, y_lru, p['w_out0'], x, ng[0, 1])
    x = cross_attention(x, bsz, t, mem_k[0], mem_v[0], p['w_xq'][0], p['w_xo'][0], ng[0, 2], ng[0, 3])
    x = dense_ffn(x, ng[0, 4], p['w_ff_gate'], p['w_ff_up'], p['w_ff_down'], ng[0, 5])
    p1 = norm_matmul(x, ng[1, 0], p['w_in1'], 1024, 1792)
    attn, k_rot = swa_mixer(p1, bsz, t, start, swa_k0, swa_v0, p['swa_sinks'])
    y_smlp, smlp_v = smlp_mixer(p1, bsz, t, p['smlp_ln_g'], p['smlp_ln_b'], p['w_spatial'], p['b_spatial'])
    x = outproj_norm_resid(attn, y_smlp, p['w_out1'], x, ng[1, 1])
    x = cross_attention(x, bsz, t, mem_k[1], mem_v[1], p['w_xq'][1], p['w_xo'][1], ng[1, 2], ng[1, 3])
    k_rows = k_rot.reshape(bsz, t, SWA_KV_W)[:, t - keep:].reshape(bsz, keep, SWA_KV_HEADS, SWA_HD)
    v_rows = p1.reshape(bsz, t, IN1_W)[:, t - keep:, IN1_W - SWA_KV_W:].reshape(bsz, keep, SWA_KV_HEADS, SWA_HD)
    return (x, gdn_conv, gdn_s, lru_conv, lru_h.reshape(bsz, LRU_W), k_rows, v_rows, smlp_v.reshape(bsz, t, SMLP_W))


def _prepare_weights(norm_g, w_in0, gdn_conv_w, gdn_a_log, gdn_dt_bias, gdn_norm_g, lru_conv_w, lru_conv_b,
                     lru_w_a, lru_b_a, lru_w_x, lru_b_x, lru_lambda, w_out0, w_in1, swa_sinks, smlp_ln_g,
                     smlp_ln_b, w_spatial, b_spatial, w_out1, w_xq, w_xo, w_ff_gate, w_ff_up, w_ff_down,
                     w_router, w_moe_gate, w_moe_up, w_moe_down):
    qkvz_w = GDN_QKV_W + GDN_HEADS * GDN_D
    bd_w = 2 * GDN_HEADS
    w0 = jnp.concatenate([w_in0[:, :qkvz_w], w_in0[:, qkvz_w + bd_w:], w_in0[:, qkvz_w:qkvz_w + bd_w],
                          jnp.zeros((D_MODEL, IN0_PAD_W - w_in0.shape[1]), w_in0.dtype)], axis=1)
    qw = SWA_Q_HEADS * SWA_HD
    w1 = jnp.concatenate([w_in1[:, :qw], w_in1[:, qw + 2 * SWA_KV_W:], w_in1[:, qw:qw + 2 * SWA_KV_W]], axis=1)
    return dict(
        norm_g=norm_g, w_in0=w0.astype(BF16), gdn_conv_w=gdn_conv_w, gdn_a_log=gdn_a_log,
        gdn_dt_bias=gdn_dt_bias, gdn_norm_g=gdn_norm_g, lru_conv_w=lru_conv_w, lru_conv_b=lru_conv_b,
        lru_w_a=lru_w_a, lru_b_a=lru_b_a, lru_w_x=lru_w_x, lru_b_x=lru_b_x, lru_lambda=lru_lambda,
        w_out0=w_out0.astype(BF16), w_in1=w1.astype(BF16), swa_sinks=swa_sinks, smlp_ln_g=smlp_ln_g,
        smlp_ln_b=smlp_ln_b, w_spatial=w_spatial, b_spatial=b_spatial, w_out1=w_out1.astype(BF16),
        w_xq=w_xq.astype(BF16), w_xo=w_xo.astype(BF16), w_ff_gate=w_ff_gate.astype(BF16),
        w_ff_up=w_ff_up.astype(BF16), w_ff_down=w_ff_down.astype(BF16), w_router=w_router,
        w_moe_gate=w_moe_gate.astype(BF16), w_moe_up=w_moe_up.astype(BF16), w_moe_down=w_moe_down.astype(BF16))


def kernel(x_prompt, x_sample, mem_prompt, cache_mem_k, cache_mem_v, state_gdn, state_gdn_conv, state_rglru_h, state_rglru_conv, cache_swa_k, cache_swa_v, norm_g, mem_norm_g, w_in0, gdn_conv_w, gdn_a_log, gdn_dt_bias, gdn_norm_g, lru_conv_w, lru_conv_b, lru_w_a, lru_b_a, lru_w_x, lru_b_x, lru_lambda, w_out0, w_in1, swa_sinks, smlp_ln_g, smlp_ln_b, w_spatial, b_spatial, w_out1, w_xq, w_xk, w_xv, w_xo, w_ff_gate, w_ff_up, w_ff_down, w_router, w_moe_gate, w_moe_up, w_moe_down):
    p = _prepare_weights(norm_g, w_in0, gdn_conv_w, gdn_a_log, gdn_dt_bias, gdn_norm_g, lru_conv_w, lru_conv_b,
                         lru_w_a, lru_b_a, lru_w_x, lru_b_x, lru_lambda, w_out0, w_in1, swa_sinks, smlp_ln_g,
                         smlp_ln_b, w_spatial, b_spatial, w_out1, w_xq, w_xo, w_ff_gate, w_ff_up, w_ff_down,
                         w_router, w_moe_gate, w_moe_up, w_moe_down)
    bsz, t, d = x_prompt.shape
    depth = w_xk.shape[0]
    mem_flat = mem_prompt.reshape(bsz * MEM_LEN, d)
    mem_k_p = jnp.stack([norm_matmul(mem_flat, mem_norm_g[l], w_xk[l].astype(BF16), 512, MEM_W)
                         for l in range(depth)]).reshape(depth, bsz, MEM_LEN, MEM_W)
    mem_v_p = jnp.stack([norm_matmul(mem_flat, mem_norm_g[l], w_xv[l].astype(BF16), 512, MEM_W)
                         for l in range(depth)]).reshape(depth, bsz, MEM_LEN, MEM_W)
    keep = min(WINDOW, t)
    (x_p, gdn_conv_p, gdn_s_p, lru_conv_p, lru_h_p, k_rows_p, v_rows_p, _) = _forward(
        x_prompt, 0, keep, mem_k_p, mem_v_p,
        jnp.zeros((bsz, CONV_W - 1, GDN_QKV_W), F32), jnp.zeros((bsz, GDN_HEADS, GDN_D, GDN_D), F32),
        jnp.zeros((bsz, CONV_W - 1, LRU_W), F32), jnp.zeros((bsz, LRU_W), F32),
        jnp.zeros((bsz, WINDOW, SWA_KV_W), F32), jnp.zeros((bsz, WINDOW, SWA_KV_W), F32), p)
    dbs, dec_t = x_sample.shape[:2]
    n_prev = cache_swa_k.shape[1]
    assert n_prev == WINDOW
    (x_s, gdn_conv_s, gdn_s_s, lru_conv_s, lru_h_s, k_rows_s, v_rows_s, smlp_v_s) = _forward(
        x_sample, PAST_LEN, dec_t, cache_mem_k.reshape(depth, dbs, MEM_LEN, MEM_W),
        cache_mem_v.reshape(depth, dbs, MEM_LEN, MEM_W), state_gdn_conv, state_gdn, state_rglru_conv,
        state_rglru_h, cache_swa_k.reshape(dbs, n_prev, SWA_KV_W), cache_swa_v.reshape(dbs, n_prev, SWA_KV_W), p)
    y_p, y_s = moe_block([x_p, x_s], norm_g[1, 4], p['w_router'], p['w_moe_gate'], p['w_moe_up'],
                         p['w_moe_down'], norm_g[1, 5])
    y_p = y_p.reshape(bsz, t, d)
    y_s = y_s.reshape(dbs, dec_t, d)
    shape5 = (depth, bsz, MEM_LEN, MEM_HEADS, MEM_HD)
    return (y_p, y_s, mem_k_p.reshape(shape5), mem_v_p.reshape(shape5), gdn_s_p, gdn_conv_p, lru_h_p, lru_conv_p,
            k_rows_p, v_rows_p, gdn_s_s, gdn_conv_s, lru_h_s, lru_conv_s,
            k_rows_s, v_rows_s, smlp_v_s)
```

```python
import functools
import math

import jax
import jax.numpy as jnp
from jax import lax
from jax.experimental import pallas as pl
from jax.experimental.pallas import tpu as pltpu

F32 = jnp.float32
BF16 = jnp.bfloat16
HIGHEST = lax.Precision.HIGHEST

D_MODEL = 2048
EPS = 1e-6
CHUNK = 64
CONV_W = 4
CONV_PAD = 8
GDN_HEADS = 8
GDN_D = 128
GDN_QKV_W = 3 * GDN_HEADS * GDN_D
GDN_CHUNKS_PER_STEP = 2
LRU_W = 1024
LRU_BLOCKS = 8
LRU_BLOCK_W = LRU_W // LRU_BLOCKS
LRU_C = 8.0
IN0_PAD_W = 6400
BD_COL_BLOCK = 6144 // 128
SWA_Q_HEADS = 16
SWA_KV_HEADS = 4
SWA_GROUP = SWA_Q_HEADS // SWA_KV_HEADS
SWA_HD = 64
SWA_KV_W = SWA_KV_HEADS * SWA_HD
WINDOW = 128
SWA_CHUNKS_PER_STEP = 4
ROPE_THETA = 10000.0
PAST_LEN = 4096
SMLP_GROUPS = 8
SMLP_GROUP_W = 128
SMLP_W = SMLP_GROUPS * SMLP_GROUP_W
SMLP_CHUNK = 128
IN1_W = 3584
MEM_LEN = 256
MEM_HEADS = 4
MEM_HD = 128
MEM_W = MEM_HEADS * MEM_HD
XATTN_SUB_ROWS = 512
FF_DENSE = 5632
N_EXPERTS = 8
FF_EXPERT = 2816
MOE_BM = 512
MOE_TF = FF_EXPERT // 11
LANES = 128
SUBLANES = 8

VMEM_LIMIT_MB = 56

IN_PROJ_TM = 1024
IN0_TN = 1280
IN1_TN = 1792
MEM_PROJ_TM = 512
OUT_PROJ_TM = 512
XATTN_TM = 1024
FFN_TM = 512
FFN_TF = 512
LRU_ROWS = 256
ROUTER_TM = 512
MOE_TC = 256
IN0_LRU_IN_BLOCK = 4
IN0_LRU_GATE_BLOCK = 5
IN1_U_BLOCK = 1
IN1_VG_BLOCK = 2


def _cparams(semantics, vmem_mb=VMEM_LIMIT_MB):
    return pltpu.CompilerParams(dimension_semantics=semantics, vmem_limit_bytes=vmem_mb * 2 ** 20)


def _rms(x, g):
    return x * lax.rsqrt(jnp.mean(x * x, axis=-1, keepdims=True) + EPS) * g


def _sigmoid(x):
    return 1.0 / (1.0 + jnp.exp(-x))


def _silu(x):
    return x * _sigmoid(x)


def _softplus(x):
    return jnp.maximum(x, 0.0) + jnp.log(1.0 + jnp.exp(-jnp.abs(x)))


def _gelu(x):
    c = math.sqrt(2.0 / math.pi)
    return 0.5 * x * (1.0 + jnp.tanh(c * (x + 0.044715 * (x * x * x))))


def _dot(a, b):
    return jnp.dot(a.astype(BF16), b.astype(BF16), preferred_element_type=F32)


def _dot_nt(a, b):
    return lax.dot_general(a.astype(BF16), b.astype(BF16), (((1,), (1,)), ((), ())),
                           preferred_element_type=F32)


def _dot_tn(a, b):
    return lax.dot_general(a.astype(BF16), b.astype(BF16), (((0,), (0,)), ((), ())),
                           preferred_element_type=F32)


def _split3(x):
    x1 = x.astype(BF16)
    r1 = x - x1.astype(F32)
    x2 = r1.astype(BF16)
    x3 = (r1 - x2.astype(F32)).astype(BF16)
    return x1, x2, x3


def _norm_matmul_body(x_ref, g_ref, w_ref, o_ref, xn_ref):
    @pl.when(pl.program_id(1) == 0)
    def _():
        xn_ref[...] = _rms(x_ref[...], g_ref[...]).astype(BF16)

    o_ref[...] = jnp.dot(xn_ref[...], w_ref[...], preferred_element_type=F32)


def norm_matmul(x, g, w, tm, tn):
    n, k = x.shape
    nout = w.shape[1]
    tm = min(tm, n)
    return pl.pallas_call(
        _norm_matmul_body,
        grid=(n // tm, nout // tn),
        in_specs=[pl.BlockSpec((tm, k), lambda i, j: (i, 0)),
                  pl.BlockSpec((1, k), lambda i, j: (0, 0)),
                  pl.BlockSpec((k, tn), lambda i, j: (0, j))],
        out_specs=pl.BlockSpec((tm, tn), lambda i, j: (i, j)),
        out_shape=jax.ShapeDtypeStruct((n, nout), F32),
        scratch_shapes=[pltpu.VMEM((tm, k), BF16)],
        compiler_params=_cparams(("arbitrary", "arbitrary")),
        name="norm_matmul",
    )(x, g.reshape(1, k), w)


def _outproj_body(a_ref, b_ref, wa_ref, wb_ref, r_ref, g_ref, o_ref):
    acc = jnp.dot(a_ref[...].astype(BF16), wa_ref[...], preferred_element_type=F32)
    acc = acc + jnp.dot(b_ref[...].astype(BF16), wb_ref[...], preferred_element_type=F32)
    o_ref[...] = r_ref[...] + _rms(acc, g_ref[...])


def outproj_norm_resid(a, b, w, resid, g, tm=OUT_PROJ_TM):
    n, ka = a.shape
    kb = b.shape[1]
    d = w.shape[1]
    tm = min(tm, n)
    return pl.pallas_call(
        _outproj_body,
        grid=(n // tm,),
        in_specs=[pl.BlockSpec((tm, ka), lambda i: (i, 0)),
                  pl.BlockSpec((tm, kb), lambda i: (i, 0)),
                  pl.BlockSpec((ka, d), lambda i: (0, 0)),
                  pl.BlockSpec((kb, d), lambda i: (1, 0)),
                  pl.BlockSpec((tm, d), lambda i: (i, 0)),
                  pl.BlockSpec((1, d), lambda i: (0, 0))],
        out_specs=pl.BlockSpec((tm, d), lambda i: (i, 0)),
        out_shape=jax.ShapeDtypeStruct((n, d), F32),
        compiler_params=_cparams(("arbitrary",)),
        name="outproj_norm_resid",
    )(a, b, w, w, resid, g.reshape(1, d))


def _gdn_body(nsub, qkv_ref, z_ref, bd_ref, conv0_ref, s0_ref, cw_ref, pvec_ref, ng_ref,
              o_ref, sfin_ref, cfin_ref, s_scr, xbuf):
    c = pl.program_id(1)
    last = pl.num_programs(1) - 1
    lo = CONV_PAD - (CONV_W - 1)
    rows = nsub * CHUNK

    @pl.when(c == 0)
    def _():
        s_scr[...] = s0_ref[0]
        xbuf[lo:CONV_PAD, :] = conv0_ref[0]

    xbuf[CONV_PAD:CONV_PAD + rows, :] = qkv_ref[...]
    y = xbuf[lo:lo + rows, :] * cw_ref[0:1, :]
    for j in range(1, CONV_W):
        y = y + xbuf[lo + j:lo + j + rows, :] * cw_ref[j:j + 1, :]
    tail = xbuf[rows + lo:rows + CONV_PAD, :]
    xbuf[lo:CONV_PAD, :] = tail

    @pl.when(c == last)
    def _():
        cfin_ref[0] = tail

    act = _silu(y)
    bd = bd_ref[...]
    beta = _sigmoid(bd)
    g_all = -jnp.exp(pvec_ref[0:1, :]) * _softplus(bd + pvec_ref[1:2, :])

    row = lax.broadcasted_iota(jnp.int32, (CHUNK, CHUNK), 0)
    col = lax.broadcasted_iota(jnp.int32, (CHUNK, CHUNK), 1)
    causal = row >= col
    strict = row > col
    blk_xor = row ^ col
    rr = lax.broadcasted_iota(jnp.int32, (rows, rows), 0)
    cc = lax.broadcasted_iota(jnp.int32, (rows, rows), 1)
    tri = jnp.logical_and(rr >= cc, ((rr ^ cc) >> (CHUNK.bit_length() - 1)) == 0)
    g_cum3 = jnp.dot(tri.astype(BF16), jnp.concatenate(_split3(g_all), axis=-1), preferred_element_type=F32)
    g_cum = g_cum3[:, :LANES] + g_cum3[:, LANES:2 * LANES] + g_cum3[:, 2 * LANES:]
    g_cum_t = g_cum.T

    heads = range(GDN_HEADS)
    units = [(ci, h) for ci in range(nsub) for h in heads]
    hw = GDN_HEADS * GDN_D

    def rsl(ci):
        return slice(ci * CHUNK, (ci + 1) * CHUNK)

    gc = [g_cum[rsl(ci), GDN_HEADS + h:GDN_HEADS + h + 1] for ci, h in units]
    gr = [g_cum_t[GDN_HEADS + h:GDN_HEADS + h + 1, rsl(ci)] for ci, h in units]
    un = range(len(units))
    decay = [jnp.where(causal, jnp.exp(gc[u] - gr[u]), 0.0) for u in un]
    bcol = [beta[rsl(ci), h:h + 1] for ci, h in units]
    q = [act[rsl(ci), GDN_D * h:GDN_D * (h + 1)] for ci, h in units]
    k = [act[rsl(ci), hw + GDN_D * h:hw + GDN_D * (h + 1)] for ci, h in units]
    v = [act[rsl(ci), 2 * hw + GDN_D * h:2 * hw + GDN_D * (h + 1)] for ci, h in units]
    q = [x * lax.rsqrt(jnp.sum(x * x, axis=-1, keepdims=True) + EPS) * (GDN_D ** -0.5) for x in q]
    k = [x * lax.rsqrt(jnp.sum(x * x, axis=-1, keepdims=True) + EPS) for x in k]
    kb = [k[u] * bcol[u] for u in un]
    eg = [jnp.exp(gc[u]) for u in un]
    qa = [_dot_nt(jnp.concatenate([q[u], kb[u]], axis=0), k[u]) for u in un]
    qk = [qa[u][:CHUNK] * decay[u] for u in un]
    a_low = [jnp.where(strict, qa[u][CHUNK:] * decay[u], 0.0) for u in un]
    m = [jnp.where((blk_xor >> 2) == 0, -a_low[u], 0.0) for u in un]
    m2 = [_dot(m[u], m[u]) for u in un]
    n = [m[u] + m2[u] + _dot(m[u], m2[u]) for u in un]
    for lg in range(2, 6):
        low = [jnp.where((blk_xor >> lg) == 1, a_low[u], 0.0) for u in un]
        tl = [low[u] + _dot(n[u], low[u]) for u in un]
        n = [n[u] - (tl[u] + _dot(tl[u], n[u])) for u in un]
    rhs = [jnp.concatenate([v[u] * bcol[u], kb[u] * eg[u]], axis=-1) for u in un]
    sol = [rhs[u] + _dot(n[u], rhs[u]) for u in un]
    wq = [jnp.concatenate([sol[u][:, GDN_D:], q[u] * eg[u]], axis=0) for u in un]
    g_last = [g_cum[(ci + 1) * CHUNK - 1:(ci + 1) * CHUNK, GDN_HEADS + h:GDN_HEADS + h + 1] for ci, h in units]
    k_dec = [k[u] * jnp.exp(g_last[u] - gc[u]) for u in un]
    s = [s_scr[h] for h in heads]
    for ci in range(nsub):
        us = [ci * GDN_HEADS + h for h in heads]
        ws = [_dot(wq[us[h]], s[h]) for h in heads]
        v_new = [sol[us[h]][:, :GDN_D] - ws[h][:CHUNK] for h in heads]
        o = [ws[h][CHUNK:] + _dot(qk[us[h]], v_new[h]) for h in heads]
        s = [s[h] * jnp.exp(g_last[us[h]]) + _dot_tn(k_dec[us[h]], v_new[h]) for h in heads]
        for h in heads:
            zh = z_ref[rsl(ci), GDN_D * h:GDN_D * (h + 1)]
            o_ref[rsl(ci), GDN_D * h:GDN_D * (h + 1)] = _rms(o[h], ng_ref[...]) * _silu(zh)
    for h in heads:
        s_scr[h] = s[h]

    @pl.when(c == last)
    def _():
        sfin_ref[0] = s_scr[...]


def gdn_mixer(p0, bsz, t, conv0, s0, conv_w, a_log, dt_bias, norm_g):
    nsub = min(GDN_CHUNKS_PER_STEP, t // CHUNK)
    rows = nsub * CHUNK
    nc = t // rows
    n = bsz * t
    pvec = jnp.zeros((2, LANES), F32)
    pvec = pvec.at[0, GDN_HEADS:2 * GDN_HEADS].set(a_log).at[1, GDN_HEADS:2 * GDN_HEADS].set(dt_bias)
    vw = GDN_HEADS * GDN_D
    return pl.pallas_call(
        functools.partial(_gdn_body, nsub),
        grid=(bsz, nc),
        in_specs=[pl.BlockSpec((rows, GDN_QKV_W), lambda b, c: (b * nc + c, 0)),
                  pl.BlockSpec((rows, vw), lambda b, c: (b * nc + c, GDN_QKV_W // vw)),
                  pl.BlockSpec((rows, LANES), lambda b, c: (b * nc + c, BD_COL_BLOCK)),
                  pl.BlockSpec((1, CONV_W - 1, GDN_QKV_W), lambda b, c: (b, 0, 0)),
                  pl.BlockSpec((1, GDN_HEADS, GDN_D, GDN_D), lambda b, c: (b, 0, 0, 0)),
                  pl.BlockSpec((CONV_W, GDN_QKV_W), lambda b, c: (0, 0)),
                  pl.BlockSpec((2, LANES), lambda b, c: (0, 0)),
                  pl.BlockSpec((1, GDN_D), lambda b, c: (0, 0))],
        out_specs=[pl.BlockSpec((rows, vw), lambda b, c: (b * nc + c, 0)),
                   pl.BlockSpec((1, GDN_HEADS, GDN_D, GDN_D), lambda b, c: (b, 0, 0, 0)),
                   pl.BlockSpec((1, CONV_W - 1, GDN_QKV_W), lambda b, c: (b, 0, 0))],
        out_shape=[jax.ShapeDtypeStruct((n, vw), F32),
                   jax.ShapeDtypeStruct((bsz, GDN_HEADS, GDN_D, GDN_D), F32),
                   jax.ShapeDtypeStruct((bsz, CONV_W - 1, GDN_QKV_W), F32)],
        scratch_shapes=[pltpu.VMEM((GDN_HEADS, GDN_D, GDN_D), F32),
                        pltpu.VMEM((CONV_PAD + rows, GDN_QKV_W), F32)],
        compiler_params=_cparams(("arbitrary", "arbitrary")),
        name="gdn_mixer",
    )(p0, p0, p0, conv0, s0, conv_w, pvec, norm_g.reshape(1, GDN_D))


def _lru_body(tl, x_ref, gate_ref, conv0_ref, h0_ref, cw_ref, cb_ref, wa_ref, ba_ref, wx_ref, bx_ref,
              lam_ref, y_ref, hfin_ref, cfin_ref, h_scr, xbuf, abuf, bbuf):
    c = pl.program_id(1)
    last = pl.num_programs(1) - 1
    lo = CONV_PAD - (CONV_W - 1)
    pad = tl // 2

    @pl.when(c == 0)
    def _():
        h_scr[...] = h0_ref[0]
        xbuf[lo:CONV_PAD, :] = conv0_ref[0]
        abuf[0:pad, :] = jnp.ones((pad, LRU_W), F32)
        bbuf[0:pad, :] = jnp.zeros((pad, LRU_W), F32)

    xbuf[CONV_PAD:CONV_PAD + tl, :] = x_ref[...]
    xr = xbuf[lo:lo + tl, :] * cw_ref[0:1, :]
    for j in range(1, CONV_W):
        xr = xr + xbuf[lo + j:lo + j + tl, :] * cw_ref[j:j + 1, :]
    tail = xbuf[tl + lo:tl + CONV_PAD, :]
    xbuf[lo:CONV_PAD, :] = tail

    @pl.when(c == last)
    def _():
        cfin_ref[0] = tail

    xr = xr + cb_ref[...]
    ga = jnp.concatenate([_dot(xr[:, LRU_BLOCK_W * n:LRU_BLOCK_W * (n + 1)], wa_ref[n])
                          for n in range(LRU_BLOCKS)], axis=-1)
    gx = jnp.concatenate([_dot(xr[:, LRU_BLOCK_W * n:LRU_BLOCK_W * (n + 1)], wx_ref[n])
                          for n in range(LRU_BLOCKS)], axis=-1)
    gate_a = _sigmoid(ga + ba_ref[...])
    gate_x = _sigmoid(gx + bx_ref[...])
    log_a = -LRU_C * gate_a * _softplus(-lam_ref[...])
    a = jnp.exp(log_a)
    b = jnp.sqrt(1.0 - jnp.exp(2.0 * log_a)) * gate_x * xr
    sub = lax.broadcasted_iota(jnp.int32, (tl, 1), 0) % SUBLANES
    d = 1
    while d < SUBLANES:
        abuf[pad:pad + tl, :] = a
        bbuf[pad:pad + tl, :] = b
        in_group = sub >= d
        a_sh = jnp.where(in_group, abuf[pad - d:pad - d + tl, :], 1.0)
        b_sh = jnp.where(in_group, bbuf[pad - d:pad - d + tl, :], 0.0)
        b = a * b_sh + b
        a = a * a_sh
        d *= 2
    carry = h_scr[...]
    pieces = []
    for r in range(0, tl, SUBLANES):
        h_grp = a[r:r + SUBLANES, :] * carry + b[r:r + SUBLANES, :]
        pieces.append(h_grp)
        carry = h_grp[SUBLANES - 1:SUBLANES, :]
    h = jnp.concatenate(pieces, axis=0)
    h_last = carry
    h_scr[...] = h_last
    y_ref[...] = h * _gelu(gate_ref[...])

    @pl.when(c == last)
    def _():
        hfin_ref[0] = h_last


def lru_mixer(p0, bsz, t, conv0, h0, conv_w, conv_b, w_a, b_a, w_x, b_x, lam):
    tl = min(t, LRU_ROWS)
    nc = t // tl
    n = bsz * t
    row = lambda v: v.reshape(1, LRU_W)
    return pl.pallas_call(
        functools.partial(_lru_body, tl),
        grid=(bsz, nc),
        in_specs=[pl.BlockSpec((tl, LRU_W), lambda b, c: (b * nc + c, IN0_LRU_IN_BLOCK)),
                  pl.BlockSpec((tl, LRU_W), lambda b, c: (b * nc + c, IN0_LRU_GATE_BLOCK)),
                  pl.BlockSpec((1, CONV_W - 1, LRU_W), lambda b, c: (b, 0, 0)),
                  pl.BlockSpec((1, 1, LRU_W), lambda b, c: (b, 0, 0)),
                  pl.BlockSpec((CONV_W, LRU_W), lambda b, c: (0, 0)),
                  pl.BlockSpec((1, LRU_W), lambda b, c: (0, 0)),
                  pl.BlockSpec((LRU_BLOCKS, LRU_BLOCK_W, LRU_BLOCK_W), lambda b, c: (0, 0, 0)),
                  pl.BlockSpec((1, LRU_W), lambda b, c: (0, 0)),
                  pl.BlockSpec((LRU_BLOCKS, LRU_BLOCK_W, LRU_BLOCK_W), lambda b, c: (0, 0, 0)),
                  pl.BlockSpec((1, LRU_W), lambda b, c: (0, 0)),
                  pl.BlockSpec((1, LRU_W), lambda b, c: (0, 0))],
        out_specs=[pl.BlockSpec((tl, LRU_W), lambda b, c: (b * nc + c, 0)),
                   pl.BlockSpec((1, 1, LRU_W), lambda b, c: (b, 0, 0)),
                   pl.BlockSpec((1, CONV_W - 1, LRU_W), lambda b, c: (b, 0, 0))],
        out_shape=[jax.ShapeDtypeStruct((n, LRU_W), F32),
                   jax.ShapeDtypeStruct((bsz, 1, LRU_W), F32),
                   jax.ShapeDtypeStruct((bsz, CONV_W - 1, LRU_W), F32)],
        scratch_shapes=[pltpu.VMEM((1, LRU_W), F32),
                        pltpu.VMEM((CONV_PAD + tl, LRU_W), F32),
                        pltpu.VMEM((tl // 2 + tl, LRU_W), F32),
                        pltpu.VMEM((tl // 2 + tl, LRU_W), F32)],
        compiler_params=_cparams(("arbitrary", "arbitrary")),
        name="lru_mixer",
    )(p0, p0, conv0, h0.reshape(bsz, 1, LRU_W), conv_w, row(conv_b), w_a, row(b_a), w_x, row(b_x), row(lam))


def _xattn_body(x_ref, mk_ref, mv_ref, wq_ref, wo_ref, g_in_ref, g_out_ref, o_ref):
    tm = x_ref.shape[0]
    sub = min(tm, XATTN_SUB_ROWS)
    parts = range(tm // sub)
    x = [x_ref[sub * j:sub * (j + 1), :] for j in parts]
    q = [jnp.dot(_rms(x[j], g_in_ref[...]).astype(BF16), wq_ref[...], preferred_element_type=F32) for j in parts]
    mk = mk_ref[0].astype(BF16)
    mv = mv_ref[0].astype(BF16)
    outs = [[] for _ in parts]
    for h in range(MEM_HEADS):
        sl = slice(MEM_HD * h, MEM_HD * (h + 1))
        s = [_dot_nt(q[j][:, sl], mk[:, sl]) * (MEM_HD ** -0.5) for j in parts]
        m = [jnp.max(s[j], axis=-1, keepdims=True) for j in parts]
        p = [jnp.exp(s[j] - m[j]) for j in parts]
        for j in parts:
            outs[j].append(_dot(p[j], mv[:, sl]) / jnp.sum(p[j], axis=-1, keepdims=True))
    y = [jnp.dot(jnp.concatenate(outs[j], axis=-1).astype(BF16), wo_ref[...], preferred_element_type=F32)
         for j in parts]
    for j in parts:
        o_ref[sub * j:sub * (j + 1), :] = x[j] + _rms(y[j], g_out_ref[...])


def cross_attention(x, bsz, t, mem_k, mem_v, wq, wo, g_in, g_out):
    tm = min(t, XATTN_TM)
    nt = t // tm
    n, d = x.shape
    return pl.pallas_call(
        _xattn_body,
        grid=(bsz, nt),
        in_specs=[pl.BlockSpec((tm, d), lambda b, i: (b * nt + i, 0)),
                  pl.BlockSpec((1, MEM_LEN, MEM_W), lambda b, i: (b, 0, 0)),
                  pl.BlockSpec((1, MEM_LEN, MEM_W), lambda b, i: (b, 0, 0)),
                  pl.BlockSpec((d, MEM_W), lambda b, i: (0, 0)),
                  pl.BlockSpec((MEM_W, d), lambda b, i: (0, 0)),
                  pl.BlockSpec((1, d), lambda b, i: (0, 0)),
                  pl.BlockSpec((1, d), lambda b, i: (0, 0))],
        out_specs=pl.BlockSpec((tm, d), lambda b, i: (b * nt + i, 0)),
        out_shape=jax.ShapeDtypeStruct((n, d), F32),
        compiler_params=_cparams(("arbitrary", "arbitrary")),
        name="cross_attention",
    )(x, mem_k, mem_v, wq, wo, g_in.reshape(1, d), g_out.reshape(1, d))


def _ffn_body(x_ref, g_in_ref, wg_ref, wu_ref, wd_ref, g_out_ref, o_ref, xn_ref, acc_ref):
    f = pl.program_id(1)

    @pl.when(f == 0)
    def _():
        xn_ref[...] = _rms(x_ref[...], g_in_ref[...]).astype(BF16)
        acc_ref[...] = jnp.zeros_like(acc_ref)

    xn = xn_ref[...]
    gate = jnp.dot(xn, wg_ref[...], preferred_element_type=F32)
    up = jnp.dot(xn, wu_ref[...], preferred_element_type=F32)
    acc_ref[...] += jnp.dot((_silu(gate) * up).astype(BF16), wd_ref[...], preferred_element_type=F32)

    @pl.when(f == pl.num_programs(1) - 1)
    def _():
        o_ref[...] = x_ref[...] + _rms(acc_ref[...], g_out_ref[...])


def dense_ffn(x, g_in, wg, wu, wd, g_out, tm=FFN_TM, tf=FFN_TF):
    n, d = x.shape
    ff = wg.shape[1]
    tm = min(tm, n)
    return pl.pallas_call(
        _ffn_body,
        grid=(n // tm, ff // tf),
        in_specs=[pl.BlockSpec((tm, d), lambda i, f: (i, 0)),
                  pl.BlockSpec((1, d), lambda i, f: (0, 0)),
                  pl.BlockSpec((d, tf), lambda i, f: (0, f)),
                  pl.BlockSpec((d, tf), lambda i, f: (0, f)),
                  pl.BlockSpec((tf, d), lambda i, f: (f, 0)),
                  pl.BlockSpec((1, d), lambda i, f: (0, 0))],
        out_specs=pl.BlockSpec((tm, d), lambda i, f: (i, 0)),
        out_shape=jax.ShapeDtypeStruct((n, d), F32),
        scratch_shapes=[pltpu.VMEM((tm, d), BF16), pltpu.VMEM((tm, d), F32)],
        compiler_params=_cparams(("arbitrary", "arbitrary")),
        name="dense_ffn",
    )(x, g_in.reshape(1, d), wg, wu, wd, g_out.reshape(1, d))


def _swa_body(start, nsub, q_ref, kv_ref, cos_ref, sin_ref, kprev_ref, vprev_ref, sink_ref,
              o_ref, krot_ref, kbuf, vbuf):
    c = pl.program_id(1)
    rows_t = nsub * CHUNK

    @pl.when(c == 0)
    def _():
        kbuf[0:WINDOW, :] = kprev_ref[0]
        vbuf[0:WINDOW, :] = vprev_ref[0]

    cos = cos_ref[...]
    sin = sin_ref[...]
    lane = lax.broadcasted_iota(jnp.int32, (rows_t, LANES), 1)
    first_half = (lane % SWA_HD) < (SWA_HD // 2)

    def rope(x):
        outs = []
        for j in range(x.shape[1] // LANES):
            xb = x[:, LANES * j:LANES * (j + 1)]
            fwd = pltpu.roll(xb, LANES - SWA_HD // 2, 1)
            bwd = pltpu.roll(xb, SWA_HD // 2, 1)
            outs.append(xb * cos + jnp.where(first_half, fwd, bwd) * sin)
        return jnp.concatenate(outs, axis=-1)

    q = rope(q_ref[...])
    kv = kv_ref[...]
    k = rope(kv[:, :SWA_KV_W])
    krot_ref[...] = k
    kbuf[WINDOW:WINDOW + rows_t, :] = k
    vbuf[WINDOW:WINDOW + rows_t, :] = kv[:, SWA_KV_W:]

    nk = WINDOW + CHUNK
    rows = SWA_GROUP * CHUNK
    key_off = lax.broadcasted_iota(jnp.int32, (rows, nk), 1) - WINDOW
    row_head = lax.broadcasted_iota(jnp.int32, (rows, 1), 0) // CHUNK
    kvh = range(SWA_KV_HEADS)
    units = [(ci, hk) for ci in range(nsub) for hk in kvh]
    un = range(len(units))
    valid = [start + (c * nsub + ci) * CHUNK + key_off >= 0 for ci in range(nsub)]
    qg = [jnp.concatenate([q[ci * CHUNK:(ci + 1) * CHUNK,
                             SWA_HD * (hk * SWA_GROUP + gi):SWA_HD * (hk * SWA_GROUP + gi + 1)]
                           for gi in range(SWA_GROUP)], axis=0) for ci, hk in units]
    kh = [kbuf[ci * CHUNK:ci * CHUNK + nk, SWA_HD * hk:SWA_HD * (hk + 1)] for ci, hk in units]
    vh = [vbuf[ci * CHUNK:ci * CHUNK + nk, SWA_HD * hk:SWA_HD * (hk + 1)] for ci, hk in units]
    sink_h = []
    for hk in kvh:
        col = jnp.full((rows, 1), sink_ref[hk * SWA_GROUP], F32)
        for gi in range(1, SWA_GROUP):
            col = jnp.where(row_head == gi, sink_ref[hk * SWA_GROUP + gi], col)
        sink_h.append(col)
    sink = [sink_h[hk] for _, hk in units]
    s = [jnp.where(valid[units[u][0]], _dot_nt(qg[u], kh[u]) * (SWA_HD ** -0.5), -jnp.inf) for u in un]
    m = [jnp.maximum(jnp.max(s[u], axis=-1, keepdims=True), sink[u]) for u in un]
    p = [jnp.exp(s[u] - m[u]) for u in un]
    denom = [jnp.sum(p[u], axis=-1, keepdims=True) + jnp.exp(sink[u] - m[u]) for u in un]
    og = [_dot(p[u], vh[u]) / denom[u] for u in un]
    for u, (ci, hk) in enumerate(units):
        for pair in range(SWA_GROUP // 2):
            lo_rows = og[u][CHUNK * 2 * pair:CHUNK * (2 * pair + 1)]
            hi_rows = og[u][CHUNK * (2 * pair + 1):CHUNK * (2 * pair + 2)]
            lane0 = SWA_HD * (hk * SWA_GROUP + 2 * pair)
            o_ref[ci * CHUNK:(ci + 1) * CHUNK, lane0:lane0 + 2 * SWA_HD] = jnp.concatenate(
                [lo_rows, hi_rows], axis=-1)

    k_keep = kbuf[rows_t:rows_t + WINDOW, :]
    v_keep = vbuf[rows_t:rows_t + WINDOW, :]
    kbuf[0:WINDOW, :] = k_keep
    vbuf[0:WINDOW, :] = v_keep


def _rope_tables(start, t):
    half = SWA_HD // 2
    inv_freq = jnp.exp(-math.log(ROPE_THETA) * jnp.arange(half, dtype=F32) / half)
    ang = (start + jnp.arange(t)).astype(F32)[:, None] * inv_freq[None, :]
    cos = jnp.cos(ang)
    sin = jnp.sin(ang)
    return jnp.tile(cos, (1, LANES // half)), jnp.tile(jnp.concatenate([-sin, sin], axis=-1), (1, LANES // SWA_HD))


def swa_mixer(p1, bsz, t, start, k_prev, v_prev, sinks):
    nsub = min(SWA_CHUNKS_PER_STEP, t // CHUNK)
    rows = nsub * CHUNK
    nc = t // rows
    n = bsz * t
    qw = SWA_Q_HEADS * SWA_HD
    cos, sin = _rope_tables(start, t)
    return pl.pallas_call(
        functools.partial(_swa_body, start, nsub),
        grid=(bsz, nc),
        in_specs=[pl.BlockSpec((rows, qw), lambda b, c: (b * nc + c, 0)),
                  pl.BlockSpec((rows, 2 * SWA_KV_W), lambda b, c: (b * nc + c, 3 * qw // (2 * SWA_KV_W))),
                  pl.BlockSpec((rows, LANES), lambda b, c: (c, 0)),
                  pl.BlockSpec((rows, LANES), lambda b, c: (c, 0)),
                  pl.BlockSpec((1, WINDOW, SWA_KV_W), lambda b, c: (b, 0, 0)),
                  pl.BlockSpec((1, WINDOW, SWA_KV_W), lambda b, c: (b, 0, 0)),
                  pl.BlockSpec(memory_space=pltpu.SMEM)],
        out_specs=[pl.BlockSpec((rows, qw), lambda b, c: (b * nc + c, 0)),
                   pl.BlockSpec((rows, SWA_KV_W), lambda b, c: (b * nc + c, 0))],
        out_shape=[jax.ShapeDtypeStruct((n, qw), F32),
                   jax.ShapeDtypeStruct((n, SWA_KV_W), F32)],
        scratch_shapes=[pltpu.VMEM((WINDOW + rows, SWA_KV_W), F32),
                        pltpu.VMEM((WINDOW + rows, SWA_KV_W), F32)],
        compiler_params=_cparams(("arbitrary", "arbitrary")),
        name="swa_mixer",
    )(p1, p1, cos, sin, k_prev, v_prev, sinks)


def _smlp_body(lc, u_ref, v_ref, lg_ref, lb_ref, ws_ref, bs_ref, y_ref, vn_ref):
    v = _gelu(v_ref[...])
    mu = jnp.mean(v, axis=-1, keepdims=True)
    vc = v - mu
    vn = vc * lax.rsqrt(jnp.mean(vc * vc, axis=-1, keepdims=True) + EPS) * lg_ref[...] + lb_ref[...]
    vn_ref[...] = vn
    u = _gelu(u_ref[...])
    row = lax.broadcasted_iota(jnp.int32, (lc, lc), 0)
    col = lax.broadcasted_iota(jnp.int32, (lc, lc), 1)
    for g in range(SMLP_GROUPS):
        sl = slice(SMLP_GROUP_W * g, SMLP_GROUP_W * (g + 1))
        w = jnp.where(row >= col, ws_ref[g, 0:lc, 0:lc], 0.0)
        s = _dot(w, vn[:, sl]) + bs_ref[0:lc, g:g + 1]
        y_ref[:, sl] = u[:, sl] * s


def smlp_mixer(p1, bsz, t, ln_g, ln_b, w_spatial, b_spatial):
    lc = min(SMLP_CHUNK, t)
    n = bsz * t
    row = lambda v: v.reshape(1, SMLP_W)
    return pl.pallas_call(
        functools.partial(_smlp_body, lc),
        grid=(n // lc,),
        in_specs=[pl.BlockSpec((lc, SMLP_W), lambda i: (i, IN1_U_BLOCK)),
                  pl.BlockSpec((lc, SMLP_W), lambda i: (i, IN1_VG_BLOCK)),
                  pl.BlockSpec((1, SMLP_W), lambda i: (0, 0)),
                  pl.BlockSpec((1, SMLP_W), lambda i: (0, 0)),
                  pl.BlockSpec((SMLP_GROUPS, SMLP_CHUNK, SMLP_CHUNK), lambda i: (0, 0, 0)),
                  pl.BlockSpec((SMLP_CHUNK, SMLP_GROUPS), lambda i: (0, 0))],
        out_specs=[pl.BlockSpec((lc, SMLP_W), lambda i: (i, 0)),
                   pl.BlockSpec((lc, SMLP_W), lambda i: (i, 0))],
        out_shape=[jax.ShapeDtypeStruct((n, SMLP_W), F32),
                   jax.ShapeDtypeStruct((n, SMLP_W), F32)],
        compiler_params=_cparams(("arbitrary",)),
        name="smlp_mixer",
    )(p1, p1, row(ln_g), row(ln_b), w_spatial, b_spatial.T)


def _router_body(x_ref, g_ref, wr_ref, idx_ref, gate_ref):
    hn = _rms(x_ref[...], g_ref[...])
    logits = lax.dot_general(wr_ref[...], hn, (((1,), (1,)), ((), ())), precision=HIGHEST,
                             preferred_element_type=F32)
    e_iota = lax.broadcasted_iota(jnp.int32, logits.shape, 0)
    m1 = jnp.max(logits, axis=0, keepdims=True)
    i1 = jnp.min(jnp.where(logits == m1, e_iota, N_EXPERTS), axis=0, keepdims=True)
    rest = jnp.where(e_iota == i1, -jnp.inf, logits)
    m2 = jnp.max(rest, axis=0, keepdims=True)
    i2 = jnp.min(jnp.where(rest == m2, e_iota, N_EXPERTS), axis=0, keepdims=True)
    e2 = jnp.exp(m2 - m1)
    den = 1.0 + e2
    idx_ref[...] = jnp.concatenate([i1, i2], axis=0)
    tm = logits.shape[1]
    gates = jnp.concatenate([1.0 / den, e2 / den, jnp.zeros((LANES - 2, tm), F32)], axis=0)
    gate_ref[...] = gates.T


def moe_router(x, g, w_router, tm=ROUTER_TM):
    n, d = x.shape
    tm = min(tm, n)
    return pl.pallas_call(
        _router_body,
        grid=(n // tm,),
        in_specs=[pl.BlockSpec((tm, d), lambda i: (i, 0)),
                  pl.BlockSpec((1, d), lambda i: (0, 0)),
                  pl.BlockSpec((N_EXPERTS, d), lambda i: (0, 0))],
        out_specs=[pl.BlockSpec((2, tm), lambda i: (0, i)),
                   pl.BlockSpec((tm, LANES), lambda i: (i, 0))],
        out_shape=[jax.ShapeDtypeStruct((2, n), jnp.int32),
                   jax.ShapeDtypeStruct((n, LANES), F32)],
        compiler_params=_cparams(("arbitrary",)),
        name="moe_router",
    )(x, g.reshape(1, d), w_router.T)


def _moe_block_rows(n_tokens):
    return MOE_BM if 2 * n_tokens // N_EXPERTS >= 4 * MOE_BM else MOE_BM // 2


def _moe_plan(top_idx, bm):
    n = top_idx.shape[1]
    flat_e = top_idx.reshape(-1)
    onehot = (flat_e[:, None] == jnp.arange(N_EXPERTS, dtype=jnp.int32)[None, :]).astype(jnp.int32)
    rank = jnp.sum(jnp.cumsum(onehot, axis=0) * onehot, axis=1) - 1
    counts = jnp.sum(onehot, axis=0)
    padded = (counts + bm - 1) // bm * bm
    pad_end = jnp.cumsum(padded)
    pad_start = pad_end - padded
    dest = jnp.sum(onehot * pad_start[None, :], axis=1) + rank
    n_blk = -(-2 * n // bm) + N_EXPERTS
    blk_start = jnp.arange(n_blk, dtype=jnp.int32) * bm
    blk_e = jnp.minimum(jnp.sum((blk_start[:, None] >= pad_end[None, :]).astype(jnp.int32), axis=1),
                        N_EXPERTS - 1)
    blk_valid = jnp.clip((pad_start + counts)[blk_e] - blk_start, 0, bm).astype(jnp.int32)
    n_active = (pad_end[-1] // bm).astype(jnp.int32).reshape(1)
    return dest.reshape(2, n).astype(jnp.int32), blk_e.astype(jnp.int32), blk_valid, n_active, n_blk


def _moe_dispatch_body(tc, dest_ref, x_ref, g_ref, xs_hbm, hn_scr, sem):
    d = x_ref.shape[1]
    hn_scr[...] = _rms(x_ref[...], g_ref[...]).reshape(tc // SUBLANES, SUBLANES, d)

    def start(i8, carry):
        for j in range(SUBLANES):
            for slot in range(2):
                dst_row = dest_ref[0, slot, i8 * SUBLANES + j]
                pltpu.make_async_copy(hn_scr.at[i8, pl.ds(j, 1), :], xs_hbm.at[pl.ds(dst_row, 1), :],
                                      sem).start(priority=slot)
        return carry

    lax.fori_loop(0, tc // SUBLANES, start, 0)
    for _ in range(2):
        pltpu.make_async_copy(x_ref, xs_hbm.at[pl.ds(0, tc), :], sem).wait()


def _moe_dispatch_into_body(tc, dest_ref, x_ref, g_ref, xs_prev_hbm, xs_hbm, hn_scr, sem):
    del xs_prev_hbm
    _moe_dispatch_body(tc, dest_ref, x_ref, g_ref, xs_hbm, hn_scr, sem)


def moe_dispatch(x, g, dest_blocks, n_rows, tc, xs_prev=None):
    n, d = x.shape
    in_specs = [pl.BlockSpec((1, 2, tc), lambda i: (i, 0, 0), memory_space=pltpu.SMEM),
                pl.BlockSpec((tc, d), lambda i: (i, 0)),
                pl.BlockSpec((1, d), lambda i: (0, 0))]
    args = [dest_blocks, x, g.reshape(1, d)]
    body, aliases = _moe_dispatch_body, {}
    if xs_prev is not None:
        in_specs.append(pl.BlockSpec(memory_space=pl.ANY))
        args.append(xs_prev)
        body, aliases = _moe_dispatch_into_body, {3: 0}
    return pl.pallas_call(
        functools.partial(body, tc),
        grid=(n // tc,),
        in_specs=in_specs,
        out_specs=pl.BlockSpec(memory_space=pl.ANY),
        out_shape=jax.ShapeDtypeStruct((n_rows, d), F32),
        scratch_shapes=[pltpu.VMEM((tc // SUBLANES, SUBLANES, d), F32), pltpu.SemaphoreType.DMA(())],
        input_output_aliases=aliases,
        compiler_params=_cparams(("arbitrary",)),
        name="moe_dispatch",
    )(*args)


def _moe_ffn_body(blk_e_ref, blk_valid_ref, nact_ref, xs_ref, wg_hbm, wu_hbm, wd_hbm, ys_ref,
                  wg_scr, wu_scr, wd_scr, sem):
    i = pl.program_id(0)
    active = i < nact_ref[0]
    e = blk_e_ref[i]
    new_expert = jnp.logical_or(i == 0, e != blk_e_ref[jnp.maximum(i - 1, 0)])

    @pl.when(jnp.logical_and(active, new_expert))
    def _():
        copies = [pltpu.make_async_copy(w_hbm.at[e], w_scr, sem.at[j])
                  for j, (w_hbm, w_scr) in enumerate(((wg_hbm, wg_scr), (wu_hbm, wu_scr), (wd_hbm, wd_scr)))]
        for cp in copies:
            cp.start()
        for cp in copies:
            cp.wait()

    @pl.when(active)
    def _():
        row = lax.broadcasted_iota(jnp.int32, (xs_ref.shape[0], 1), 0)
        xb = jnp.where(row < blk_valid_ref[i], xs_ref[...], 0.0).astype(BF16)
        acc = None
        for j in range(FF_EXPERT // MOE_TF):
            sl = slice(MOE_TF * j, MOE_TF * (j + 1))
            gate = jnp.dot(xb, wg_scr[:, sl], preferred_element_type=F32)
            up = jnp.dot(xb, wu_scr[:, sl], preferred_element_type=F32)
            part = jnp.dot((_silu(gate) * up).astype(BF16), wd_scr[sl, :], preferred_element_type=F32)
            acc = part if acc is None else acc + part
        ys_ref[...] = acc

    @pl.when(jnp.logical_not(active))
    def _():
        ys_ref[...] = jnp.zeros_like(ys_ref)


def moe_expert_ffn(xs, blk_e, blk_valid, n_active, wg, wu, wd, bm):
    n_rows, d = xs.shape
    n_blk = n_rows // bm
    return pl.pallas_call(
        _moe_ffn_body,
        grid_spec=pltpu.PrefetchScalarGridSpec(
            num_scalar_prefetch=3,
            grid=(n_blk,),
            in_specs=[pl.BlockSpec((bm, d), lambda i, be, bv, na: (jnp.minimum(i, na[0] - 1), 0)),
                      pl.BlockSpec(memory_space=pl.ANY),
                      pl.BlockSpec(memory_space=pl.ANY),
                      pl.BlockSpec(memory_space=pl.ANY)],
            out_specs=pl.BlockSpec((bm, d), lambda i, be, bv, na: (i, 0)),
            scratch_shapes=[pltpu.VMEM((d, FF_EXPERT), BF16), pltpu.VMEM((d, FF_EXPERT), BF16),
                            pltpu.VMEM((FF_EXPERT, d), BF16), pltpu.SemaphoreType.DMA((3,))]),
        out_shape=jax.ShapeDtypeStruct((n_rows, d), F32),
        compiler_params=_cparams(("arbitrary",)),
        name="moe_expert_ffn",
    )(blk_e, blk_valid, n_active, xs, wg, wu, wd)


def _moe_combine_body(tc, pos_ref, pos_next_ref, ys_hbm, gate_ref, x_ref, g_ref, o_ref, buf, sem):
    i = pl.program_id(0)
    nb = pl.num_programs(0)
    slot = i % 2

    def issue(p_ref, s):
        def start(i8, carry):
            for j in range(SUBLANES):
                for choice in range(2):
                    src_row = p_ref[0, choice, i8 * SUBLANES + j]
                    pltpu.make_async_copy(ys_hbm.at[pl.ds(src_row, 1), :], buf.at[s, choice, i8, pl.ds(j, 1), :],
                                          sem.at[s]).start(priority=choice)
            return carry

        lax.fori_loop(0, tc // SUBLANES, start, 0)

    @pl.when(i == 0)
    def _():
        issue(pos_ref, 0)

    @pl.when(i + 1 < nb)
    def _():
        issue(pos_next_ref, 1 - slot)

    d = x_ref.shape[1]
    for choice in range(2):
        pltpu.make_async_copy(ys_hbm.at[pl.ds(0, tc), :], o_ref, sem.at[slot]).wait()
    gates = gate_ref[...]
    y = (gates[:, 0:1] * buf[slot, 0].reshape(tc, d) + gates[:, 1:2] * buf[slot, 1].reshape(tc, d))
    o_ref[...] = x_ref[...] + _rms(y, g_ref[...])


def moe_combine(ys, pos_blocks, gates, x, g, tc):
    n, d = x.shape
    nb = n // tc
    return pl.pallas_call(
        functools.partial(_moe_combine_body, tc),
        grid=(nb,),
        in_specs=[pl.BlockSpec((1, 2, tc), lambda i: (i, 0, 0), memory_space=pltpu.SMEM),
                  pl.BlockSpec((1, 2, tc), lambda i: (jnp.minimum(i + 1, nb - 1), 0, 0), memory_space=pltpu.SMEM),
                  pl.BlockSpec(memory_space=pl.ANY),
                  pl.BlockSpec((tc, LANES), lambda i: (i, 0)),
                  pl.BlockSpec((tc, d), lambda i: (i, 0)),
                  pl.BlockSpec((1, d), lambda i: (0, 0))],
        out_specs=pl.BlockSpec((tc, d), lambda i: (i, 0)),
        out_shape=jax.ShapeDtypeStruct((n, d), F32),
        scratch_shapes=[pltpu.VMEM((2, 2, tc // SUBLANES, SUBLANES, d), F32), pltpu.SemaphoreType.DMA((2,))],
        compiler_params=_cparams(("arbitrary",)),
        name="moe_combine",
    )(pos_blocks, pos_blocks, ys, gates, x, g.reshape(1, d))


def moe_block(xs_in, g_in, w_router, wg, wu, wd, g_out, tc=MOE_TC):
    routed = [moe_router(x, g_in, w_router) for x in xs_in]
    sizes = [x.shape[0] for x in xs_in]
    n_all = sum(sizes)
    bm = _moe_block_rows(n_all)
    dest, blk_e, blk_valid, n_active, n_blk = _moe_plan(jnp.concatenate([r[0] for r in routed], axis=1), bm)
    xs, dest_blocks, off = None, [], 0
    for x, n in zip(xs_in, sizes):
        t = min(tc, n)
        dest_blocks.append(dest[:, off:off + n].reshape(2, n // t, t).transpose(1, 0, 2))
        xs = moe_dispatch(x, g_in, dest_blocks[-1], n_blk * bm, t, xs_prev=xs)
        off += n
    ys = moe_expert_ffn(xs, blk_e, blk_valid, n_active, wg, wu, wd, bm)
    return [moe_combine(ys, db, r[1], x, g_out, min(tc, x.shape[0]))
            for x, db, r in zip(xs_in, dest_blocks, routed)]


def _forward(x, start, keep, mem_k, mem_v, gdn_conv0, gdn_s0, lru_conv0, lru_h0, swa_k0, swa_v0, p):
    bsz, t, d = x.shape
    x = x.reshape(bsz * t, d)
    ng = p['norm_g']
    p0 = norm_matmul(x, ng[0, 0], p['w_in0'], IN_PROJ_TM, IN0_TN)
    o_gdn, gdn_s, gdn_conv = gdn_mixer(p0, bsz, t, gdn_conv0, gdn_s0, p['gdn_conv_w'], p['gdn_a_log'],
                                       p['gdn_dt_bias'], p['gdn_norm_g'])
    y_lru, lru_h, lru_conv = lru_mixer(p0, bsz, t, lru_conv0, lru_h0, p['lru_conv_w'], p['lru_conv_b'],
                                       p['lru_w_a'], p['lru_b_a'], p['lru_w_x'], p['lru_b_x'], p['lru_lambda'])
    x = outproj_norm_resid(o_gdn, y_lru, p['w_out0'], x, ng[0, 1])
    x = cross_attention(x, bsz, t, mem_k[0], mem_v[0], p['w_xq'][0], p['w_xo'][0], ng[0, 2], ng[0, 3])
    x = dense_ffn(x, ng[0, 4], p['w_ff_gate'], p['w_ff_up'], p['w_ff_down'], ng[0, 5])
    p1 = norm_matmul(x, ng[1, 0], p['w_in1'], IN_PROJ_TM, IN1_TN)
    attn, k_rot = swa_mixer(p1, bsz, t, start, swa_k0, swa_v0, p['swa_sinks'])
    y_smlp, smlp_v = smlp_mixer(p1, bsz, t, p['smlp_ln_g'], p['smlp_ln_b'], p['w_spatial'], p['b_spatial'])
    x = outproj_norm_resid(attn, y_smlp, p['w_out1'], x, ng[1, 1])
    x = cross_attention(x, bsz, t, mem_k[1], mem_v[1], p['w_xq'][1], p['w_xo'][1], ng[1, 2], ng[1, 3])
    k_rows = k_rot.reshape(bsz, t, SWA_KV_W)[:, t - keep:].reshape(bsz, keep, SWA_KV_HEADS, SWA_HD)
    v_rows = p1.reshape(bsz, t, IN1_W)[:, t - keep:, IN1_W - SWA_KV_W:].reshape(bsz, keep, SWA_KV_HEADS, SWA_HD)
    return (x, gdn_conv, gdn_s, lru_conv, lru_h.reshape(bsz, LRU_W), k_rows, v_rows, smlp_v.reshape(bsz, t, SMLP_W))


def _prepare_weights(norm_g, w_in0, gdn_conv_w, gdn_a_log, gdn_dt_bias, gdn_norm_g, lru_conv_w, lru_conv_b,
                     lru_w_a, lru_b_a, lru_w_x, lru_b_x, lru_lambda, w_out0, w_in1, swa_sinks, smlp_ln_g,
                     smlp_ln_b, w_spatial, b_spatial, w_out1, w_xq, w_xo, w_ff_gate, w_ff_up, w_ff_down,
                     w_router, w_moe_gate, w_moe_up, w_moe_down):
    qkvz_w = GDN_QKV_W + GDN_HEADS * GDN_D
    bd_w = 2 * GDN_HEADS
    w0 = jnp.concatenate([w_in0[:, :qkvz_w], w_in0[:, qkvz_w + bd_w:], w_in0[:, qkvz_w:qkvz_w + bd_w],
                          jnp.zeros((D_MODEL, IN0_PAD_W - w_in0.shape[1]), w_in0.dtype)], axis=1)
    qw = SWA_Q_HEADS * SWA_HD
    w1 = jnp.concatenate([w_in1[:, :qw], w_in1[:, qw + 2 * SWA_KV_W:], w_in1[:, qw:qw + 2 * SWA_KV_W]], axis=1)
    return dict(
        norm_g=norm_g, w_in0=w0.astype(BF16), gdn_conv_w=gdn_conv_w, gdn_a_log=gdn_a_log,
        gdn_dt_bias=gdn_dt_bias, gdn_norm_g=gdn_norm_g, lru_conv_w=lru_conv_w, lru_conv_b=lru_conv_b,
        lru_w_a=lru_w_a, lru_b_a=lru_b_a, lru_w_x=lru_w_x, lru_b_x=lru_b_x, lru_lambda=lru_lambda,
        w_out0=w_out0.astype(BF16), w_in1=w1.astype(BF16), swa_sinks=swa_sinks, smlp_ln_g=smlp_ln_g,
        smlp_ln_b=smlp_ln_b, w_spatial=w_spatial, b_spatial=b_spatial, w_out1=w_out1.astype(BF16),
        w_xq=w_xq.astype(BF16), w_xo=w_xo.astype(BF16), w_ff_gate=w_ff_gate.astype(BF16),
        w_ff_up=w_ff_up.astype(BF16), w_ff_down=w_ff_down.astype(BF16), w_router=w_router,
        w_moe_gate=w_moe_gate.astype(BF16), w_moe_up=w_moe_up.astype(BF16), w_moe_down=w_moe_down.astype(BF16))


def kernel(x_prompt, x_sample, mem_prompt, cache_mem_k, cache_mem_v, state_gdn, state_gdn_conv, state_rglru_h, state_rglru_conv, cache_swa_k, cache_swa_v, norm_g, mem_norm_g, w_in0, gdn_conv_w, gdn_a_log, gdn_dt_bias, gdn_norm_g, lru_conv_w, lru_conv_b, lru_w_a, lru_b_a, lru_w_x, lru_b_x, lru_lambda, w_out0, w_in1, swa_sinks, smlp_ln_g, smlp_ln_b, w_spatial, b_spatial, w_out1, w_xq, w_xk, w_xv, w_xo, w_ff_gate, w_ff_up, w_ff_down, w_router, w_moe_gate, w_moe_up, w_moe_down):
    p = _prepare_weights(norm_g, w_in0, gdn_conv_w, gdn_a_log, gdn_dt_bias, gdn_norm_g, lru_conv_w, lru_conv_b,
                         lru_w_a, lru_b_a, lru_w_x, lru_b_x, lru_lambda, w_out0, w_in1, swa_sinks, smlp_ln_g,
                         smlp_ln_b, w_spatial, b_spatial, w_out1, w_xq, w_xo, w_ff_gate, w_ff_up, w_ff_down,
                         w_router, w_moe_gate, w_moe_up, w_moe_down)
    bsz, t, d = x_prompt.shape
    depth = w_xk.shape[0]
    mem_flat = mem_prompt.reshape(bsz * MEM_LEN, d)
    mem_k_p = jnp.stack([norm_matmul(mem_flat, mem_norm_g[l], w_xk[l].astype(BF16), MEM_PROJ_TM, MEM_W)
                         for l in range(depth)]).reshape(depth, bsz, MEM_LEN, MEM_W)
    mem_v_p = jnp.stack([norm_matmul(mem_flat, mem_norm_g[l], w_xv[l].astype(BF16), MEM_PROJ_TM, MEM_W)
                         for l in range(depth)]).reshape(depth, bsz, MEM_LEN, MEM_W)
    keep = min(WINDOW, t)
    (x_p, gdn_conv_p, gdn_s_p, lru_conv_p, lru_h_p, k_rows_p, v_rows_p, _) = _forward(
        x_prompt, 0, keep, mem_k_p, mem_v_p,
        jnp.zeros((bsz, CONV_W - 1, GDN_QKV_W), F32), jnp.zeros((bsz, GDN_HEADS, GDN_D, GDN_D), F32),
        jnp.zeros((bsz, CONV_W - 1, LRU_W), F32), jnp.zeros((bsz, LRU_W), F32),
        jnp.zeros((bsz, WINDOW, SWA_KV_W), F32), jnp.zeros((bsz, WINDOW, SWA_KV_W), F32), p)
    dbs, dec_t = x_sample.shape[:2]
    n_prev = cache_swa_k.shape[1]
    assert n_prev == WINDOW
    (x_s, gdn_conv_s, gdn_s_s, lru_conv_s, lru_h_s, k_rows_s, v_rows_s, smlp_v_s) = _forward(
        x_sample, PAST_LEN, dec_t, cache_mem_k.reshape(depth, dbs, MEM_LEN, MEM_W),
        cache_mem_v.reshape(depth, dbs, MEM_LEN, MEM_W), state_gdn_conv, state_gdn, state_rglru_conv,
        state_rglru_h, cache_swa_k.reshape(dbs, n_prev, SWA_KV_W), cache_swa_v.reshape(dbs, n_prev, SWA_KV_W), p)
    y_p, y_s = moe_block([x_p, x_s], norm_g[1, 4], p['w_router'], p['w_moe_gate'], p['w_moe_up'],
                         p['w_moe_down'], norm_g[1, 5])
    y_p = y_p.reshape(bsz, t, d)
    y_s = y_s.reshape(dbs, dec_t, d)
    shape5 = (depth, bsz, MEM_LEN, MEM_HEADS, MEM_HD)
    return (y_p, y_s, mem_k_p.reshape(shape5), mem_v_p.reshape(shape5), gdn_s_p, gdn_conv_p, lru_h_p, lru_conv_p,
            k_rows_p, v_rows_p, gdn_s_s, gdn_conv_s, lru_h_s, lru_conv_s,
            k_rows_s, v_rows_s, smlp_v_s)
```

```python
import functools
import math

import jax
import jax.numpy as jnp
from jax import lax
from jax.experimental import pallas as pl
from jax.experimental.pallas import tpu as pltpu

F32 = jnp.float32
BF16 = jnp.bfloat16
LOG2_E = math.log2(math.e)

D_MODEL = 2048
EPS = 1e-6
CHUNK = 64
CONV_W = 4
CONV_PAD = 8
GDN_HEADS = 8
GDN_D = 128
GDN_QKV_W = 3 * GDN_HEADS * GDN_D
GDN_CHUNKS_PER_STEP = 2
LRU_W = 1024
LRU_BLOCKS = 8
LRU_BLOCK_W = LRU_W // LRU_BLOCKS
LRU_C = 8.0
IN0_PAD_W = 6400
BD_COL_BLOCK = 6144 // 128
SWA_Q_HEADS = 16
SWA_KV_HEADS = 4
SWA_GROUP = SWA_Q_HEADS // SWA_KV_HEADS
SWA_HD = 64
SWA_KV_W = SWA_KV_HEADS * SWA_HD
WINDOW = 128
SWA_CHUNKS_PER_STEP = 4
ROPE_THETA = 10000.0
PAST_LEN = 4096
SMLP_GROUPS = 8
SMLP_GROUP_W = 128
SMLP_W = SMLP_GROUPS * SMLP_GROUP_W
SMLP_CHUNK = 128
IN1_W = 3584
MEM_LEN = 256
MEM_HEADS = 4
MEM_HD = 128
MEM_W = MEM_HEADS * MEM_HD
XATTN_SUB_ROWS = 512
FF_DENSE = 5632
N_EXPERTS = 8
FF_EXPERT = 2816
MOE_BM = 512
MOE_TF = FF_EXPERT // 11
LANES = 128
SUBLANES = 8

VMEM_LIMIT_MB = 56

IN_PROJ_TM = 1024
IN0_TN = 1280
IN1_TN = 1792
MEM_PROJ_TM = 512
OUT_PROJ_TM = 512
XATTN_TM = 1024
FFN_TM = 512
FFN_TF = 512
LRU_ROWS = 512
ROUTER_TM = 512
MOE_TC = 256
IN0_LRU_IN_BLOCK = 4
IN0_LRU_GATE_BLOCK = 5
IN1_U_BLOCK = 1
IN1_VG_BLOCK = 2


def _cparams(semantics, vmem_mb=VMEM_LIMIT_MB):
    return pltpu.CompilerParams(dimension_semantics=semantics, vmem_limit_bytes=vmem_mb * 2 ** 20)


def _rms(x, g):
    return x * lax.rsqrt(jnp.mean(x * x, axis=-1, keepdims=True) + EPS) * g


def _sigmoid(x):
    return 1.0 / (1.0 + jnp.exp2(x * (-LOG2_E)))


def _silu(x):
    return x * _sigmoid(x)


def _softplus(x):
    return jnp.maximum(x, 0.0) + jnp.log(1.0 + jnp.exp(-jnp.abs(x)))


def _gelu(x):
    c = math.sqrt(2.0 / math.pi)
    return 0.5 * x * (1.0 + jnp.tanh(x * (c + (0.044715 * c) * (x * x))))


def _dot(a, b):
    return jnp.dot(a.astype(BF16), b.astype(BF16), preferred_element_type=F32)


def _dot_nt(a, b):
    return lax.dot_general(a.astype(BF16), b.astype(BF16), (((1,), (1,)), ((), ())),
                           preferred_element_type=F32)


def _dot_tn(a, b):
    return lax.dot_general(a.astype(BF16), b.astype(BF16), (((0,), (0,)), ((), ())),
                           preferred_element_type=F32)


def _split3(x):
    x1 = x.astype(BF16)
    r1 = x - x1.astype(F32)
    x2 = r1.astype(BF16)
    x3 = (r1 - x2.astype(F32)).astype(BF16)
    return x1, x2, x3


def _norm_matmul_body(x_ref, g_ref, w_ref, o_ref, xn_ref):
    @pl.when(pl.program_id(1) == 0)
    def _():
        xn_ref[...] = _rms(x_ref[...], g_ref[...]).astype(BF16)

    o_ref[...] = jnp.dot(xn_ref[...], w_ref[...], preferred_element_type=F32)


def norm_matmul(x, g, w, tm, tn):
    n, k = x.shape
    nout = w.shape[1]
    tm = min(tm, n)
    return pl.pallas_call(
        _norm_matmul_body,
        grid=(n // tm, nout // tn),
        in_specs=[pl.BlockSpec((tm, k), lambda i, j: (i, 0)),
                  pl.BlockSpec((1, k), lambda i, j: (0, 0)),
                  pl.BlockSpec((k, tn), lambda i, j: (0, j))],
        out_specs=pl.BlockSpec((tm, tn), lambda i, j: (i, j)),
        out_shape=jax.ShapeDtypeStruct((n, nout), F32),
        scratch_shapes=[pltpu.VMEM((tm, k), BF16)],
        compiler_params=_cparams(("arbitrary", "arbitrary")),
        name="norm_matmul",
    )(x, g.reshape(1, k), w)


def _outproj_body(a_ref, b_ref, wa_ref, wb_ref, r_ref, g_ref, o_ref):
    acc = jnp.dot(a_ref[...].astype(BF16), wa_ref[...], preferred_element_type=F32)
    acc = acc + jnp.dot(b_ref[...].astype(BF16), wb_ref[...], preferred_element_type=F32)
    o_ref[...] = r_ref[...] + _rms(acc, g_ref[...])


def outproj_norm_resid(a, b, w, resid, g, tm=OUT_PROJ_TM):
    n, ka = a.shape
    kb = b.shape[1]
    d = w.shape[1]
    tm = min(tm, n)
    return pl.pallas_call(
        _outproj_body,
        grid=(n // tm,),
        in_specs=[pl.BlockSpec((tm, ka), lambda i: (i, 0)),
                  pl.BlockSpec((tm, kb), lambda i: (i, 0)),
                  pl.BlockSpec((ka, d), lambda i: (0, 0)),
                  pl.BlockSpec((kb, d), lambda i: (1, 0)),
                  pl.BlockSpec((tm, d), lambda i: (i, 0)),
                  pl.BlockSpec((1, d), lambda i: (0, 0))],
        out_specs=pl.BlockSpec((tm, d), lambda i: (i, 0)),
        out_shape=jax.ShapeDtypeStruct((n, d), F32),
        compiler_params=_cparams(("arbitrary",)),
        name="outproj_norm_resid",
    )(a, b, w, w, resid, g.reshape(1, d))


def _gdn_body(nsub, qkv_ref, z_ref, bd_ref, conv0_ref, s0_ref, cw_ref, pvec_ref, ng_ref,
              o_ref, sfin_ref, cfin_ref, s_scr, xbuf):
    c = pl.program_id(1)
    last = pl.num_programs(1) - 1
    lo = CONV_PAD - (CONV_W - 1)
    rows = nsub * CHUNK

    @pl.when(c == 0)
    def _():
        s_scr[...] = s0_ref[0]
        xbuf[lo:CONV_PAD, :] = conv0_ref[0]

    xbuf[CONV_PAD:CONV_PAD + rows, :] = qkv_ref[...]
    y = xbuf[lo:lo + rows, :] * cw_ref[0:1, :]
    for j in range(1, CONV_W):
        y = y + xbuf[lo + j:lo + j + rows, :] * cw_ref[j:j + 1, :]
    tail = xbuf[rows + lo:rows + CONV_PAD, :]
    xbuf[lo:CONV_PAD, :] = tail

    @pl.when(c == last)
    def _():
        cfin_ref[0] = tail

    act = _silu(y)
    bd = bd_ref[...]
    beta = _sigmoid(bd)
    g_all = -jnp.exp(pvec_ref[0:1, :]) * _softplus(bd + pvec_ref[1:2, :])

    row = lax.broadcasted_iota(jnp.int32, (CHUNK, CHUNK), 0)
    col = lax.broadcasted_iota(jnp.int32, (CHUNK, CHUNK), 1)
    causal = row >= col
    strict = row > col
    blk_xor = row ^ col
    rr = lax.broadcasted_iota(jnp.int32, (rows, rows), 0)
    cc = lax.broadcasted_iota(jnp.int32, (rows, rows), 1)
    tri = jnp.logical_and(rr >= cc, ((rr ^ cc) >> (CHUNK.bit_length() - 1)) == 0)
    g_cum3 = jnp.dot(tri.astype(BF16), jnp.concatenate(_split3(g_all), axis=-1), preferred_element_type=F32)
    g_cum = (g_cum3[:, :LANES] + g_cum3[:, LANES:2 * LANES] + g_cum3[:, 2 * LANES:]) * LOG2_E
    g_cum_t = g_cum.T

    heads = range(GDN_HEADS)
    units = [(ci, h) for ci in range(nsub) for h in heads]
    hw = GDN_HEADS * GDN_D

    def rsl(ci):
        return slice(ci * CHUNK, (ci + 1) * CHUNK)

    gc = [g_cum[rsl(ci), GDN_HEADS + h:GDN_HEADS + h + 1] for ci, h in units]
    gr = [g_cum_t[GDN_HEADS + h:GDN_HEADS + h + 1, rsl(ci)] for ci, h in units]
    un = range(len(units))
    decay = [jnp.where(causal, jnp.exp2(gc[u] - gr[u]), 0.0) for u in un]
    bcol = [beta[rsl(ci), h:h + 1] for ci, h in units]
    q = [act[rsl(ci), GDN_D * h:GDN_D * (h + 1)] for ci, h in units]
    k = [act[rsl(ci), hw + GDN_D * h:hw + GDN_D * (h + 1)] for ci, h in units]
    v = [act[rsl(ci), 2 * hw + GDN_D * h:2 * hw + GDN_D * (h + 1)] for ci, h in units]
    q = [x * lax.rsqrt(jnp.sum(x * x, axis=-1, keepdims=True) + EPS) * (GDN_D ** -0.5) for x in q]
    k = [x * lax.rsqrt(jnp.sum(x * x, axis=-1, keepdims=True) + EPS) for x in k]
    kb = [k[u] * bcol[u] for u in un]
    eg = [jnp.exp2(gc[u]) for u in un]
    qa = [_dot_nt(jnp.concatenate([q[u], kb[u]], axis=0), k[u]) for u in un]
    qk = [qa[u][:CHUNK] * decay[u] for u in un]
    a_low = [jnp.where(strict, qa[u][CHUNK:] * decay[u], 0.0) for u in un]
    m = [jnp.where((blk_xor >> 2) == 0, -a_low[u], 0.0) for u in un]
    m2 = [_dot(m[u], m[u]) for u in un]
    n = [m[u] + m2[u] + _dot(m[u], m2[u]) for u in un]
    for lg in range(2, 6):
        low = [jnp.where((blk_xor >> lg) == 1, a_low[u], 0.0) for u in un]
        tl = [low[u] + _dot(n[u], low[u]) for u in un]
        n = [n[u] - (tl[u] + _dot(tl[u], n[u])) for u in un]
    rhs = [jnp.concatenate([v[u] * bcol[u], kb[u] * eg[u]], axis=-1) for u in un]
    sol = [rhs[u] + _dot(n[u], rhs[u]) for u in un]
    wq = [jnp.concatenate([sol[u][:, GDN_D:], q[u] * eg[u]], axis=0) for u in un]
    g_last = [g_cum[(ci + 1) * CHUNK - 1:(ci + 1) * CHUNK, GDN_HEADS + h:GDN_HEADS + h + 1] for ci, h in units]
    k_dec = [k[u] * jnp.exp2(g_last[u] - gc[u]) for u in un]
    s = [s_scr[h] for h in heads]
    for ci in range(nsub):
        us = [ci * GDN_HEADS + h for h in heads]
        ws = [_dot(wq[us[h]], s[h]) for h in heads]
        v_new = [sol[us[h]][:, :GDN_D] - ws[h][:CHUNK] for h in heads]
        o = [ws[h][CHUNK:] + _dot(qk[us[h]], v_new[h]) for h in heads]
        s = [s[h] * jnp.exp2(g_last[us[h]]) + _dot_tn(k_dec[us[h]], v_new[h]) for h in heads]
        for h in heads:
            zh = z_ref[rsl(ci), GDN_D * h:GDN_D * (h + 1)]
            o_ref[rsl(ci), GDN_D * h:GDN_D * (h + 1)] = _rms(o[h], ng_ref[...]) * _silu(zh)
    for h in heads:
        s_scr[h] = s[h]

    @pl.when(c == last)
    def _():
        sfin_ref[0] = s_scr[...]


def gdn_mixer(p0, bsz, t, conv0, s0, conv_w, a_log, dt_bias, norm_g):
    nsub = min(GDN_CHUNKS_PER_STEP, t // CHUNK)
    rows = nsub * CHUNK
    nc = t // rows
    n = bsz * t
    pvec = jnp.zeros((2, LANES), F32)
    pvec = pvec.at[0, GDN_HEADS:2 * GDN_HEADS].set(a_log).at[1, GDN_HEADS:2 * GDN_HEADS].set(dt_bias)
    vw = GDN_HEADS * GDN_D
    return pl.pallas_call(
        functools.partial(_gdn_body, nsub),
        grid=(bsz, nc),
        in_specs=[pl.BlockSpec((rows, GDN_QKV_W), lambda b, c: (b * nc + c, 0)),
                  pl.BlockSpec((rows, vw), lambda b, c: (b * nc + c, GDN_QKV_W // vw)),
                  pl.BlockSpec((rows, LANES), lambda b, c: (b * nc + c, BD_COL_BLOCK)),
                  pl.BlockSpec((1, CONV_W - 1, GDN_QKV_W), lambda b, c: (b, 0, 0)),
                  pl.BlockSpec((1, GDN_HEADS, GDN_D, GDN_D), lambda b, c: (b, 0, 0, 0)),
                  pl.BlockSpec((CONV_W, GDN_QKV_W), lambda b, c: (0, 0)),
                  pl.BlockSpec((2, LANES), lambda b, c: (0, 0)),
                  pl.BlockSpec((1, GDN_D), lambda b, c: (0, 0))],
        out_specs=[pl.BlockSpec((rows, vw), lambda b, c: (b * nc + c, 0)),
                   pl.BlockSpec((1, GDN_HEADS, GDN_D, GDN_D), lambda b, c: (b, 0, 0, 0)),
                   pl.BlockSpec((1, CONV_W - 1, GDN_QKV_W), lambda b, c: (b, 0, 0))],
        out_shape=[jax.ShapeDtypeStruct((n, vw), F32),
                   jax.ShapeDtypeStruct((bsz, GDN_HEADS, GDN_D, GDN_D), F32),
                   jax.ShapeDtypeStruct((bsz, CONV_W - 1, GDN_QKV_W), F32)],
        scratch_shapes=[pltpu.VMEM((GDN_HEADS, GDN_D, GDN_D), F32),
                        pltpu.VMEM((CONV_PAD + rows, GDN_QKV_W), F32)],
        compiler_params=_cparams(("arbitrary", "arbitrary")),
        name="gdn_mixer",
    )(p0, p0, p0, conv0, s0, conv_w, pvec, norm_g.reshape(1, GDN_D))


def _lru_body(tl, x_ref, gate_ref, conv0_ref, h0_ref, cw_ref, cb_ref, wa_ref, ba_ref, wx_ref, bx_ref,
              lam_ref, y_ref, hfin_ref, cfin_ref, h_scr, xbuf, abuf, bbuf):
    c = pl.program_id(1)
    last = pl.num_programs(1) - 1
    lo = CONV_PAD - (CONV_W - 1)
    pad = tl // 2

    @pl.when(c == 0)
    def _():
        h_scr[...] = h0_ref[0]
        xbuf[lo:CONV_PAD, :] = conv0_ref[0]
        abuf[0:pad, :] = jnp.ones((pad, LRU_W), F32)
        bbuf[0:pad, :] = jnp.zeros((pad, LRU_W), F32)

    xbuf[CONV_PAD:CONV_PAD + tl, :] = x_ref[...]
    xr = xbuf[lo:lo + tl, :] * cw_ref[0:1, :]
    for j in range(1, CONV_W):
        xr = xr + xbuf[lo + j:lo + j + tl, :] * cw_ref[j:j + 1, :]
    tail = xbuf[tl + lo:tl + CONV_PAD, :]
    xbuf[lo:CONV_PAD, :] = tail

    @pl.when(c == last)
    def _():
        cfin_ref[0] = tail

    xr = xr + cb_ref[...]
    ga = jnp.concatenate([_dot(xr[:, LRU_BLOCK_W * n:LRU_BLOCK_W * (n + 1)], wa_ref[n])
                          for n in range(LRU_BLOCKS)], axis=-1)
    gx = jnp.concatenate([_dot(xr[:, LRU_BLOCK_W * n:LRU_BLOCK_W * (n + 1)], wx_ref[n])
                          for n in range(LRU_BLOCKS)], axis=-1)
    gate_a = _sigmoid(ga + ba_ref[...])
    gate_x = _sigmoid(gx + bx_ref[...])
    a = jnp.exp2(gate_a * ((-LRU_C * LOG2_E) * _softplus(-lam_ref[...])))
    b = jnp.sqrt(1.0 - a * a) * gate_x * xr
    sub = lax.broadcasted_iota(jnp.int32, (tl, 1), 0) % SUBLANES
    d = 1
    while d < SUBLANES:
        abuf[pad:pad + tl, :] = a
        bbuf[pad:pad + tl, :] = b
        in_group = sub >= d
        a_sh = jnp.where(in_group, abuf[pad - d:pad - d + tl, :], 1.0)
        b_sh = jnp.where(in_group, bbuf[pad - d:pad - d + tl, :], 0.0)
        b = a * b_sh + b
        a = a * a_sh
        d *= 2
    carry = h_scr[...]
    pieces = []
    for r in range(0, tl, SUBLANES):
        h_grp = a[r:r + SUBLANES, :] * carry + b[r:r + SUBLANES, :]
        pieces.append(h_grp)
        carry = h_grp[SUBLANES - 1:SUBLANES, :]
    h = jnp.concatenate(pieces, axis=0)
    h_last = carry
    h_scr[...] = h_last
    y_ref[...] = h * _gelu(gate_ref[...])

    @pl.when(c == last)
    def _():
        hfin_ref[0] = h_last


def lru_mixer(p0, bsz, t, conv0, h0, conv_w, conv_b, w_a, b_a, w_x, b_x, lam):
    tl = min(t, LRU_ROWS)
    nc = t // tl
    n = bsz * t
    row = lambda v: v.reshape(1, LRU_W)
    return pl.pallas_call(
        functools.partial(_lru_body, tl),
        grid=(bsz, nc),
        in_specs=[pl.BlockSpec((tl, LRU_W), lambda b, c: (b * nc + c, IN0_LRU_IN_BLOCK)),
                  pl.BlockSpec((tl, LRU_W), lambda b, c: (b * nc + c, IN0_LRU_GATE_BLOCK)),
                  pl.BlockSpec((1, CONV_W - 1, LRU_W), lambda b, c: (b, 0, 0)),
                  pl.BlockSpec((1, 1, LRU_W), lambda b, c: (b, 0, 0)),
                  pl.BlockSpec((CONV_W, LRU_W), lambda b, c: (0, 0)),
                  pl.BlockSpec((1, LRU_W), lambda b, c: (0, 0)),
                  pl.BlockSpec((LRU_BLOCKS, LRU_BLOCK_W, LRU_BLOCK_W), lambda b, c: (0, 0, 0)),
                  pl.BlockSpec((1, LRU_W), lambda b, c: (0, 0)),
                  pl.BlockSpec((LRU_BLOCKS, LRU_BLOCK_W, LRU_BLOCK_W), lambda b, c: (0, 0, 0)),
                  pl.BlockSpec((1, LRU_W), lambda b, c: (0, 0)),
                  pl.BlockSpec((1, LRU_W), lambda b, c: (0, 0))],
        out_specs=[pl.BlockSpec((tl, LRU_W), lambda b, c: (b * nc + c, 0)),
                   pl.BlockSpec((1, 1, LRU_W), lambda b, c: (b, 0, 0)),
                   pl.BlockSpec((1, CONV_W - 1, LRU_W), lambda b, c: (b, 0, 0))],
        out_shape=[jax.ShapeDtypeStruct((n, LRU_W), F32),
                   jax.ShapeDtypeStruct((bsz, 1, LRU_W), F32),
                   jax.ShapeDtypeStruct((bsz, CONV_W - 1, LRU_W), F32)],
        scratch_shapes=[pltpu.VMEM((1, LRU_W), F32),
                        pltpu.VMEM((CONV_PAD + tl, LRU_W), F32),
                        pltpu.VMEM((tl // 2 + tl, LRU_W), F32),
                        pltpu.VMEM((tl // 2 + tl, LRU_W), F32)],
        compiler_params=_cparams(("arbitrary", "arbitrary")),
        name="lru_mixer",
    )(p0, p0, conv0, h0.reshape(bsz, 1, LRU_W), conv_w, row(conv_b), w_a, row(b_a), w_x, row(b_x), row(lam))


def _xattn_body(x_ref, mk_ref, mv_ref, wq_ref, wo_ref, g_in_ref, g_out_ref, o_ref):
    tm = x_ref.shape[0]
    sub = min(tm, XATTN_SUB_ROWS)
    parts = range(tm // sub)
    x = [x_ref[sub * j:sub * (j + 1), :] for j in parts]
    q = [jnp.dot(_rms(x[j], g_in_ref[...]).astype(BF16), wq_ref[...], preferred_element_type=F32) for j in parts]
    mk = mk_ref[0].astype(BF16)
    mv = mv_ref[0].astype(BF16)
    outs = [[] for _ in parts]
    for h in range(MEM_HEADS):
        sl = slice(MEM_HD * h, MEM_HD * (h + 1))
        s = [_dot_nt(q[j][:, sl], mk[:, sl]) * (MEM_HD ** -0.5) for j in parts]
        m = [jnp.max(s[j], axis=-1, keepdims=True) for j in parts]
        p = [jnp.exp(s[j] - m[j]) for j in parts]
        for j in parts:
            outs[j].append(_dot(p[j], mv[:, sl]) / jnp.sum(p[j], axis=-1, keepdims=True))
    y = [jnp.dot(jnp.concatenate(outs[j], axis=-1).astype(BF16), wo_ref[...], preferred_element_type=F32)
         for j in parts]
    for j in parts:
        o_ref[sub * j:sub * (j + 1), :] = x[j] + _rms(y[j], g_out_ref[...])


def cross_attention(x, bsz, t, mem_k, mem_v, wq, wo, g_in, g_out):
    tm = min(t, XATTN_TM)
    nt = t // tm
    n, d = x.shape
    return pl.pallas_call(
        _xattn_body,
        grid=(bsz, nt),
        in_specs=[pl.BlockSpec((tm, d), lambda b, i: (b * nt + i, 0)),
                  pl.BlockSpec((1, MEM_LEN, MEM_W), lambda b, i: (b, 0, 0)),
                  pl.BlockSpec((1, MEM_LEN, MEM_W), lambda b, i: (b, 0, 0)),
                  pl.BlockSpec((d, MEM_W), lambda b, i: (0, 0)),
                  pl.BlockSpec((MEM_W, d), lambda b, i: (0, 0)),
                  pl.BlockSpec((1, d), lambda b, i: (0, 0)),
                  pl.BlockSpec((1, d), lambda b, i: (0, 0))],
        out_specs=pl.BlockSpec((tm, d), lambda b, i: (b * nt + i, 0)),
        out_shape=jax.ShapeDtypeStruct((n, d), F32),
        compiler_params=_cparams(("arbitrary", "arbitrary")),
        name="cross_attention",
    )(x, mem_k, mem_v, wq, wo, g_in.reshape(1, d), g_out.reshape(1, d))


def _ffn_body(x_ref, g_in_ref, wg_ref, wu_ref, wd_ref, g_out_ref, o_ref, xn_ref, acc_ref):
    f = pl.program_id(1)

    @pl.when(f == 0)
    def _():
        xn_ref[...] = _rms(x_ref[...], g_in_ref[...]).astype(BF16)
        acc_ref[...] = jnp.zeros_like(acc_ref)

    xn = xn_ref[...]
    gate = jnp.dot(xn, wg_ref[...], preferred_element_type=F32)
    up = jnp.dot(xn, wu_ref[...], preferred_element_type=F32)
    acc_ref[...] += jnp.dot((_silu(gate) * up).astype(BF16), wd_ref[...], preferred_element_type=F32)

    @pl.when(f == pl.num_programs(1) - 1)
    def _():
        o_ref[...] = x_ref[...] + _rms(acc_ref[...], g_out_ref[...])


def dense_ffn(x, g_in, wg, wu, wd, g_out, tm=FFN_TM, tf=FFN_TF):
    n, d = x.shape
    ff = wg.shape[1]
    tm = min(tm, n)
    return pl.pallas_call(
        _ffn_body,
        grid=(n // tm, ff // tf),
        in_specs=[pl.BlockSpec((tm, d), lambda i, f: (i, 0)),
                  pl.BlockSpec((1, d), lambda i, f: (0, 0)),
                  pl.BlockSpec((d, tf), lambda i, f: (0, f)),
                  pl.BlockSpec((d, tf), lambda i, f: (0, f)),
                  pl.BlockSpec((tf, d), lambda i, f: (f, 0)),
                  pl.BlockSpec((1, d), lambda i, f: (0, 0))],
        out_specs=pl.BlockSpec((tm, d), lambda i, f: (i, 0)),
        out_shape=jax.ShapeDtypeStruct((n, d), F32),
        scratch_shapes=[pltpu.VMEM((tm, d), BF16), pltpu.VMEM((tm, d), F32)],
        compiler_params=_cparams(("arbitrary", "arbitrary")),
        name="dense_ffn",
    )(x, g_in.reshape(1, d), wg, wu, wd, g_out.reshape(1, d))


def _swa_body(start, nsub, q_ref, kv_ref, cos_ref, sin_ref, kprev_ref, vprev_ref, sink_ref,
              o_ref, krot_ref, kbuf, vbuf):
    c = pl.program_id(1)
    rows_t = nsub * CHUNK

    @pl.when(c == 0)
    def _():
        kbuf[0:WINDOW, :] = kprev_ref[0]
        vbuf[0:WINDOW, :] = vprev_ref[0]

    cos = cos_ref[...]
    sin = sin_ref[...]
    lane = lax.broadcasted_iota(jnp.int32, (rows_t, LANES), 1)
    first_half = (lane % SWA_HD) < (SWA_HD // 2)

    def rope(x):
        outs = []
        for j in range(x.shape[1] // LANES):
            xb = x[:, LANES * j:LANES * (j + 1)]
            fwd = pltpu.roll(xb, LANES - SWA_HD // 2, 1)
            bwd = pltpu.roll(xb, SWA_HD // 2, 1)
            outs.append(xb * cos + jnp.where(first_half, fwd, bwd) * sin)
        return jnp.concatenate(outs, axis=-1)

    q = rope(q_ref[...])
    kv = kv_ref[...]
    k = rope(kv[:, :SWA_KV_W])
    krot_ref[...] = k
    kbuf[WINDOW:WINDOW + rows_t, :] = k
    vbuf[WINDOW:WINDOW + rows_t, :] = kv[:, SWA_KV_W:]

    nk = WINDOW + CHUNK
    rows = SWA_GROUP * CHUNK
    key_off = lax.broadcasted_iota(jnp.int32, (rows, nk), 1) - WINDOW
    row_head = lax.broadcasted_iota(jnp.int32, (rows, 1), 0) // CHUNK
    kvh = range(SWA_KV_HEADS)
    units = [(ci, hk) for ci in range(nsub) for hk in kvh]
    un = range(len(units))
    valid = [start + (c * nsub + ci) * CHUNK + key_off >= 0 for ci in range(nsub)]
    qg = [jnp.concatenate([q[ci * CHUNK:(ci + 1) * CHUNK,
                             SWA_HD * (hk * SWA_GROUP + gi):SWA_HD * (hk * SWA_GROUP + gi + 1)]
                           for gi in range(SWA_GROUP)], axis=0) for ci, hk in units]
    kh = [kbuf[ci * CHUNK:ci * CHUNK + nk, SWA_HD * hk:SWA_HD * (hk + 1)] for ci, hk in units]
    vh = [vbuf[ci * CHUNK:ci * CHUNK + nk, SWA_HD * hk:SWA_HD * (hk + 1)] for ci, hk in units]
    sink_h = []
    for hk in kvh:
        col = jnp.full((rows, 1), sink_ref[hk * SWA_GROUP], F32)
        for gi in range(1, SWA_GROUP):
            col = jnp.where(row_head == gi, sink_ref[hk * SWA_GROUP + gi], col)
        sink_h.append(col)
    sink = [sink_h[hk] for _, hk in units]
    s = [jnp.where(valid[units[u][0]], _dot_nt(qg[u], kh[u]) * (SWA_HD ** -0.5), -jnp.inf) for u in un]
    m = [jnp.maximum(jnp.max(s[u], axis=-1, keepdims=True), sink[u]) for u in un]
    p = [jnp.exp(s[u] - m[u]) for u in un]
    denom = [jnp.sum(p[u], axis=-1, keepdims=True) + jnp.exp(sink[u] - m[u]) for u in un]
    og = [_dot(p[u], vh[u]) / denom[u] for u in un]
    for u, (ci, hk) in enumerate(units):
        for pair in range(SWA_GROUP // 2):
            lo_rows = og[u][CHUNK * 2 * pair:CHUNK * (2 * pair + 1)]
            hi_rows = og[u][CHUNK * (2 * pair + 1):CHUNK * (2 * pair + 2)]
            lane0 = SWA_HD * (hk * SWA_GROUP + 2 * pair)
            o_ref[ci * CHUNK:(ci + 1) * CHUNK, lane0:lane0 + 2 * SWA_HD] = jnp.concatenate(
                [lo_rows, hi_rows], axis=-1)

    k_keep = kbuf[rows_t:rows_t + WINDOW, :]
    v_keep = vbuf[rows_t:rows_t + WINDOW, :]
    kbuf[0:WINDOW, :] = k_keep
    vbuf[0:WINDOW, :] = v_keep


def _rope_tables(start, t):
    half = SWA_HD // 2
    inv_freq = jnp.exp(-math.log(ROPE_THETA) * jnp.arange(half, dtype=F32) / half)
    ang = (start + jnp.arange(t)).astype(F32)[:, None] * inv_freq[None, :]
    cos = jnp.cos(ang)
    sin = jnp.sin(ang)
    return jnp.tile(cos, (1, LANES // half)), jnp.tile(jnp.concatenate([-sin, sin], axis=-1), (1, LANES // SWA_HD))


def swa_mixer(p1, bsz, t, start, k_prev, v_prev, sinks):
    nsub = min(SWA_CHUNKS_PER_STEP, t // CHUNK)
    rows = nsub * CHUNK
    nc = t // rows
    n = bsz * t
    qw = SWA_Q_HEADS * SWA_HD
    cos, sin = _rope_tables(start, t)
    return pl.pallas_call(
        functools.partial(_swa_body, start, nsub),
        grid=(bsz, nc),
        in_specs=[pl.BlockSpec((rows, qw), lambda b, c: (b * nc + c, 0)),
                  pl.BlockSpec((rows, 2 * SWA_KV_W), lambda b, c: (b * nc + c, 3 * qw // (2 * SWA_KV_W))),
                  pl.BlockSpec((rows, LANES), lambda b, c: (c, 0)),
                  pl.BlockSpec((rows, LANES), lambda b, c: (c, 0)),
                  pl.BlockSpec((1, WINDOW, SWA_KV_W), lambda b, c: (b, 0, 0)),
                  pl.BlockSpec((1, WINDOW, SWA_KV_W), lambda b, c: (b, 0, 0)),
                  pl.BlockSpec(memory_space=pltpu.SMEM)],
        out_specs=[pl.BlockSpec((rows, qw), lambda b, c: (b * nc + c, 0)),
                   pl.BlockSpec((rows, SWA_KV_W), lambda b, c: (b * nc + c, 0))],
        out_shape=[jax.ShapeDtypeStruct((n, qw), F32),
                   jax.ShapeDtypeStruct((n, SWA_KV_W), F32)],
        scratch_shapes=[pltpu.VMEM((WINDOW + rows, SWA_KV_W), F32),
                        pltpu.VMEM((WINDOW + rows, SWA_KV_W), F32)],
        compiler_params=_cparams(("arbitrary", "arbitrary")),
        name="swa_mixer",
    )(p1, p1, cos, sin, k_prev, v_prev, sinks)


def _smlp_body(lc, u_ref, v_ref, lg_ref, lb_ref, ws_ref, bs_ref, y_ref, vn_ref):
    v = _gelu(v_ref[...])
    mu = jnp.mean(v, axis=-1, keepdims=True)
    vc = v - mu
    vn = vc * lax.rsqrt(jnp.mean(vc * vc, axis=-1, keepdims=True) + EPS) * lg_ref[...] + lb_ref[...]
    vn_ref[...] = vn
    u = _gelu(u_ref[...])
    row = lax.broadcasted_iota(jnp.int32, (lc, lc), 0)
    col = lax.broadcasted_iota(jnp.int32, (lc, lc), 1)
    for g in range(SMLP_GROUPS):
        sl = slice(SMLP_GROUP_W * g, SMLP_GROUP_W * (g + 1))
        w = jnp.where(row >= col, ws_ref[g, 0:lc, 0:lc], 0.0)
        s = _dot(w, vn[:, sl]) + bs_ref[0:lc, g:g + 1]
        y_ref[:, sl] = u[:, sl] * s


def smlp_mixer(p1, bsz, t, ln_g, ln_b, w_spatial, b_spatial):
    lc = min(SMLP_CHUNK, t)
    n = bsz * t
    row = lambda v: v.reshape(1, SMLP_W)
    return pl.pallas_call(
        functools.partial(_smlp_body, lc),
        grid=(n // lc,),
        in_specs=[pl.BlockSpec((lc, SMLP_W), lambda i: (i, IN1_U_BLOCK)),
                  pl.BlockSpec((lc, SMLP_W), lambda i: (i, IN1_VG_BLOCK)),
                  pl.BlockSpec((1, SMLP_W), lambda i: (0, 0)),
                  pl.BlockSpec((1, SMLP_W), lambda i: (0, 0)),
                  pl.BlockSpec((SMLP_GROUPS, SMLP_CHUNK, SMLP_CHUNK), lambda i: (0, 0, 0)),
                  pl.BlockSpec((SMLP_CHUNK, SMLP_GROUPS), lambda i: (0, 0))],
        out_specs=[pl.BlockSpec((lc, SMLP_W), lambda i: (i, 0)),
                   pl.BlockSpec((lc, SMLP_W), lambda i: (i, 0))],
        out_shape=[jax.ShapeDtypeStruct((n, SMLP_W), F32),
                   jax.ShapeDtypeStruct((n, SMLP_W), F32)],
        compiler_params=_cparams(("arbitrary",)),
        name="smlp_mixer",
    )(p1, p1, row(ln_g), row(ln_b), w_spatial, b_spatial.T)


def _router_body(x_ref, g_ref, wr_ref, idx_ref, gate_ref):
    hn = _rms(x_ref[...], g_ref[...])
    wr = wr_ref[...]
    h1 = hn.astype(BF16)
    h2 = (hn - h1.astype(F32)).astype(BF16)
    w1 = wr.astype(BF16)
    w2 = (wr - w1.astype(F32)).astype(BF16)
    nt = (((1,), (1,)), ((), ()))
    lg = lax.dot_general(jnp.concatenate([w1, w2], axis=0), h1, nt, preferred_element_type=F32)
    logits = (lg[:N_EXPERTS] + lg[N_EXPERTS:]
              + lax.dot_general(w1, h2, nt, preferred_element_type=F32))
    e_iota = lax.broadcasted_iota(jnp.int32, logits.shape, 0)
    m1 = jnp.max(logits, axis=0, keepdims=True)
    i1 = jnp.min(jnp.where(logits == m1, e_iota, N_EXPERTS), axis=0, keepdims=True)
    rest = jnp.where(e_iota == i1, -jnp.inf, logits)
    m2 = jnp.max(rest, axis=0, keepdims=True)
    i2 = jnp.min(jnp.where(rest == m2, e_iota, N_EXPERTS), axis=0, keepdims=True)
    e2 = jnp.exp(m2 - m1)
    den = 1.0 + e2
    idx_ref[...] = jnp.concatenate([i1, i2], axis=0)
    tm = logits.shape[1]
    gates = jnp.concatenate([1.0 / den, e2 / den, jnp.zeros((LANES - 2, tm), F32)], axis=0)
    gate_ref[...] = gates.T


def moe_router(x, g, w_router, tm=ROUTER_TM):
    n, d = x.shape
    tm = min(tm, n)
    return pl.pallas_call(
        _router_body,
        grid=(n // tm,),
        in_specs=[pl.BlockSpec((tm, d), lambda i: (i, 0)),
                  pl.BlockSpec((1, d), lambda i: (0, 0)),
                  pl.BlockSpec((N_EXPERTS, d), lambda i: (0, 0))],
        out_specs=[pl.BlockSpec((2, tm), lambda i: (0, i)),
                   pl.BlockSpec((tm, LANES), lambda i: (i, 0))],
        out_shape=[jax.ShapeDtypeStruct((2, n), jnp.int32),
                   jax.ShapeDtypeStruct((n, LANES), F32)],
        compiler_params=_cparams(("arbitrary",)),
        name="moe_router",
    )(x, g.reshape(1, d), w_router.T)


def _moe_block_rows(n_tokens):
    return MOE_BM if 2 * n_tokens // N_EXPERTS >= 4 * MOE_BM else MOE_BM // 2


def _moe_plan(top_idx, bm):
    n = top_idx.shape[1]
    flat_e = top_idx.reshape(-1)
    onehot = (flat_e[:, None] == jnp.arange(N_EXPERTS, dtype=jnp.int32)[None, :]).astype(jnp.int32)
    rank = jnp.sum(jnp.cumsum(onehot, axis=0) * onehot, axis=1) - 1
    counts = jnp.sum(onehot, axis=0)
    padded = (counts + bm - 1) // bm * bm
    pad_end = jnp.cumsum(padded)
    pad_start = pad_end - padded
    dest = jnp.sum(onehot * pad_start[None, :], axis=1) + rank
    n_blk = -(-2 * n // bm) + N_EXPERTS
    blk_start = jnp.arange(n_blk, dtype=jnp.int32) * bm
    blk_e = jnp.minimum(jnp.sum((blk_start[:, None] >= pad_end[None, :]).astype(jnp.int32), axis=1),
                        N_EXPERTS - 1)
    blk_valid = jnp.clip((pad_start + counts)[blk_e] - blk_start, 0, bm).astype(jnp.int32)
    n_active = (pad_end[-1] // bm).astype(jnp.int32).reshape(1)
    return dest.reshape(2, n).astype(jnp.int32), blk_e.astype(jnp.int32), blk_valid, n_active, n_blk


def _moe_dispatch_body(tc, dest_ref, x_ref, g_ref, xs_hbm, hn_scr, sem):
    d = x_ref.shape[1]
    hn_scr[...] = _rms(x_ref[...], g_ref[...]).reshape(tc // SUBLANES, SUBLANES, d)

    def start(i8, carry):
        for j in range(SUBLANES):
            for slot in range(2):
                dst_row = dest_ref[0, slot, i8 * SUBLANES + j]
                pltpu.make_async_copy(hn_scr.at[i8, pl.ds(j, 1), :], xs_hbm.at[pl.ds(dst_row, 1), :],
                                      sem).start(priority=slot)
        return carry

    lax.fori_loop(0, tc // SUBLANES, start, 0)
    for _ in range(2):
        pltpu.make_async_copy(x_ref, xs_hbm.at[pl.ds(0, tc), :], sem).wait()


def _moe_dispatch_into_body(tc, dest_ref, x_ref, g_ref, xs_prev_hbm, xs_hbm, hn_scr, sem):
    del xs_prev_hbm
    _moe_dispatch_body(tc, dest_ref, x_ref, g_ref, xs_hbm, hn_scr, sem)


def moe_dispatch(x, g, dest_blocks, n_rows, tc, xs_prev=None):
    n, d = x.shape
    in_specs = [pl.BlockSpec((1, 2, tc), lambda i: (i, 0, 0), memory_space=pltpu.SMEM),
                pl.BlockSpec((tc, d), lambda i: (i, 0)),
                pl.BlockSpec((1, d), lambda i: (0, 0))]
    args = [dest_blocks, x, g.reshape(1, d)]
    body, aliases = _moe_dispatch_body, {}
    if xs_prev is not None:
        in_specs.append(pl.BlockSpec(memory_space=pl.ANY))
        args.append(xs_prev)
        body, aliases = _moe_dispatch_into_body, {3: 0}
    return pl.pallas_call(
        functools.partial(body, tc),
        grid=(n // tc,),
        in_specs=in_specs,
        out_specs=pl.BlockSpec(memory_space=pl.ANY),
        out_shape=jax.ShapeDtypeStruct((n_rows, d), F32),
        scratch_shapes=[pltpu.VMEM((tc // SUBLANES, SUBLANES, d), F32), pltpu.SemaphoreType.DMA(())],
        input_output_aliases=aliases,
        compiler_params=_cparams(("arbitrary",)),
        name="moe_dispatch",
    )(*args)


def _moe_ffn_body(blk_e_ref, blk_valid_ref, nact_ref, xs_ref, wg_hbm, wu_hbm, wd_hbm, ys_ref,
                  wg_scr, wu_scr, wd_scr, sem):
    i = pl.program_id(0)
    active = i < nact_ref[0]
    e = blk_e_ref[i]
    new_expert = jnp.logical_or(i == 0, e != blk_e_ref[jnp.maximum(i - 1, 0)])

    @pl.when(jnp.logical_and(active, new_expert))
    def _():
        copies = [pltpu.make_async_copy(w_hbm.at[e], w_scr, sem.at[j])
                  for j, (w_hbm, w_scr) in enumerate(((wg_hbm, wg_scr), (wu_hbm, wu_scr), (wd_hbm, wd_scr)))]
        for cp in copies:
            cp.start()
        for cp in copies:
            cp.wait()

    @pl.when(active)
    def _():
        row = lax.broadcasted_iota(jnp.int32, (xs_ref.shape[0], 1), 0)
        xb = jnp.where(row < blk_valid_ref[i], xs_ref[...], 0.0).astype(BF16)
        acc = None
        for j in range(FF_EXPERT // MOE_TF):
            sl = slice(MOE_TF * j, MOE_TF * (j + 1))
            gate = jnp.dot(xb, wg_scr[:, sl], preferred_element_type=F32)
            up = jnp.dot(xb, wu_scr[:, sl], preferred_element_type=F32)
            part = jnp.dot((_silu(gate) * up).astype(BF16), wd_scr[sl, :], preferred_element_type=F32)
            acc = part if acc is None else acc + part
        ys_ref[...] = acc

    @pl.when(jnp.logical_not(active))
    def _():
        ys_ref[...] = jnp.zeros_like(ys_ref)


def moe_expert_ffn(xs, blk_e, blk_valid, n_active, wg, wu, wd, bm):
    n_rows, d = xs.shape
    n_blk = n_rows // bm
    return pl.pallas_call(
        _moe_ffn_body,
        grid_spec=pltpu.PrefetchScalarGridSpec(
            num_scalar_prefetch=3,
            grid=(n_blk,),
            in_specs=[pl.BlockSpec((bm, d), lambda i, be, bv, na: (jnp.minimum(i, na[0] - 1), 0)),
                      pl.BlockSpec(memory_space=pl.ANY),
                      pl.BlockSpec(memory_space=pl.ANY),
                      pl.BlockSpec(memory_space=pl.ANY)],
            out_specs=pl.BlockSpec((bm, d), lambda i, be, bv, na: (i, 0)),
            scratch_shapes=[pltpu.VMEM((d, FF_EXPERT), BF16), pltpu.VMEM((d, FF_EXPERT), BF16),
                            pltpu.VMEM((FF_EXPERT, d), BF16), pltpu.SemaphoreType.DMA((3,))]),
        out_shape=jax.ShapeDtypeStruct((n_rows, d), F32),
        compiler_params=_cparams(("arbitrary",)),
        name="moe_expert_ffn",
    )(blk_e, blk_valid, n_active, xs, wg, wu, wd)


def _moe_combine_body(tc, pos_ref, pos_next_ref, ys_hbm, gate_ref, x_ref, g_ref, o_ref, buf, sem):
    i = pl.program_id(0)
    nb = pl.num_programs(0)
    slot = i % 2

    def issue(p_ref, s):
        def start(i8, carry):
            for j in range(SUBLANES):
                for choice in range(2):
                    src_row = p_ref[0, choice, i8 * SUBLANES + j]
                    pltpu.make_async_copy(ys_hbm.at[pl.ds(src_row, 1), :], buf.at[s, choice, i8, pl.ds(j, 1), :],
                                          sem.at[s]).start(priority=choice)
            return carry

        lax.fori_loop(0, tc // SUBLANES, start, 0)

    @pl.when(i == 0)
    def _():
        issue(pos_ref, 0)

    @pl.when(i + 1 < nb)
    def _():
        issue(pos_next_ref, 1 - slot)

    d = x_ref.shape[1]
    for choice in range(2):
        pltpu.make_async_copy(ys_hbm.at[pl.ds(0, tc), :], o_ref, sem.at[slot]).wait()
    gates = gate_ref[...]
    y = (gates[:, 0:1] * buf[slot, 0].reshape(tc, d) + gates[:, 1:2] * buf[slot, 1].reshape(tc, d))
    o_ref[...] = x_ref[...] + _rms(y, g_ref[...])


def moe_combine(ys, pos_blocks, gates, x, g, tc):
    n, d = x.shape
    nb = n // tc
    return pl.pallas_call(
        functools.partial(_moe_combine_body, tc),
        grid=(nb,),
        in_specs=[pl.BlockSpec((1, 2, tc), lambda i: (i, 0, 0), memory_space=pltpu.SMEM),
                  pl.BlockSpec((1, 2, tc), lambda i: (jnp.minimum(i + 1, nb - 1), 0, 0), memory_space=pltpu.SMEM),
                  pl.BlockSpec(memory_space=pl.ANY),
                  pl.BlockSpec((tc, LANES), lambda i: (i, 0)),
                  pl.BlockSpec((tc, d), lambda i: (i, 0)),
                  pl.BlockSpec((1, d), lambda i: (0, 0))],
        out_specs=pl.BlockSpec((tc, d), lambda i: (i, 0)),
        out_shape=jax.ShapeDtypeStruct((n, d), F32),
        scratch_shapes=[pltpu.VMEM((2, 2, tc // SUBLANES, SUBLANES, d), F32), pltpu.SemaphoreType.DMA((2,))],
        compiler_params=_cparams(("arbitrary",)),
        name="moe_combine",
    )(pos_blocks, pos_blocks, ys, gates, x, g.reshape(1, d))


def moe_block(xs_in, g_in, w_router, wg, wu, wd, g_out, tc=MOE_TC):
    routed = [moe_router(x, g_in, w_router) for x in xs_in]
    sizes = [x.shape[0] for x in xs_in]
    n_all = sum(sizes)
    bm = _moe_block_rows(n_all)
    dest, blk_e, blk_valid, n_active, n_blk = _moe_plan(jnp.concatenate([r[0] for r in routed], axis=1), bm)
    xs, dest_blocks, off = None, [], 0
    for x, n in zip(xs_in, sizes):
        t = min(tc, n)
        dest_blocks.append(dest[:, off:off + n].reshape(2, n // t, t).transpose(1, 0, 2))
        xs = moe_dispatch(x, g_in, dest_blocks[-1], n_blk * bm, t, xs_prev=xs)
        off += n
    ys = moe_expert_ffn(xs, blk_e, blk_valid, n_active, wg, wu, wd, bm)
    return [moe_combine(ys, db, r[1], x, g_out, min(tc, x.shape[0]))
            for x, db, r in zip(xs_in, dest_blocks, routed)]


def _forward(x, start, keep, mem_k, mem_v, gdn_conv0, gdn_s0, lru_conv0, lru_h0, swa_k0, swa_v0, p):
    bsz, t, d = x.shape
    x = x.reshape(bsz * t, d)
    ng = p['norm_g']
    p0 = norm_matmul(x, ng[0, 0], p['w_in0'], IN_PROJ_TM, IN0_TN)
    o_gdn, gdn_s, gdn_conv = gdn_mixer(p0, bsz, t, gdn_conv0, gdn_s0, p['gdn_conv_w'], p['gdn_a_log'],
                                       p['gdn_dt_bias'], p['gdn_norm_g'])
    y_lru, lru_h, lru_conv = lru_mixer(p0, bsz, t, lru_conv0, lru_h0, p['lru_conv_w'], p['lru_conv_b'],
                                       p['lru_w_a'], p['lru_b_a'], p['lru_w_x'], p['lru_b_x'], p['lru_lambda'])
    x = outproj_norm_resid(o_gdn, y_lru, p['w_out0'], x, ng[0, 1])
    x = cross_attention(x, bsz, t, mem_k[0], mem_v[0], p['w_xq'][0], p['w_xo'][0], ng[0, 2], ng[0, 3])
    x = dense_ffn(x, ng[0, 4], p['w_ff_gate'], p['w_ff_up'], p['w_ff_down'], ng[0, 5])
    p1 = norm_matmul(x, ng[1, 0], p['w_in1'], IN_PROJ_TM, IN1_TN)
    attn, k_rot = swa_mixer(p1, bsz, t, start, swa_k0, swa_v0, p['swa_sinks'])
    y_smlp, smlp_v = smlp_mixer(p1, bsz, t, p['smlp_ln_g'], p['smlp_ln_b'], p['w_spatial'], p['b_spatial'])
    x = outproj_norm_resid(attn, y_smlp, p['w_out1'], x, ng[1, 1])
    x = cross_attention(x, bsz, t, mem_k[1], mem_v[1], p['w_xq'][1], p['w_xo'][1], ng[1, 2], ng[1, 3])
    k_rows = k_rot.reshape(bsz, t, SWA_KV_W)[:, t - keep:].reshape(bsz, keep, SWA_KV_HEADS, SWA_HD)
    v_rows = p1.reshape(bsz, t, IN1_W)[:, t - keep:, IN1_W - SWA_KV_W:].reshape(bsz, keep, SWA_KV_HEADS, SWA_HD)
    return (x, gdn_conv, gdn_s, lru_conv, lru_h.reshape(bsz, LRU_W), k_rows, v_rows, smlp_v.reshape(bsz, t, SMLP_W))


def _prepare_weights(norm_g, w_in0, gdn_conv_w, gdn_a_log, gdn_dt_bias, gdn_norm_g, lru_conv_w, lru_conv_b,
                     lru_w_a, lru_b_a, lru_w_x, lru_b_x, lru_lambda, w_out0, w_in1, swa_sinks, smlp_ln_g,
                     smlp_ln_b, w_spatial, b_spatial, w_out1, w_xq, w_xo, w_ff_gate, w_ff_up, w_ff_down,
                     w_router, w_moe_gate, w_moe_up, w_moe_down):
    qkvz_w = GDN_QKV_W + GDN_HEADS * GDN_D
    bd_w = 2 * GDN_HEADS
    w0 = jnp.concatenate([w_in0[:, :qkvz_w], w_in0[:, qkvz_w + bd_w:], w_in0[:, qkvz_w:qkvz_w + bd_w],
                          jnp.zeros((D_MODEL, IN0_PAD_W - w_in0.shape[1]), w_in0.dtype)], axis=1)
    qw = SWA_Q_HEADS * SWA_HD
    w1 = jnp.concatenate([w_in1[:, :qw], w_in1[:, qw + 2 * SWA_KV_W:], w_in1[:, qw:qw + 2 * SWA_KV_W]], axis=1)
    return dict(
        norm_g=norm_g, w_in0=w0.astype(BF16), gdn_conv_w=gdn_conv_w, gdn_a_log=gdn_a_log,
        gdn_dt_bias=gdn_dt_bias, gdn_norm_g=gdn_norm_g, lru_conv_w=lru_conv_w, lru_conv_b=lru_conv_b,
        lru_w_a=lru_w_a, lru_b_a=lru_b_a, lru_w_x=lru_w_x, lru_b_x=lru_b_x, lru_lambda=lru_lambda,
        w_out0=w_out0.astype(BF16), w_in1=w1.astype(BF16), swa_sinks=swa_sinks, smlp_ln_g=smlp_ln_g,
        smlp_ln_b=smlp_ln_b, w_spatial=w_spatial, b_spatial=b_spatial, w_out1=w_out1.astype(BF16),
        w_xq=w_xq.astype(BF16), w_xo=w_xo.astype(BF16), w_ff_gate=w_ff_gate.astype(BF16),
        w_ff_up=w_ff_up.astype(BF16), w_ff_down=w_ff_down.astype(BF16), w_router=w_router,
        w_moe_gate=w_moe_gate.astype(BF16), w_moe_up=w_moe_up.astype(BF16), w_moe_down=w_moe_down.astype(BF16))


def kernel(x_prompt, x_sample, mem_prompt, cache_mem_k, cache_mem_v, state_gdn, state_gdn_conv, state_rglru_h, state_rglru_conv, cache_swa_k, cache_swa_v, norm_g, mem_norm_g, w_in0, gdn_conv_w, gdn_a_log, gdn_dt_bias, gdn_norm_g, lru_conv_w, lru_conv_b, lru_w_a, lru_b_a, lru_w_x, lru_b_x, lru_lambda, w_out0, w_in1, swa_sinks, smlp_ln_g, smlp_ln_b, w_spatial, b_spatial, w_out1, w_xq, w_xk, w_xv, w_xo, w_ff_gate, w_ff_up, w_ff_down, w_router, w_moe_gate, w_moe_up, w_moe_down):
    p = _prepare_weights(norm_g, w_in0, gdn_conv_w, gdn_a_log, gdn_dt_bias, gdn_norm_g, lru_conv_w, lru_conv_b,
                         lru_w_a, lru_b_a, lru_w_x, lru_b_x, lru_lambda, w_out0, w_in1, swa_sinks, smlp_ln_g,
                         smlp_ln_b, w_spatial, b_spatial, w_out1, w_xq, w_xo, w_ff_gate, w_ff_up, w_ff_down,
                         w_router, w_moe_gate, w_moe_up, w_moe_down)
    bsz, t, d = x_prompt.shape
    depth = w_xk.shape[0]
    mem_flat = mem_prompt.reshape(bsz * MEM_LEN, d)
    mem_k_p = jnp.stack([norm_matmul(mem_flat, mem_norm_g[l], w_xk[l].astype(BF16), MEM_PROJ_TM, MEM_W)
                         for l in range(depth)]).reshape(depth, bsz, MEM_LEN, MEM_W)
    mem_v_p = jnp.stack([norm_matmul(mem_flat, mem_norm_g[l], w_xv[l].astype(BF16), MEM_PROJ_TM, MEM_W)
                         for l in range(depth)]).reshape(depth, bsz, MEM_LEN, MEM_W)
    keep = min(WINDOW, t)
    (x_p, gdn_conv_p, gdn_s_p, lru_conv_p, lru_h_p, k_rows_p, v_rows_p, _) = _forward(
        x_prompt, 0, keep, mem_k_p, mem_v_p,
        jnp.zeros((bsz, CONV_W - 1, GDN_QKV_W), F32), jnp.zeros((bsz, GDN_HEADS, GDN_D, GDN_D), F32),
        jnp.zeros((bsz, CONV_W - 1, LRU_W), F32), jnp.zeros((bsz, LRU_W), F32),
        jnp.zeros((bsz, WINDOW, SWA_KV_W), F32), jnp.zeros((bsz, WINDOW, SWA_KV_W), F32), p)
    dbs, dec_t = x_sample.shape[:2]
    n_prev = cache_swa_k.shape[1]
    assert n_prev == WINDOW
    (x_s, gdn_conv_s, gdn_s_s, lru_conv_s, lru_h_s, k_rows_s, v_rows_s, smlp_v_s) = _forward(
        x_sample, PAST_LEN, dec_t, cache_mem_k.reshape(depth, dbs, MEM_LEN, MEM_W),
        cache_mem_v.reshape(depth, dbs, MEM_LEN, MEM_W), state_gdn_conv, state_gdn, state_rglru_conv,
        state_rglru_h, cache_swa_k.reshape(dbs, n_prev, SWA_KV_W), cache_swa_v.reshape(dbs, n_prev, SWA_KV_W), p)
    y_p, y_s = moe_block([x_p, x_s], norm_g[1, 4], p['w_router'], p['w_moe_gate'], p['w_moe_up'],
                         p['w_moe_down'], norm_g[1, 5])
    y_p = y_p.reshape(bsz, t, d)
    y_s = y_s.reshape(dbs, dec_t, d)
    shape5 = (depth, bsz, MEM_LEN, MEM_HEADS, MEM_HD)
    return (y_p, y_s, mem_k_p.reshape(shape5), mem_v_p.reshape(shape5), gdn_s_p, gdn_conv_p, lru_h_p, lru_conv_p,
            k_rows_p, v_rows_p, gdn_s_s, gdn_conv_s, lru_h_s, lru_conv_s,
            k_rows_s, v_rows_s, smlp_v_s)
```

```python
import functools
import math

import jax
import jax.numpy as jnp
from jax import lax
from jax.experimental import pallas as pl
from jax.experimental.pallas import tpu as pltpu

F32 = jnp.float32
BF16 = jnp.bfloat16
LOG2_E = math.log2(math.e)

D_MODEL = 2048
EPS = 1e-6
CHUNK = 64
CONV_W = 4
CONV_PAD = 8
GDN_HEADS = 8
GDN_D = 128
GDN_QKV_W = 3 * GDN_HEADS * GDN_D
GDN_CHUNKS_PER_STEP = 2
LRU_W = 1024
LRU_BLOCKS = 8
LRU_BLOCK_W = LRU_W // LRU_BLOCKS
LRU_C = 8.0
IN0_PAD_W = 6400
BD_COL_BLOCK = 6144 // 128
SWA_Q_HEADS = 16
SWA_KV_HEADS = 4
SWA_GROUP = SWA_Q_HEADS // SWA_KV_HEADS
SWA_HD = 64
SWA_KV_W = SWA_KV_HEADS * SWA_HD
WINDOW = 128
SWA_CHUNKS_PER_STEP = 4
ROPE_THETA = 10000.0
PAST_LEN = 4096
SMLP_GROUPS = 8
SMLP_GROUP_W = 128
SMLP_W = SMLP_GROUPS * SMLP_GROUP_W
SMLP_CHUNK = 128
SMLP_CHUNKS_PER_STEP = 4
IN1_W = 3584
MEM_LEN = 256
MEM_HEADS = 4
MEM_HD = 128
MEM_W = MEM_HEADS * MEM_HD
XATTN_SUB_ROWS = 512
FF_DENSE = 5632
N_EXPERTS = 8
FF_EXPERT = 2816
MOE_BM = 512
MOE_TF = FF_EXPERT // 11
LANES = 128
SUBLANES = 8

VMEM_LIMIT_MB = 56

IN_PROJ_TM = 1024
IN0_TN = 1280
IN1_TN = 1792
MEM_PROJ_TM = 512
OUT_PROJ_TM = 512
XATTN_TM = 1024
FFN_TM = 512
FFN_TF = 512
LRU_ROWS = 512
ROUTER_TM = 512
MOE_TC = 512
IN0_LRU_IN_BLOCK = 4
IN0_LRU_GATE_BLOCK = 5
IN1_U_BLOCK = 1
IN1_VG_BLOCK = 2


def _cparams(semantics, vmem_mb=VMEM_LIMIT_MB):
    return pltpu.CompilerParams(dimension_semantics=semantics, vmem_limit_bytes=vmem_mb * 2 ** 20)


def _rms(x, g):
    return x * lax.rsqrt(jnp.mean(x * x, axis=-1, keepdims=True) + EPS) * g


def _sigmoid(x):
    return 1.0 / (1.0 + jnp.exp2(x * (-LOG2_E)))


def _silu(x):
    return x * _sigmoid(x)


def _softplus(x):
    return jnp.maximum(x, 0.0) + jnp.log(1.0 + jnp.exp(-jnp.abs(x)))


def _gelu(x):
    c = math.sqrt(2.0 / math.pi)
    return 0.5 * x * (1.0 + jnp.tanh(x * (c + (0.044715 * c) * (x * x))))


def _dot(a, b):
    return jnp.dot(a.astype(BF16), b.astype(BF16), preferred_element_type=F32)


def _dot_nt(a, b):
    return lax.dot_general(a.astype(BF16), b.astype(BF16), (((1,), (1,)), ((), ())),
                           preferred_element_type=F32)


def _dot_tn(a, b):
    return lax.dot_general(a.astype(BF16), b.astype(BF16), (((0,), (0,)), ((), ())),
                           preferred_element_type=F32)


def _split3(x):
    x1 = x.astype(BF16)
    r1 = x - x1.astype(F32)
    x2 = r1.astype(BF16)
    x3 = (r1 - x2.astype(F32)).astype(BF16)
    return x1, x2, x3


def _norm_matmul_body(x_ref, g_ref, w_ref, o_ref, xn_ref):
    @pl.when(pl.program_id(1) == 0)
    def _():
        xn_ref[...] = _rms(x_ref[...], g_ref[...]).astype(BF16)

    o_ref[...] = jnp.dot(xn_ref[...], w_ref[...], preferred_element_type=F32)


def norm_matmul(x, g, w, tm, tn):
    n, k = x.shape
    nout = w.shape[1]
    tm = min(tm, n)
    return pl.pallas_call(
        _norm_matmul_body,
        grid=(n // tm, nout // tn),
        in_specs=[pl.BlockSpec((tm, k), lambda i, j: (i, 0)),
                  pl.BlockSpec((1, k), lambda i, j: (0, 0)),
                  pl.BlockSpec((k, tn), lambda i, j: (0, j))],
        out_specs=pl.BlockSpec((tm, tn), lambda i, j: (i, j)),
        out_shape=jax.ShapeDtypeStruct((n, nout), F32),
        scratch_shapes=[pltpu.VMEM((tm, k), BF16)],
        compiler_params=_cparams(("arbitrary", "arbitrary")),
        name="norm_matmul",
    )(x, g.reshape(1, k), w)


def _outproj_body(a_ref, b_ref, wa_ref, wb_ref, r_ref, g_ref, o_ref):
    acc = jnp.dot(a_ref[...].astype(BF16), wa_ref[...], preferred_element_type=F32)
    acc = acc + jnp.dot(b_ref[...].astype(BF16), wb_ref[...], preferred_element_type=F32)
    o_ref[...] = r_ref[...] + _rms(acc, g_ref[...])


def outproj_norm_resid(a, b, w, resid, g, tm=OUT_PROJ_TM):
    n, ka = a.shape
    kb = b.shape[1]
    d = w.shape[1]
    tm = min(tm, n)
    return pl.pallas_call(
        _outproj_body,
        grid=(n // tm,),
        in_specs=[pl.BlockSpec((tm, ka), lambda i: (i, 0)),
                  pl.BlockSpec((tm, kb), lambda i: (i, 0)),
                  pl.BlockSpec((ka, d), lambda i: (0, 0)),
                  pl.BlockSpec((kb, d), lambda i: (1, 0)),
                  pl.BlockSpec((tm, d), lambda i: (i, 0)),
                  pl.BlockSpec((1, d), lambda i: (0, 0))],
        out_specs=pl.BlockSpec((tm, d), lambda i: (i, 0)),
        out_shape=jax.ShapeDtypeStruct((n, d), F32),
        compiler_params=_cparams(("arbitrary",)),
        name="outproj_norm_resid",
    )(a, b, w, w, resid, g.reshape(1, d))


def _gdn_body(nsub, qkv_ref, z_ref, bd_ref, conv0_ref, s0_ref, cw_ref, pvec_ref, ng_ref,
              o_ref, sfin_ref, cfin_ref, s_scr, xbuf):
    c = pl.program_id(1)
    last = pl.num_programs(1) - 1
    lo = CONV_PAD - (CONV_W - 1)
    rows = nsub * CHUNK

    @pl.when(c == 0)
    def _():
        s_scr[...] = s0_ref[0]
        xbuf[lo:CONV_PAD, :] = conv0_ref[0]

    xbuf[CONV_PAD:CONV_PAD + rows, :] = qkv_ref[...]
    y = xbuf[lo:lo + rows, :] * cw_ref[0:1, :]
    for j in range(1, CONV_W):
        y = y + xbuf[lo + j:lo + j + rows, :] * cw_ref[j:j + 1, :]
    tail = xbuf[rows + lo:rows + CONV_PAD, :]
    xbuf[lo:CONV_PAD, :] = tail

    @pl.when(c == last)
    def _():
        cfin_ref[0] = tail

    act = _silu(y)
    bd = bd_ref[...]
    beta = _sigmoid(bd)
    g_all = -jnp.exp(pvec_ref[0:1, :]) * _softplus(bd + pvec_ref[1:2, :])

    row = lax.broadcasted_iota(jnp.int32, (CHUNK, CHUNK), 0)
    col = lax.broadcasted_iota(jnp.int32, (CHUNK, CHUNK), 1)
    causal = row >= col
    strict = row > col
    blk_xor = row ^ col
    rr = lax.broadcasted_iota(jnp.int32, (rows, rows), 0)
    cc = lax.broadcasted_iota(jnp.int32, (rows, rows), 1)
    tri = jnp.logical_and(rr >= cc, ((rr ^ cc) >> (CHUNK.bit_length() - 1)) == 0)
    g_cum3 = jnp.dot(tri.astype(BF16), jnp.concatenate(_split3(g_all), axis=-1), preferred_element_type=F32)
    g_cum = (g_cum3[:, :LANES] + g_cum3[:, LANES:2 * LANES] + g_cum3[:, 2 * LANES:]) * LOG2_E
    g_cum_t = g_cum.T

    heads = range(GDN_HEADS)
    units = [(ci, h) for ci in range(nsub) for h in heads]
    hw = GDN_HEADS * GDN_D

    def rsl(ci):
        return slice(ci * CHUNK, (ci + 1) * CHUNK)

    gc = [g_cum[rsl(ci), GDN_HEADS + h:GDN_HEADS + h + 1] for ci, h in units]
    gr = [g_cum_t[GDN_HEADS + h:GDN_HEADS + h + 1, rsl(ci)] for ci, h in units]
    un = range(len(units))
    decay = [jnp.where(causal, jnp.exp2(gc[u] - gr[u]), 0.0) for u in un]
    bcol = [beta[rsl(ci), h:h + 1] for ci, h in units]
    q = [act[rsl(ci), GDN_D * h:GDN_D * (h + 1)] for ci, h in units]
    k = [act[rsl(ci), hw + GDN_D * h:hw + GDN_D * (h + 1)] for ci, h in units]
    v = [act[rsl(ci), 2 * hw + GDN_D * h:2 * hw + GDN_D * (h + 1)] for ci, h in units]
    q = [x * lax.rsqrt(jnp.sum(x * x, axis=-1, keepdims=True) + EPS) * (GDN_D ** -0.5) for x in q]
    k = [x * lax.rsqrt(jnp.sum(x * x, axis=-1, keepdims=True) + EPS) for x in k]
    kb = [k[u] * bcol[u] for u in un]
    eg = [jnp.exp2(gc[u]) for u in un]
    qa = [_dot_nt(jnp.concatenate([q[u], kb[u]], axis=0), k[u]) for u in un]
    qk = [qa[u][:CHUNK] * decay[u] for u in un]
    a_low = [jnp.where(strict, qa[u][CHUNK:] * decay[u], 0.0) for u in un]
    m = [jnp.where((blk_xor >> 2) == 0, -a_low[u], 0.0) for u in un]
    m2 = [_dot(m[u], m[u]) for u in un]
    n = [m[u] + m2[u] + _dot(m[u], m2[u]) for u in un]
    for lg in range(2, 6):
        low = [jnp.where((blk_xor >> lg) == 1, a_low[u], 0.0) for u in un]
        tl = [low[u] + _dot(n[u], low[u]) for u in un]
        n = [n[u] - (tl[u] + _dot(tl[u], n[u])) for u in un]
    rhs = [jnp.concatenate([v[u] * bcol[u], kb[u] * eg[u]], axis=-1) for u in un]
    sol = [rhs[u] + _dot(n[u], rhs[u]) for u in un]
    wq = [jnp.concatenate([sol[u][:, GDN_D:], q[u] * eg[u]], axis=0) for u in un]
    g_last = [g_cum[(ci + 1) * CHUNK - 1:(ci + 1) * CHUNK, GDN_HEADS + h:GDN_HEADS + h + 1] for ci, h in units]
    k_dec = [k[u] * jnp.exp2(g_last[u] - gc[u]) for u in un]
    s = [s_scr[h] for h in heads]
    for ci in range(nsub):
        us = [ci * GDN_HEADS + h for h in heads]
        ws = [_dot(wq[us[h]], s[h]) for h in heads]
        v_new = [sol[us[h]][:, :GDN_D] - ws[h][:CHUNK] for h in heads]
        o = [ws[h][CHUNK:] + _dot(qk[us[h]], v_new[h]) for h in heads]
        s = [s[h] * jnp.exp2(g_last[us[h]]) + _dot_tn(k_dec[us[h]], v_new[h]) for h in heads]
        for h in heads:
            zh = z_ref[rsl(ci), GDN_D * h:GDN_D * (h + 1)]
            o_ref[rsl(ci), GDN_D * h:GDN_D * (h + 1)] = _rms(o[h], ng_ref[...]) * _silu(zh)
    for h in heads:
        s_scr[h] = s[h]

    @pl.when(c == last)
    def _():
        sfin_ref[0] = s_scr[...]


def gdn_mixer(p0, bsz, t, conv0, s0, conv_w, a_log, dt_bias, norm_g):
    nsub = min(GDN_CHUNKS_PER_STEP, t // CHUNK)
    rows = nsub * CHUNK
    nc = t // rows
    n = bsz * t
    pvec = jnp.zeros((2, LANES), F32)
    pvec = pvec.at[0, GDN_HEADS:2 * GDN_HEADS].set(a_log).at[1, GDN_HEADS:2 * GDN_HEADS].set(dt_bias)
    vw = GDN_HEADS * GDN_D
    return pl.pallas_call(
        functools.partial(_gdn_body, nsub),
        grid=(bsz, nc),
        in_specs=[pl.BlockSpec((rows, GDN_QKV_W), lambda b, c: (b * nc + c, 0)),
                  pl.BlockSpec((rows, vw), lambda b, c: (b * nc + c, GDN_QKV_W // vw)),
                  pl.BlockSpec((rows, LANES), lambda b, c: (b * nc + c, BD_COL_BLOCK)),
                  pl.BlockSpec((1, CONV_W - 1, GDN_QKV_W), lambda b, c: (b, 0, 0)),
                  pl.BlockSpec((1, GDN_HEADS, GDN_D, GDN_D), lambda b, c: (b, 0, 0, 0)),
                  pl.BlockSpec((CONV_W, GDN_QKV_W), lambda b, c: (0, 0)),
                  pl.BlockSpec((2, LANES), lambda b, c: (0, 0)),
                  pl.BlockSpec((1, GDN_D), lambda b, c: (0, 0))],
        out_specs=[pl.BlockSpec((rows, vw), lambda b, c: (b * nc + c, 0)),
                   pl.BlockSpec((1, GDN_HEADS, GDN_D, GDN_D), lambda b, c: (b, 0, 0, 0)),
                   pl.BlockSpec((1, CONV_W - 1, GDN_QKV_W), lambda b, c: (b, 0, 0))],
        out_shape=[jax.ShapeDtypeStruct((n, vw), F32),
                   jax.ShapeDtypeStruct((bsz, GDN_HEADS, GDN_D, GDN_D), F32),
                   jax.ShapeDtypeStruct((bsz, CONV_W - 1, GDN_QKV_W), F32)],
        scratch_shapes=[pltpu.VMEM((GDN_HEADS, GDN_D, GDN_D), F32),
                        pltpu.VMEM((CONV_PAD + rows, GDN_QKV_W), F32)],
        compiler_params=_cparams(("arbitrary", "arbitrary")),
        name="gdn_mixer",
    )(p0, p0, p0, conv0, s0, conv_w, pvec, norm_g.reshape(1, GDN_D))


def _lru_body(tl, x_ref, gate_ref, conv0_ref, h0_ref, cw_ref, cb_ref, wa_ref, ba_ref, wx_ref, bx_ref,
              lam_ref, y_ref, hfin_ref, cfin_ref, h_scr, xbuf, abuf, bbuf):
    c = pl.program_id(1)
    last = pl.num_programs(1) - 1
    lo = CONV_PAD - (CONV_W - 1)
    pad = tl // 2

    @pl.when(c == 0)
    def _():
        h_scr[...] = h0_ref[0]
        xbuf[lo:CONV_PAD, :] = conv0_ref[0]
        abuf[0:pad, :] = jnp.ones((pad, LRU_W), F32)
        bbuf[0:pad, :] = jnp.zeros((pad, LRU_W), F32)

    xbuf[CONV_PAD:CONV_PAD + tl, :] = x_ref[...]
    xr = xbuf[lo:lo + tl, :] * cw_ref[0:1, :]
    for j in range(1, CONV_W):
        xr = xr + xbuf[lo + j:lo + j + tl, :] * cw_ref[j:j + 1, :]
    tail = xbuf[tl + lo:tl + CONV_PAD, :]
    xbuf[lo:CONV_PAD, :] = tail

    @pl.when(c == last)
    def _():
        cfin_ref[0] = tail

    xr = xr + cb_ref[...]
    ga = jnp.concatenate([_dot(xr[:, LRU_BLOCK_W * n:LRU_BLOCK_W * (n + 1)], wa_ref[n])
                          for n in range(LRU_BLOCKS)], axis=-1)
    gx = jnp.concatenate([_dot(xr[:, LRU_BLOCK_W * n:LRU_BLOCK_W * (n + 1)], wx_ref[n])
                          for n in range(LRU_BLOCKS)], axis=-1)
    gate_a = _sigmoid(ga + ba_ref[...])
    gate_x = _sigmoid(gx + bx_ref[...])
    a = jnp.exp2(gate_a * ((-LRU_C * LOG2_E) * _softplus(-lam_ref[...])))
    b = jnp.sqrt(1.0 - a * a) * gate_x * xr
    sub = lax.broadcasted_iota(jnp.int32, (tl, 1), 0) % SUBLANES
    d = 1
    while d < SUBLANES:
        abuf[pad:pad + tl, :] = a
        bbuf[pad:pad + tl, :] = b
        in_group = sub >= d
        a_sh = jnp.where(in_group, abuf[pad - d:pad - d + tl, :], 1.0)
        b_sh = jnp.where(in_group, bbuf[pad - d:pad - d + tl, :], 0.0)
        b = a * b_sh + b
        a = a * a_sh
        d *= 2
    carry = h_scr[...]
    pieces = []
    for r in range(0, tl, SUBLANES):
        h_grp = a[r:r + SUBLANES, :] * carry + b[r:r + SUBLANES, :]
        pieces.append(h_grp)
        carry = h_grp[SUBLANES - 1:SUBLANES, :]
    h = jnp.concatenate(pieces, axis=0)
    h_last = carry
    h_scr[...] = h_last
    y_ref[...] = h * _gelu(gate_ref[...])

    @pl.when(c == last)
    def _():
        hfin_ref[0] = h_last


def lru_mixer(p0, bsz, t, conv0, h0, conv_w, conv_b, w_a, b_a, w_x, b_x, lam):
    tl = min(t, LRU_ROWS)
    nc = t // tl
    n = bsz * t
    row = lambda v: v.reshape(1, LRU_W)
    return pl.pallas_call(
        functools.partial(_lru_body, tl),
        grid=(bsz, nc),
        in_specs=[pl.BlockSpec((tl, LRU_W), lambda b, c: (b * nc + c, IN0_LRU_IN_BLOCK)),
                  pl.BlockSpec((tl, LRU_W), lambda b, c: (b * nc + c, IN0_LRU_GATE_BLOCK)),
                  pl.BlockSpec((1, CONV_W - 1, LRU_W), lambda b, c: (b, 0, 0)),
                  pl.BlockSpec((1, 1, LRU_W), lambda b, c: (b, 0, 0)),
                  pl.BlockSpec((CONV_W, LRU_W), lambda b, c: (0, 0)),
                  pl.BlockSpec((1, LRU_W), lambda b, c: (0, 0)),
                  pl.BlockSpec((LRU_BLOCKS, LRU_BLOCK_W, LRU_BLOCK_W), lambda b, c: (0, 0, 0)),
                  pl.BlockSpec((1, LRU_W), lambda b, c: (0, 0)),
                  pl.BlockSpec((LRU_BLOCKS, LRU_BLOCK_W, LRU_BLOCK_W), lambda b, c: (0, 0, 0)),
                  pl.BlockSpec((1, LRU_W), lambda b, c: (0, 0)),
                  pl.BlockSpec((1, LRU_W), lambda b, c: (0, 0))],
        out_specs=[pl.BlockSpec((tl, LRU_W), lambda b, c: (b * nc + c, 0)),
                   pl.BlockSpec((1, 1, LRU_W), lambda b, c: (b, 0, 0)),
                   pl.BlockSpec((1, CONV_W - 1, LRU_W), lambda b, c: (b, 0, 0))],
        out_shape=[jax.ShapeDtypeStruct((n, LRU_W), F32),
                   jax.ShapeDtypeStruct((bsz, 1, LRU_W), F32),
                   jax.ShapeDtypeStruct((bsz, CONV_W - 1, LRU_W), F32)],
        scratch_shapes=[pltpu.VMEM((1, LRU_W), F32),
                        pltpu.VMEM((CONV_PAD + tl, LRU_W), F32),
                        pltpu.VMEM((tl // 2 + tl, LRU_W), F32),
                        pltpu.VMEM((tl // 2 + tl, LRU_W), F32)],
        compiler_params=_cparams(("arbitrary", "arbitrary")),
        name="lru_mixer",
    )(p0, p0, conv0, h0.reshape(bsz, 1, LRU_W), conv_w, row(conv_b), w_a, row(b_a), w_x, row(b_x), row(lam))


def _xattn_body(x_ref, mk_ref, mv_ref, wq_ref, wo_ref, g_in_ref, g_out_ref, o_ref):
    tm = x_ref.shape[0]
    sub = min(tm, XATTN_SUB_ROWS)
    parts = range(tm // sub)
    x = [x_ref[sub * j:sub * (j + 1), :] for j in parts]
    q = [jnp.dot(_rms(x[j], g_in_ref[...]).astype(BF16), wq_ref[...], preferred_element_type=F32) for j in parts]
    mk = mk_ref[0].astype(BF16)
    mv = mv_ref[0].astype(BF16)
    outs = [[] for _ in parts]
    for h in range(MEM_HEADS):
        sl = slice(MEM_HD * h, MEM_HD * (h + 1))
        s = [_dot_nt(q[j][:, sl], mk[:, sl]) * (MEM_HD ** -0.5) for j in parts]
        m = [jnp.max(s[j], axis=-1, keepdims=True) for j in parts]
        p = [jnp.exp(s[j] - m[j]) for j in parts]
        for j in parts:
            outs[j].append(_dot(p[j], mv[:, sl]) / jnp.sum(p[j], axis=-1, keepdims=True))
    y = [jnp.dot(jnp.concatenate(outs[j], axis=-1).astype(BF16), wo_ref[...], preferred_element_type=F32)
         for j in parts]
    for j in parts:
        o_ref[sub * j:sub * (j + 1), :] = x[j] + _rms(y[j], g_out_ref[...])


def cross_attention(x, bsz, t, mem_k, mem_v, wq, wo, g_in, g_out):
    tm = min(t, XATTN_TM)
    nt = t // tm
    n, d = x.shape
    return pl.pallas_call(
        _xattn_body,
        grid=(bsz, nt),
        in_specs=[pl.BlockSpec((tm, d), lambda b, i: (b * nt + i, 0)),
                  pl.BlockSpec((1, MEM_LEN, MEM_W), lambda b, i: (b, 0, 0)),
                  pl.BlockSpec((1, MEM_LEN, MEM_W), lambda b, i: (b, 0, 0)),
                  pl.BlockSpec((d, MEM_W), lambda b, i: (0, 0)),
                  pl.BlockSpec((MEM_W, d), lambda b, i: (0, 0)),
                  pl.BlockSpec((1, d), lambda b, i: (0, 0)),
                  pl.BlockSpec((1, d), lambda b, i: (0, 0))],
        out_specs=pl.BlockSpec((tm, d), lambda b, i: (b * nt + i, 0)),
        out_shape=jax.ShapeDtypeStruct((n, d), F32),
        compiler_params=_cparams(("arbitrary", "arbitrary")),
        name="cross_attention",
    )(x, mem_k, mem_v, wq, wo, g_in.reshape(1, d), g_out.reshape(1, d))


def _ffn_body(x_ref, g_in_ref, wg_ref, wu_ref, wd_ref, g_out_ref, o_ref, xn_ref, acc_ref):
    f = pl.program_id(1)

    @pl.when(f == 0)
    def _():
        xn_ref[...] = _rms(x_ref[...], g_in_ref[...]).astype(BF16)
        acc_ref[...] = jnp.zeros_like(acc_ref)

    xn = xn_ref[...]
    gate = jnp.dot(xn, wg_ref[...], preferred_element_type=F32)
    up = jnp.dot(xn, wu_ref[...], preferred_element_type=F32)
    acc_ref[...] += jnp.dot((_silu(gate) * up).astype(BF16), wd_ref[...], preferred_element_type=F32)

    @pl.when(f == pl.num_programs(1) - 1)
    def _():
        o_ref[...] = x_ref[...] + _rms(acc_ref[...], g_out_ref[...])


def dense_ffn(x, g_in, wg, wu, wd, g_out, tm=FFN_TM, tf=FFN_TF):
    n, d = x.shape
    ff = wg.shape[1]
    tm = min(tm, n)
    return pl.pallas_call(
        _ffn_body,
        grid=(n // tm, ff // tf),
        in_specs=[pl.BlockSpec((tm, d), lambda i, f: (i, 0)),
                  pl.BlockSpec((1, d), lambda i, f: (0, 0)),
                  pl.BlockSpec((d, tf), lambda i, f: (0, f)),
                  pl.BlockSpec((d, tf), lambda i, f: (0, f)),
                  pl.BlockSpec((tf, d), lambda i, f: (f, 0)),
                  pl.BlockSpec((1, d), lambda i, f: (0, 0))],
        out_specs=pl.BlockSpec((tm, d), lambda i, f: (i, 0)),
        out_shape=jax.ShapeDtypeStruct((n, d), F32),
        scratch_shapes=[pltpu.VMEM((tm, d), BF16), pltpu.VMEM((tm, d), F32)],
        compiler_params=_cparams(("arbitrary", "arbitrary")),
        name="dense_ffn",
    )(x, g_in.reshape(1, d), wg, wu, wd, g_out.reshape(1, d))


def _swa_body(start, nsub, q_ref, kv_ref, cos_ref, sin_ref, kprev_ref, vprev_ref, sink_ref,
              o_ref, krot_ref, kbuf, vbuf):
    c = pl.program_id(1)
    rows_t = nsub * CHUNK

    @pl.when(c == 0)
    def _():
        kbuf[0:WINDOW, :] = kprev_ref[0]
        vbuf[0:WINDOW, :] = vprev_ref[0]

    cos = cos_ref[...]
    sin = sin_ref[...]
    lane = lax.broadcasted_iota(jnp.int32, (rows_t, LANES), 1)
    first_half = (lane % SWA_HD) < (SWA_HD // 2)

    def rope(x):
        outs = []
        for j in range(x.shape[1] // LANES):
            xb = x[:, LANES * j:LANES * (j + 1)]
            fwd = pltpu.roll(xb, LANES - SWA_HD // 2, 1)
            bwd = pltpu.roll(xb, SWA_HD // 2, 1)
            outs.append(xb * cos + jnp.where(first_half, fwd, bwd) * sin)
        return jnp.concatenate(outs, axis=-1)

    q = rope(q_ref[...])
    kv = kv_ref[...]
    k = rope(kv[:, :SWA_KV_W])
    krot_ref[...] = k
    kbuf[WINDOW:WINDOW + rows_t, :] = k
    vbuf[WINDOW:WINDOW + rows_t, :] = kv[:, SWA_KV_W:]

    nk = WINDOW + CHUNK
    rows = SWA_GROUP * CHUNK
    key_off = lax.broadcasted_iota(jnp.int32, (rows, nk), 1) - WINDOW
    row_head = lax.broadcasted_iota(jnp.int32, (rows, 1), 0) // CHUNK
    kvh = range(SWA_KV_HEADS)
    units = [(ci, hk) for ci in range(nsub) for hk in kvh]
    un = range(len(units))
    valid = [start + (c * nsub + ci) * CHUNK + key_off >= 0 for ci in range(nsub)]
    qg = [jnp.concatenate([q[ci * CHUNK:(ci + 1) * CHUNK,
                             SWA_HD * (hk * SWA_GROUP + gi):SWA_HD * (hk * SWA_GROUP + gi + 1)]
                           for gi in range(SWA_GROUP)], axis=0) for ci, hk in units]
    kh = [kbuf[ci * CHUNK:ci * CHUNK + nk, SWA_HD * hk:SWA_HD * (hk + 1)] for ci, hk in units]
    vh = [vbuf[ci * CHUNK:ci * CHUNK + nk, SWA_HD * hk:SWA_HD * (hk + 1)] for ci, hk in units]
    sink_h = []
    for hk in kvh:
        col = jnp.full((rows, 1), sink_ref[hk * SWA_GROUP], F32)
        for gi in range(1, SWA_GROUP):
            col = jnp.where(row_head == gi, sink_ref[hk * SWA_GROUP + gi], col)
        sink_h.append(col)
    sink = [sink_h[hk] for _, hk in units]
    s = [jnp.where(valid[units[u][0]], _dot_nt(qg[u], kh[u]) * (SWA_HD ** -0.5), -jnp.inf) for u in un]
    m = [jnp.maximum(jnp.max(s[u], axis=-1, keepdims=True), sink[u]) for u in un]
    p = [jnp.exp(s[u] - m[u]) for u in un]
    denom = [jnp.sum(p[u], axis=-1, keepdims=True) + jnp.exp(sink[u] - m[u]) for u in un]
    og = [_dot(p[u], vh[u]) / denom[u] for u in un]
    for u, (ci, hk) in enumerate(units):
        for pair in range(SWA_GROUP // 2):
            lo_rows = og[u][CHUNK * 2 * pair:CHUNK * (2 * pair + 1)]
            hi_rows = og[u][CHUNK * (2 * pair + 1):CHUNK * (2 * pair + 2)]
            lane0 = SWA_HD * (hk * SWA_GROUP + 2 * pair)
            o_ref[ci * CHUNK:(ci + 1) * CHUNK, lane0:lane0 + 2 * SWA_HD] = jnp.concatenate(
                [lo_rows, hi_rows], axis=-1)

    k_keep = kbuf[rows_t:rows_t + WINDOW, :]
    v_keep = vbuf[rows_t:rows_t + WINDOW, :]
    kbuf[0:WINDOW, :] = k_keep
    vbuf[0:WINDOW, :] = v_keep


def _rope_tables(start, t):
    half = SWA_HD // 2
    inv_freq = jnp.exp(-math.log(ROPE_THETA) * jnp.arange(half, dtype=F32) / half)
    ang = (start + jnp.arange(t)).astype(F32)[:, None] * inv_freq[None, :]
    cos = jnp.cos(ang)
    sin = jnp.sin(ang)
    return jnp.tile(cos, (1, LANES // half)), jnp.tile(jnp.concatenate([-sin, sin], axis=-1), (1, LANES // SWA_HD))


def swa_mixer(p1, bsz, t, start, k_prev, v_prev, sinks):
    nsub = min(SWA_CHUNKS_PER_STEP, t // CHUNK)
    rows = nsub * CHUNK
    nc = t // rows
    n = bsz * t
    qw = SWA_Q_HEADS * SWA_HD
    cos, sin = _rope_tables(start, t)
    return pl.pallas_call(
        functools.partial(_swa_body, start, nsub),
        grid=(bsz, nc),
        in_specs=[pl.BlockSpec((rows, qw), lambda b, c: (b * nc + c, 0)),
                  pl.BlockSpec((rows, 2 * SWA_KV_W), lambda b, c: (b * nc + c, 3 * qw // (2 * SWA_KV_W))),
                  pl.BlockSpec((rows, LANES), lambda b, c: (c, 0)),
                  pl.BlockSpec((rows, LANES), lambda b, c: (c, 0)),
                  pl.BlockSpec((1, WINDOW, SWA_KV_W), lambda b, c: (b, 0, 0)),
                  pl.BlockSpec((1, WINDOW, SWA_KV_W), lambda b, c: (b, 0, 0)),
                  pl.BlockSpec(memory_space=pltpu.SMEM)],
        out_specs=[pl.BlockSpec((rows, qw), lambda b, c: (b * nc + c, 0)),
                   pl.BlockSpec((rows, SWA_KV_W), lambda b, c: (b * nc + c, 0))],
        out_shape=[jax.ShapeDtypeStruct((n, qw), F32),
                   jax.ShapeDtypeStruct((n, SWA_KV_W), F32)],
        scratch_shapes=[pltpu.VMEM((WINDOW + rows, SWA_KV_W), F32),
                        pltpu.VMEM((WINDOW + rows, SWA_KV_W), F32)],
        compiler_params=_cparams(("arbitrary", "arbitrary")),
        name="swa_mixer",
    )(p1, p1, cos, sin, k_prev, v_prev, sinks)


def _smlp_body(lc, u_ref, v_ref, lg_ref, lb_ref, ws_ref, bs_ref, y_ref, vn_ref):
    v = _gelu(v_ref[...])
    mu = jnp.mean(v, axis=-1, keepdims=True)
    vc = v - mu
    vn = vc * lax.rsqrt(jnp.mean(vc * vc, axis=-1, keepdims=True) + EPS) * lg_ref[...] + lb_ref[...]
    vn_ref[...] = vn
    u = _gelu(u_ref[...])
    row = lax.broadcasted_iota(jnp.int32, (lc, lc), 0)
    col = lax.broadcasted_iota(jnp.int32, (lc, lc), 1)
    for g in range(SMLP_GROUPS):
        sl = slice(SMLP_GROUP_W * g, SMLP_GROUP_W * (g + 1))
        w = jnp.where(row >= col, ws_ref[g, 0:lc, 0:lc], 0.0)
        for c in range(u_ref.shape[0] // lc):
            rows = slice(lc * c, lc * (c + 1))
            s = _dot(w, vn[rows, sl]) + bs_ref[0:lc, g:g + 1]
            y_ref[rows, sl] = u[rows, sl] * s


def smlp_mixer(p1, bsz, t, ln_g, ln_b, w_spatial, b_spatial):
    lc = min(SMLP_CHUNK, t)
    n = bsz * t
    rows = lc * min(SMLP_CHUNKS_PER_STEP, n // lc)
    row = lambda v: v.reshape(1, SMLP_W)
    return pl.pallas_call(
        functools.partial(_smlp_body, lc),
        grid=(n // rows,),
        in_specs=[pl.BlockSpec((rows, SMLP_W), lambda i: (i, IN1_U_BLOCK)),
                  pl.BlockSpec((rows, SMLP_W), lambda i: (i, IN1_VG_BLOCK)),
                  pl.BlockSpec((1, SMLP_W), lambda i: (0, 0)),
                  pl.BlockSpec((1, SMLP_W), lambda i: (0, 0)),
                  pl.BlockSpec((SMLP_GROUPS, SMLP_CHUNK, SMLP_CHUNK), lambda i: (0, 0, 0)),
                  pl.BlockSpec((SMLP_CHUNK, SMLP_GROUPS), lambda i: (0, 0))],
        out_specs=[pl.BlockSpec((rows, SMLP_W), lambda i: (i, 0)),
                   pl.BlockSpec((rows, SMLP_W), lambda i: (i, 0))],
        out_shape=[jax.ShapeDtypeStruct((n, SMLP_W), F32),
                   jax.ShapeDtypeStruct((n, SMLP_W), F32)],
        compiler_params=_cparams(("arbitrary",)),
        name="smlp_mixer",
    )(p1, p1, row(ln_g), row(ln_b), w_spatial, b_spatial.T)


def _router_body(x_ref, g_ref, wr_ref, idx_ref, gate_ref):
    hn = _rms(x_ref[...], g_ref[...])
    wr = wr_ref[...]
    h1 = hn.astype(BF16)
    h2 = (hn - h1.astype(F32)).astype(BF16)
    w1 = wr.astype(BF16)
    w2 = (wr - w1.astype(F32)).astype(BF16)
    nt = (((1,), (1,)), ((), ()))
    lg = lax.dot_general(jnp.concatenate([w1, w2], axis=0), h1, nt, preferred_element_type=F32)
    logits = (lg[:N_EXPERTS] + lg[N_EXPERTS:]
              + lax.dot_general(w1, h2, nt, preferred_element_type=F32))
    e_iota = lax.broadcasted_iota(jnp.int32, logits.shape, 0)
    m1 = jnp.max(logits, axis=0, keepdims=True)
    i1 = jnp.min(jnp.where(logits == m1, e_iota, N_EXPERTS), axis=0, keepdims=True)
    rest = jnp.where(e_iota == i1, -jnp.inf, logits)
    m2 = jnp.max(rest, axis=0, keepdims=True)
    i2 = jnp.min(jnp.where(rest == m2, e_iota, N_EXPERTS), axis=0, keepdims=True)
    e2 = jnp.exp(m2 - m1)
    den = 1.0 + e2
    idx_ref[...] = jnp.concatenate([i1, i2], axis=0)
    tm = logits.shape[1]
    gates = jnp.concatenate([1.0 / den, e2 / den, jnp.zeros((LANES - 2, tm), F32)], axis=0)
    gate_ref[...] = gates.T


def moe_router(x, g, w_router, tm=ROUTER_TM):
    n, d = x.shape
    tm = min(tm, n)
    return pl.pallas_call(
        _router_body,
        grid=(n // tm,),
        in_specs=[pl.BlockSpec((tm, d), lambda i: (i, 0)),
                  pl.BlockSpec((1, d), lambda i: (0, 0)),
                  pl.BlockSpec((N_EXPERTS, d), lambda i: (0, 0))],
        out_specs=[pl.BlockSpec((2, tm), lambda i: (0, i)),
                   pl.BlockSpec((tm, LANES), lambda i: (i, 0))],
        out_shape=[jax.ShapeDtypeStruct((2, n), jnp.int32),
                   jax.ShapeDtypeStruct((n, LANES), F32)],
        compiler_params=_cparams(("arbitrary",)),
        name="moe_router",
    )(x, g.reshape(1, d), w_router.T)


def _moe_block_rows(n_tokens):
    return MOE_BM if 2 * n_tokens // N_EXPERTS >= 4 * MOE_BM else MOE_BM // 2


def _moe_plan(top_idx, bm):
    n = top_idx.shape[1]
    flat_e = top_idx.reshape(-1)
    onehot = (flat_e[:, None] == jnp.arange(N_EXPERTS, dtype=jnp.int32)[None, :]).astype(jnp.int32)
    rank = jnp.sum(jnp.cumsum(onehot, axis=0) * onehot, axis=1) - 1
    counts = jnp.sum(onehot, axis=0)
    padded = (counts + bm - 1) // bm * bm
    pad_end = jnp.cumsum(padded)
    pad_start = pad_end - padded
    dest = jnp.sum(onehot * pad_start[None, :], axis=1) + rank
    n_blk = -(-2 * n // bm) + N_EXPERTS
    blk_start = jnp.arange(n_blk, dtype=jnp.int32) * bm
    blk_e = jnp.minimum(jnp.sum((blk_start[:, None] >= pad_end[None, :]).astype(jnp.int32), axis=1),
                        N_EXPERTS - 1)
    blk_valid = jnp.clip((pad_start + counts)[blk_e] - blk_start, 0, bm).astype(jnp.int32)
    n_active = (pad_end[-1] // bm).astype(jnp.int32).reshape(1)
    return dest.reshape(2, n).astype(jnp.int32), blk_e.astype(jnp.int32), blk_valid, n_active, n_blk


def _moe_dispatch_body(tc, dest_ref, x_ref, g_ref, xs_hbm, hn_scr, sem):
    d = x_ref.shape[1]
    hn_scr[...] = _rms(x_ref[...], g_ref[...]).reshape(tc // SUBLANES, SUBLANES, d)

    def start(i8, carry):
        for j in range(SUBLANES):
            for slot in range(2):
                dst_row = dest_ref[0, slot, i8 * SUBLANES + j]
                pltpu.make_async_copy(hn_scr.at[i8, pl.ds(j, 1), :], xs_hbm.at[pl.ds(dst_row, 1), :],
                                      sem).start(priority=slot)
        return carry

    lax.fori_loop(0, tc // SUBLANES, start, 0)
    for _ in range(2):
        pltpu.make_async_copy(x_ref, xs_hbm.at[pl.ds(0, tc), :], sem).wait()


def _moe_dispatch_into_body(tc, dest_ref, x_ref, g_ref, xs_prev_hbm, xs_hbm, hn_scr, sem):
    del xs_prev_hbm
    _moe_dispatch_body(tc, dest_ref, x_ref, g_ref, xs_hbm, hn_scr, sem)


def moe_dispatch(x, g, dest_blocks, n_rows, tc, xs_prev=None):
    n, d = x.shape
    in_specs = [pl.BlockSpec((1, 2, tc), lambda i: (i, 0, 0), memory_space=pltpu.SMEM),
                pl.BlockSpec((tc, d), lambda i: (i, 0)),
                pl.BlockSpec((1, d), lambda i: (0, 0))]
    args = [dest_blocks, x, g.reshape(1, d)]
    body, aliases = _moe_dispatch_body, {}
    if xs_prev is not None:
        in_specs.append(pl.BlockSpec(memory_space=pl.ANY))
        args.append(xs_prev)
        body, aliases = _moe_dispatch_into_body, {3: 0}
    return pl.pallas_call(
        functools.partial(body, tc),
        grid=(n // tc,),
        in_specs=in_specs,
        out_specs=pl.BlockSpec(memory_space=pl.ANY),
        out_shape=jax.ShapeDtypeStruct((n_rows, d), F32),
        scratch_shapes=[pltpu.VMEM((tc // SUBLANES, SUBLANES, d), F32), pltpu.SemaphoreType.DMA(())],
        input_output_aliases=aliases,
        compiler_params=_cparams(("arbitrary",)),
        name="moe_dispatch",
    )(*args)


def _moe_ffn_body(blk_e_ref, blk_valid_ref, nact_ref, xs_ref, wg_hbm, wu_hbm, wd_hbm, ys_ref,
                  wg_scr, wu_scr, wd_scr, sem):
    i = pl.program_id(0)
    active = i < nact_ref[0]
    e = blk_e_ref[i]
    new_expert = jnp.logical_or(i == 0, e != blk_e_ref[jnp.maximum(i - 1, 0)])

    @pl.when(jnp.logical_and(active, new_expert))
    def _():
        copies = [pltpu.make_async_copy(w_hbm.at[e], w_scr, sem.at[j])
                  for j, (w_hbm, w_scr) in enumerate(((wg_hbm, wg_scr), (wu_hbm, wu_scr), (wd_hbm, wd_scr)))]
        for cp in copies:
            cp.start()
        for cp in copies:
            cp.wait()

    @pl.when(active)
    def _():
        row = lax.broadcasted_iota(jnp.int32, (xs_ref.shape[0], 1), 0)
        xb = jnp.where(row < blk_valid_ref[i], xs_ref[...], 0.0).astype(BF16)
        acc = None
        for j in range(FF_EXPERT // MOE_TF):
            sl = slice(MOE_TF * j, MOE_TF * (j + 1))
            gate = jnp.dot(xb, wg_scr[:, sl], preferred_element_type=F32)
            up = jnp.dot(xb, wu_scr[:, sl], preferred_element_type=F32)
            part = jnp.dot((_silu(gate) * up).astype(BF16), wd_scr[sl, :], preferred_element_type=F32)
            acc = part if acc is None else acc + part
        ys_ref[...] = acc

    @pl.when(jnp.logical_not(active))
    def _():
        ys_ref[...] = jnp.zeros_like(ys_ref)


def moe_expert_ffn(xs, blk_e, blk_valid, n_active, wg, wu, wd, bm):
    n_rows, d = xs.shape
    n_blk = n_rows // bm
    return pl.pallas_call(
        _moe_ffn_body,
        grid_spec=pltpu.PrefetchScalarGridSpec(
            num_scalar_prefetch=3,
            grid=(n_blk,),
            in_specs=[pl.BlockSpec((bm, d), lambda i, be, bv, na: (jnp.minimum(i, na[0] - 1), 0)),
                      pl.BlockSpec(memory_space=pl.ANY),
                      pl.BlockSpec(memory_space=pl.ANY),
                      pl.BlockSpec(memory_space=pl.ANY)],
            out_specs=pl.BlockSpec((bm, d), lambda i, be, bv, na: (i, 0)),
            scratch_shapes=[pltpu.VMEM((d, FF_EXPERT), BF16), pltpu.VMEM((d, FF_EXPERT), BF16),
                            pltpu.VMEM((FF_EXPERT, d), BF16), pltpu.SemaphoreType.DMA((3,))]),
        out_shape=jax.ShapeDtypeStruct((n_rows, d), F32),
        compiler_params=_cparams(("arbitrary",)),
        name="moe_expert_ffn",
    )(blk_e, blk_valid, n_active, xs, wg, wu, wd)


def _moe_combine_body(tc, pos_ref, pos_next_ref, ys_hbm, gate_ref, x_ref, g_ref, o_ref, buf, sem):
    i = pl.program_id(0)
    nb = pl.num_programs(0)
    slot = i % 2

    def issue(p_ref, s):
        def start(i8, carry):
            for j in range(SUBLANES):
                for choice in range(2):
                    src_row = p_ref[0, choice, i8 * SUBLANES + j]
                    pltpu.make_async_copy(ys_hbm.at[pl.ds(src_row, 1), :], buf.at[s, choice, i8, pl.ds(j, 1), :],
                                          sem.at[s]).start(priority=choice)
            return carry

        lax.fori_loop(0, tc // SUBLANES, start, 0)

    @pl.when(i == 0)
    def _():
        issue(pos_ref, 0)

    @pl.when(i + 1 < nb)
    def _():
        issue(pos_next_ref, 1 - slot)

    d = x_ref.shape[1]
    for choice in range(2):
        pltpu.make_async_copy(ys_hbm.at[pl.ds(0, tc), :], o_ref, sem.at[slot]).wait()
    gates = gate_ref[...]
    y = (gates[:, 0:1] * buf[slot, 0].reshape(tc, d) + gates[:, 1:2] * buf[slot, 1].reshape(tc, d))
    o_ref[...] = x_ref[...] + _rms(y, g_ref[...])


def moe_combine(ys, pos_blocks, gates, x, g, tc):
    n, d = x.shape
    nb = n // tc
    return pl.pallas_call(
        functools.partial(_moe_combine_body, tc),
        grid=(nb,),
        in_specs=[pl.BlockSpec((1, 2, tc), lambda i: (i, 0, 0), memory_space=pltpu.SMEM),
                  pl.BlockSpec((1, 2, tc), lambda i: (jnp.minimum(i + 1, nb - 1), 0, 0), memory_space=pltpu.SMEM),
                  pl.BlockSpec(memory_space=pl.ANY),
                  pl.BlockSpec((tc, LANES), lambda i: (i, 0)),
                  pl.BlockSpec((tc, d), lambda i: (i, 0)),
                  pl.BlockSpec((1, d), lambda i: (0, 0))],
        out_specs=pl.BlockSpec((tc, d), lambda i: (i, 0)),
        out_shape=jax.ShapeDtypeStruct((n, d), F32),
        scratch_shapes=[pltpu.VMEM((2, 2, tc // SUBLANES, SUBLANES, d), F32), pltpu.SemaphoreType.DMA((2,))],
        compiler_params=_cparams(("arbitrary",)),
        name="moe_combine",
    )(pos_blocks, pos_blocks, ys, gates, x, g.reshape(1, d))


def moe_block(xs_in, g_in, w_router, wg, wu, wd, g_out, tc=MOE_TC):
    routed = [moe_router(x, g_in, w_router) for x in xs_in]
    sizes = [x.shape[0] for x in xs_in]
    n_all = sum(sizes)
    bm = _moe_block_rows(n_all)
    dest, blk_e, blk_valid, n_active, n_blk = _moe_plan(jnp.concatenate([r[0] for r in routed], axis=1), bm)
    xs, dest_blocks, off = None, [], 0
    for x, n in zip(xs_in, sizes):
        t = min(tc, n)
        dest_blocks.append(dest[:, off:off + n].reshape(2, n // t, t).transpose(1, 0, 2))
        xs = moe_dispatch(x, g_in, dest_blocks[-1], n_blk * bm, t, xs_prev=xs)
        off += n
    ys = moe_expert_ffn(xs, blk_e, blk_valid, n_active, wg, wu, wd, bm)
    return [moe_combine(ys, db, r[1], x, g_out, min(tc, x.shape[0]))
            for x, db, r in zip(xs_in, dest_blocks, routed)]


def _forward(x, start, keep, mem_k, mem_v, gdn_conv0, gdn_s0, lru_conv0, lru_h0, swa_k0, swa_v0, p):
    bsz, t, d = x.shape
    x = x.reshape(bsz * t, d)
    ng = p['norm_g']
    p0 = norm_matmul(x, ng[0, 0], p['w_in0'], IN_PROJ_TM, IN0_TN)
    o_gdn, gdn_s, gdn_conv = gdn_mixer(p0, bsz, t, gdn_conv0, gdn_s0, p['gdn_conv_w'], p['gdn_a_log'],
                                       p['gdn_dt_bias'], p['gdn_norm_g'])
    y_lru, lru_h, lru_conv = lru_mixer(p0, bsz, t, lru_conv0, lru_h0, p['lru_conv_w'], p['lru_conv_b'],
                                       p['lru_w_a'], p['lru_b_a'], p['lru_w_x'], p['lru_b_x'], p['lru_lambda'])
    x = outproj_norm_resid(o_gdn, y_lru, p['w_out0'], x, ng[0, 1])
    x = cross_attention(x, bsz, t, mem_k[0], mem_v[0], p['w_xq'][0], p['w_xo'][0], ng[0, 2], ng[0, 3])
    x = dense_ffn(x, ng[0, 4], p['w_ff_gate'], p['w_ff_up'], p['w_ff_down'], ng[0, 5])
    p1 = norm_matmul(x, ng[1, 0], p['w_in1'], IN_PROJ_TM, IN1_TN)
    attn, k_rot = swa_mixer(p1, bsz, t, start, swa_k0, swa_v0, p['swa_sinks'])
    y_smlp, smlp_v = smlp_mixer(p1, bsz, t, p['smlp_ln_g'], p['smlp_ln_b'], p['w_spatial'], p['b_spatial'])
    x = outproj_norm_resid(attn, y_smlp, p['w_out1'], x, ng[1, 1])
    x = cross_attention(x, bsz, t, mem_k[1], mem_v[1], p['w_xq'][1], p['w_xo'][1], ng[1, 2], ng[1, 3])
    k_rows = k_rot.reshape(bsz, t, SWA_KV_W)[:, t - keep:].reshape(bsz, keep, SWA_KV_HEADS, SWA_HD)
    v_rows = p1.reshape(bsz, t, IN1_W)[:, t - keep:, IN1_W - SWA_KV_W:].reshape(bsz, keep, SWA_KV_HEADS, SWA_HD)
    return (x, gdn_conv, gdn_s, lru_conv, lru_h.reshape(bsz, LRU_W), k_rows, v_rows, smlp_v.reshape(bsz, t, SMLP_W))


def _prepare_weights(norm_g, w_in0, gdn_conv_w, gdn_a_log, gdn_dt_bias, gdn_norm_g, lru_conv_w, lru_conv_b,
                     lru_w_a, lru_b_a, lru_w_x, lru_b_x, lru_lambda, w_out0, w_in1, swa_sinks, smlp_ln_g,
                     smlp_ln_b, w_spatial, b_spatial, w_out1, w_xq, w_xo, w_ff_gate, w_ff_up, w_ff_down,
                     w_router, w_moe_gate, w_moe_up, w_moe_down):
    qkvz_w = GDN_QKV_W + GDN_HEADS * GDN_D
    bd_w = 2 * GDN_HEADS
    w0 = jnp.concatenate([w_in0[:, :qkvz_w], w_in0[:, qkvz_w + bd_w:], w_in0[:, qkvz_w:qkvz_w + bd_w],
                          jnp.zeros((D_MODEL, IN0_PAD_W - w_in0.shape[1]), w_in0.dtype)], axis=1)
    qw = SWA_Q_HEADS * SWA_HD
    w1 = jnp.concatenate([w_in1[:, :qw], w_in1[:, qw + 2 * SWA_KV_W:], w_in1[:, qw:qw + 2 * SWA_KV_W]], axis=1)
    return dict(
        norm_g=norm_g, w_in0=w0.astype(BF16), gdn_conv_w=gdn_conv_w, gdn_a_log=gdn_a_log,
        gdn_dt_bias=gdn_dt_bias, gdn_norm_g=gdn_norm_g, lru_conv_w=lru_conv_w, lru_conv_b=lru_conv_b,
        lru_w_a=lru_w_a, lru_b_a=lru_b_a, lru_w_x=lru_w_x, lru_b_x=lru_b_x, lru_lambda=lru_lambda,
        w_out0=w_out0.astype(BF16), w_in1=w1.astype(BF16), swa_sinks=swa_sinks, smlp_ln_g=smlp_ln_g,
        smlp_ln_b=smlp_ln_b, w_spatial=w_spatial, b_spatial=b_spatial, w_out1=w_out1.astype(BF16),
        w_xq=w_xq.astype(BF16), w_xo=w_xo.astype(BF16), w_ff_gate=w_ff_gate.astype(BF16),
        w_ff_up=w_ff_up.astype(BF16), w_ff_down=w_ff_down.astype(BF16), w_router=w_router,
        w_moe_gate=w_moe_gate.astype(BF16), w_moe_up=w_moe_up.astype(BF16), w_moe_down=w_moe_down.astype(BF16))


def kernel(x_prompt, x_sample, mem_prompt, cache_mem_k, cache_mem_v, state_gdn, state_gdn_conv, state_rglru_h, state_rglru_conv, cache_swa_k, cache_swa_v, norm_g, mem_norm_g, w_in0, gdn_conv_w, gdn_a_log, gdn_dt_bias, gdn_norm_g, lru_conv_w, lru_conv_b, lru_w_a, lru_b_a, lru_w_x, lru_b_x, lru_lambda, w_out0, w_in1, swa_sinks, smlp_ln_g, smlp_ln_b, w_spatial, b_spatial, w_out1, w_xq, w_xk, w_xv, w_xo, w_ff_gate, w_ff_up, w_ff_down, w_router, w_moe_gate, w_moe_up, w_moe_down):
    p = _prepare_weights(norm_g, w_in0, gdn_conv_w, gdn_a_log, gdn_dt_bias, gdn_norm_g, lru_conv_w, lru_conv_b,
                         lru_w_a, lru_b_a, lru_w_x, lru_b_x, lru_lambda, w_out0, w_in1, swa_sinks, smlp_ln_g,
                         smlp_ln_b, w_spatial, b_spatial, w_out1, w_xq, w_xo, w_ff_gate, w_ff_up, w_ff_down,
                         w_router, w_moe_gate, w_moe_up, w_moe_down)
    bsz, t, d = x_prompt.shape
    depth = w_xk.shape[0]
    mem_flat = mem_prompt.reshape(bsz * MEM_LEN, d)
    mem_k_p = jnp.stack([norm_matmul(mem_flat, mem_norm_g[l], w_xk[l].astype(BF16), MEM_PROJ_TM, MEM_W)
                         for l in range(depth)]).reshape(depth, bsz, MEM_LEN, MEM_W)
    mem_v_p = jnp.stack([norm_matmul(mem_flat, mem_norm_g[l], w_xv[l].astype(BF16), MEM_PROJ_TM, MEM_W)
                         for l in range(depth)]).reshape(depth, bsz, MEM_LEN, MEM_W)
    keep = min(WINDOW, t)
    (x_p, gdn_conv_p, gdn_s_p, lru_conv_p, lru_h_p, k_rows_p, v_rows_p, _) = _forward(
        x_prompt, 0, keep, mem_k_p, mem_v_p,
        jnp.zeros((bsz, CONV_W - 1, GDN_QKV_W), F32), jnp.zeros((bsz, GDN_HEADS, GDN_D, GDN_D), F32),
        jnp.zeros((bsz, CONV_W - 1, LRU_W), F32), jnp.zeros((bsz, LRU_W), F32),
        jnp.zeros((bsz, WINDOW, SWA_KV_W), F32), jnp.zeros((bsz, WINDOW, SWA_KV_W), F32), p)
    dbs, dec_t = x_sample.shape[:2]
    n_prev = cache_swa_k.shape[1]
    assert n_prev == WINDOW
    (x_s, gdn_conv_s, gdn_s_s, lru_conv_s, lru_h_s, k_rows_s, v_rows_s, smlp_v_s) = _forward(
        x_sample, PAST_LEN, dec_t, cache_mem_k.reshape(depth, dbs, MEM_LEN, MEM_W),
        cache_mem_v.reshape(depth, dbs, MEM_LEN, MEM_W), state_gdn_conv, state_gdn, state_rglru_conv,
        state_rglru_h, cache_swa_k.reshape(dbs, n_prev, SWA_KV_W), cache_swa_v.reshape(dbs, n_prev, SWA_KV_W), p)
    y_p, y_s = moe_block([x_p, x_s], norm_g[1, 4], p['w_router'], p['w_moe_gate'], p['w_moe_up'],
                         p['w_moe_down'], norm_g[1, 5])
    y_p = y_p.reshape(bsz, t, d)
    y_s = y_s.reshape(dbs, dec_t, d)
    shape5 = (depth, bsz, MEM_LEN, MEM_HEADS, MEM_HD)
    return (y_p, y_s, mem_k_p.reshape(shape5), mem_v_p.reshape(shape5), gdn_s_p, gdn_conv_p, lru_h_p, lru_conv_p,
            k_rows_p, v_rows_p, gdn_s_s, gdn_conv_s, lru_h_s, lru_conv_s,
            k_rows_s, v_rows_s, smlp_v_s)
```

```python
import functools
import math

import jax
import jax.numpy as jnp
from jax import lax
from jax.experimental import pallas as pl
from jax.experimental.pallas import tpu as pltpu

F32 = jnp.float32
BF16 = jnp.bfloat16
LOG2_E = math.log2(math.e)

D_MODEL = 2048
EPS = 1e-6
CHUNK = 64
CONV_W = 4
CONV_PAD = 8
GDN_HEADS = 8
GDN_D = 128
GDN_QKV_W = 3 * GDN_HEADS * GDN_D
GDN_CHUNKS_PER_STEP = 2
LRU_W = 1024
LRU_BLOCKS = 8
LRU_BLOCK_W = LRU_W // LRU_BLOCKS
LRU_C = 8.0
IN0_PAD_W = 6400
BD_COL_BLOCK = 6144 // 128
SWA_Q_HEADS = 16
SWA_KV_HEADS = 4
SWA_GROUP = SWA_Q_HEADS // SWA_KV_HEADS
SWA_HD = 64
SWA_KV_W = SWA_KV_HEADS * SWA_HD
WINDOW = 128
SWA_CHUNKS_PER_STEP = 8
ROPE_THETA = 10000.0
PAST_LEN = 4096
SMLP_GROUPS = 8
SMLP_GROUP_W = 128
SMLP_W = SMLP_GROUPS * SMLP_GROUP_W
SMLP_CHUNK = 128
SMLP_CHUNKS_PER_STEP = 4
IN1_W = 3584
MEM_LEN = 256
MEM_HEADS = 4
MEM_HD = 128
MEM_W = MEM_HEADS * MEM_HD
XATTN_SUB_ROWS = 512
FF_DENSE = 5632
N_EXPERTS = 8
FF_EXPERT = 2816
MOE_BM = 512
MOE_TF = FF_EXPERT // 11
LANES = 128
SUBLANES = 8

VMEM_LIMIT_MB = 56

IN_PROJ_TM = 1024
IN0_TN = 1280
IN1_TN = 1792
MEM_PROJ_TM = 512
OUT_PROJ_TM = 512
XATTN_TM = 1024
FFN_TM = 512
FFN_TF = 512
LRU_ROWS = 1024
ROUTER_TM = 512
MOE_TC = 512
IN0_LRU_IN_BLOCK = 4
IN0_LRU_GATE_BLOCK = 5
IN1_U_BLOCK = 1
IN1_VG_BLOCK = 2


def _cparams(semantics, vmem_mb=VMEM_LIMIT_MB):
    return pltpu.CompilerParams(dimension_semantics=semantics, vmem_limit_bytes=vmem_mb * 2 ** 20)


def _rms(x, g):
    return x * lax.rsqrt(jnp.mean(x * x, axis=-1, keepdims=True) + EPS) * g


def _sigmoid(x):
    return 1.0 / (1.0 + jnp.exp2(x * (-LOG2_E)))


def _silu(x):
    return x * _sigmoid(x)


def _softplus(x):
    return jnp.maximum(x, 0.0) + jnp.log(1.0 + jnp.exp(-jnp.abs(x)))


def _gelu(x):
    c = math.sqrt(2.0 / math.pi)
    return 0.5 * x * (1.0 + jnp.tanh(x * (c + (0.044715 * c) * (x * x))))


def _dot(a, b):
    return jnp.dot(a.astype(BF16), b.astype(BF16), preferred_element_type=F32)


def _dot_nt(a, b):
    return lax.dot_general(a.astype(BF16), b.astype(BF16), (((1,), (1,)), ((), ())),
                           preferred_element_type=F32)


def _dot_tn(a, b):
    return lax.dot_general(a.astype(BF16), b.astype(BF16), (((0,), (0,)), ((), ())),
                           preferred_element_type=F32)


def _split3(x):
    x1 = x.astype(BF16)
    r1 = x - x1.astype(F32)
    x2 = r1.astype(BF16)
    x3 = (r1 - x2.astype(F32)).astype(BF16)
    return x1, x2, x3


def _norm_matmul_body(x_ref, g_ref, w_ref, o_ref, xn_ref):
    @pl.when(pl.program_id(1) == 0)
    def _():
        xn_ref[...] = _rms(x_ref[...], g_ref[...]).astype(BF16)

    o_ref[...] = jnp.dot(xn_ref[...], w_ref[...], preferred_element_type=F32)


def norm_matmul(x, g, w, tm, tn):
    n, k = x.shape
    nout = w.shape[1]
    tm = min(tm, n)
    return pl.pallas_call(
        _norm_matmul_body,
        grid=(n // tm, nout // tn),
        in_specs=[pl.BlockSpec((tm, k), lambda i, j: (i, 0)),
                  pl.BlockSpec((1, k), lambda i, j: (0, 0)),
                  pl.BlockSpec((k, tn), lambda i, j: (0, j))],
        out_specs=pl.BlockSpec((tm, tn), lambda i, j: (i, j)),
        out_shape=jax.ShapeDtypeStruct((n, nout), F32),
        scratch_shapes=[pltpu.VMEM((tm, k), BF16)],
        compiler_params=_cparams(("arbitrary", "arbitrary")),
        name="norm_matmul",
    )(x, g.reshape(1, k), w)


def _outproj_body(a_ref, b_ref, wa_ref, wb_ref, r_ref, g_ref, o_ref):
    acc = jnp.dot(a_ref[...].astype(BF16), wa_ref[...], preferred_element_type=F32)
    acc = acc + jnp.dot(b_ref[...].astype(BF16), wb_ref[...], preferred_element_type=F32)
    o_ref[...] = r_ref[...] + _rms(acc, g_ref[...])


def outproj_norm_resid(a, b, w, resid, g, tm=OUT_PROJ_TM):
    n, ka = a.shape
    kb = b.shape[1]
    d = w.shape[1]
    tm = min(tm, n)
    return pl.pallas_call(
        _outproj_body,
        grid=(n // tm,),
        in_specs=[pl.BlockSpec((tm, ka), lambda i: (i, 0)),
                  pl.BlockSpec((tm, kb), lambda i: (i, 0)),
                  pl.BlockSpec((ka, d), lambda i: (0, 0)),
                  pl.BlockSpec((kb, d), lambda i: (1, 0)),
                  pl.BlockSpec((tm, d), lambda i: (i, 0)),
                  pl.BlockSpec((1, d), lambda i: (0, 0))],
        out_specs=pl.BlockSpec((tm, d), lambda i: (i, 0)),
        out_shape=jax.ShapeDtypeStruct((n, d), F32),
        compiler_params=_cparams(("arbitrary",)),
        name="outproj_norm_resid",
    )(a, b, w, w, resid, g.reshape(1, d))


def _gdn_body(nsub, qkv_ref, z_ref, bd_ref, conv0_ref, s0_ref, cw_ref, pvec_ref, ng_ref,
              o_ref, sfin_ref, cfin_ref, s_scr, xbuf):
    c = pl.program_id(1)
    last = pl.num_programs(1) - 1
    lo = CONV_PAD - (CONV_W - 1)
    rows = nsub * CHUNK

    @pl.when(c == 0)
    def _():
        s_scr[...] = s0_ref[0]
        xbuf[lo:CONV_PAD, :] = conv0_ref[0]

    xbuf[CONV_PAD:CONV_PAD + rows, :] = qkv_ref[...]
    y = xbuf[lo:lo + rows, :] * cw_ref[0:1, :]
    for j in range(1, CONV_W):
        y = y + xbuf[lo + j:lo + j + rows, :] * cw_ref[j:j + 1, :]
    tail = xbuf[rows + lo:rows + CONV_PAD, :]
    xbuf[lo:CONV_PAD, :] = tail

    @pl.when(c == last)
    def _():
        cfin_ref[0] = tail

    act = _silu(y)
    bd = bd_ref[...]
    beta = _sigmoid(bd)
    g_all = -jnp.exp(pvec_ref[0:1, :]) * _softplus(bd + pvec_ref[1:2, :])

    row = lax.broadcasted_iota(jnp.int32, (CHUNK, CHUNK), 0)
    col = lax.broadcasted_iota(jnp.int32, (CHUNK, CHUNK), 1)
    causal = row >= col
    strict = row > col
    blk_xor = row ^ col
    rr = lax.broadcasted_iota(jnp.int32, (rows, rows), 0)
    cc = lax.broadcasted_iota(jnp.int32, (rows, rows), 1)
    tri = jnp.logical_and(rr >= cc, ((rr ^ cc) >> (CHUNK.bit_length() - 1)) == 0)
    g_cum3 = jnp.dot(tri.astype(BF16), jnp.concatenate(_split3(g_all), axis=-1), preferred_element_type=F32)
    g_cum = (g_cum3[:, :LANES] + g_cum3[:, LANES:2 * LANES] + g_cum3[:, 2 * LANES:]) * LOG2_E
    g_cum_t = g_cum.T

    heads = range(GDN_HEADS)
    units = [(ci, h) for ci in range(nsub) for h in heads]
    hw = GDN_HEADS * GDN_D

    def rsl(ci):
        return slice(ci * CHUNK, (ci + 1) * CHUNK)

    gc = [g_cum[rsl(ci), GDN_HEADS + h:GDN_HEADS + h + 1] for ci, h in units]
    gr = [g_cum_t[GDN_HEADS + h:GDN_HEADS + h + 1, rsl(ci)] for ci, h in units]
    un = range(len(units))
    decay = [jnp.where(causal, jnp.exp2(gc[u] - gr[u]), 0.0) for u in un]
    bcol = [beta[rsl(ci), h:h + 1] for ci, h in units]
    q = [act[rsl(ci), GDN_D * h:GDN_D * (h + 1)] for ci, h in units]
    k = [act[rsl(ci), hw + GDN_D * h:hw + GDN_D * (h + 1)] for ci, h in units]
    v = [act[rsl(ci), 2 * hw + GDN_D * h:2 * hw + GDN_D * (h + 1)] for ci, h in units]
    q = [x * lax.rsqrt(jnp.sum(x * x, axis=-1, keepdims=True) + EPS) * (GDN_D ** -0.5) for x in q]
    k = [x * lax.rsqrt(jnp.sum(x * x, axis=-1, keepdims=True) + EPS) for x in k]
    kb = [k[u] * bcol[u] for u in un]
    eg = [jnp.exp2(gc[u]) for u in un]
    qa = [_dot_nt(jnp.concatenate([q[u], kb[u]], axis=0), k[u]) for u in un]
    qk = [qa[u][:CHUNK] * decay[u] for u in un]
    a_low = [jnp.where(strict, qa[u][CHUNK:] * decay[u], 0.0) for u in un]
    m = [jnp.where((blk_xor >> 2) == 0, -a_low[u], 0.0) for u in un]
    m2 = [_dot(m[u], m[u]) for u in un]
    n = [m[u] + m2[u] + _dot(m[u], m2[u]) for u in un]
    for lg in range(2, 6):
        low = [jnp.where((blk_xor >> lg) == 1, a_low[u], 0.0) for u in un]
        tl = [low[u] + _dot(n[u], low[u]) for u in un]
        n = [n[u] - (tl[u] + _dot(tl[u], n[u])) for u in un]
    rhs = [jnp.concatenate([v[u] * bcol[u], kb[u] * eg[u]], axis=-1) for u in un]
    sol = [rhs[u] + _dot(n[u], rhs[u]) for u in un]
    wq = [jnp.concatenate([sol[u][:, GDN_D:], q[u] * eg[u]], axis=0) for u in un]
    g_last = [g_cum[(ci + 1) * CHUNK - 1:(ci + 1) * CHUNK, GDN_HEADS + h:GDN_HEADS + h + 1] for ci, h in units]
    k_dec = [k[u] * jnp.exp2(g_last[u] - gc[u]) for u in un]
    s = [s_scr[h] for h in heads]
    for ci in range(nsub):
        us = [ci * GDN_HEADS + h for h in heads]
        ws = [_dot(wq[us[h]], s[h]) for h in heads]
        v_new = [sol[us[h]][:, :GDN_D] - ws[h][:CHUNK] for h in heads]
        o = [ws[h][CHUNK:] + _dot(qk[us[h]], v_new[h]) for h in heads]
        s = [s[h] * jnp.exp2(g_last[us[h]]) + _dot_tn(k_dec[us[h]], v_new[h]) for h in heads]
        for h in heads:
            zh = z_ref[rsl(ci), GDN_D * h:GDN_D * (h + 1)]
            o_ref[rsl(ci), GDN_D * h:GDN_D * (h + 1)] = _rms(o[h], ng_ref[...]) * _silu(zh)
    for h in heads:
        s_scr[h] = s[h]

    @pl.when(c == last)
    def _():
        sfin_ref[0] = s_scr[...]


def gdn_mixer(p0, bsz, t, conv0, s0, conv_w, a_log, dt_bias, norm_g):
    nsub = min(GDN_CHUNKS_PER_STEP, t // CHUNK)
    rows = nsub * CHUNK
    nc = t // rows
    n = bsz * t
    pvec = jnp.zeros((2, LANES), F32)
    pvec = pvec.at[0, GDN_HEADS:2 * GDN_HEADS].set(a_log).at[1, GDN_HEADS:2 * GDN_HEADS].set(dt_bias)
    vw = GDN_HEADS * GDN_D
    return pl.pallas_call(
        functools.partial(_gdn_body, nsub),
        grid=(bsz, nc),
        in_specs=[pl.BlockSpec((rows, GDN_QKV_W), lambda b, c: (b * nc + c, 0)),
                  pl.BlockSpec((rows, vw), lambda b, c: (b * nc + c, GDN_QKV_W // vw)),
                  pl.BlockSpec((rows, LANES), lambda b, c: (b * nc + c, BD_COL_BLOCK)),
                  pl.BlockSpec((1, CONV_W - 1, GDN_QKV_W), lambda b, c: (b, 0, 0)),
                  pl.BlockSpec((1, GDN_HEADS, GDN_D, GDN_D), lambda b, c: (b, 0, 0, 0)),
                  pl.BlockSpec((CONV_W, GDN_QKV_W), lambda b, c: (0, 0)),
                  pl.BlockSpec((2, LANES), lambda b, c: (0, 0)),
                  pl.BlockSpec((1, GDN_D), lambda b, c: (0, 0))],
        out_specs=[pl.BlockSpec((rows, vw), lambda b, c: (b * nc + c, 0)),
                   pl.BlockSpec((1, GDN_HEADS, GDN_D, GDN_D), lambda b, c: (b, 0, 0, 0)),
                   pl.BlockSpec((1, CONV_W - 1, GDN_QKV_W), lambda b, c: (b, 0, 0))],
        out_shape=[jax.ShapeDtypeStruct((n, vw), F32),
                   jax.ShapeDtypeStruct((bsz, GDN_HEADS, GDN_D, GDN_D), F32),
                   jax.ShapeDtypeStruct((bsz, CONV_W - 1, GDN_QKV_W), F32)],
        scratch_shapes=[pltpu.VMEM((GDN_HEADS, GDN_D, GDN_D), F32),
                        pltpu.VMEM((CONV_PAD + rows, GDN_QKV_W), F32)],
        compiler_params=_cparams(("arbitrary", "arbitrary")),
        name="gdn_mixer",
    )(p0, p0, p0, conv0, s0, conv_w, pvec, norm_g.reshape(1, GDN_D))


def _lru_body(tl, x_ref, gate_ref, conv0_ref, h0_ref, cw_ref, cb_ref, wa_ref, ba_ref, wx_ref, bx_ref,
              lam_ref, y_ref, hfin_ref, cfin_ref, h_scr, xbuf, abuf, bbuf):
    c = pl.program_id(1)
    last = pl.num_programs(1) - 1
    lo = CONV_PAD - (CONV_W - 1)
    pad = tl // 2

    @pl.when(c == 0)
    def _():
        h_scr[...] = h0_ref[0]
        xbuf[lo:CONV_PAD, :] = conv0_ref[0]
        abuf[0:pad, :] = jnp.ones((pad, LRU_W), F32)
        bbuf[0:pad, :] = jnp.zeros((pad, LRU_W), F32)

    xbuf[CONV_PAD:CONV_PAD + tl, :] = x_ref[...]
    xr = xbuf[lo:lo + tl, :] * cw_ref[0:1, :]
    for j in range(1, CONV_W):
        xr = xr + xbuf[lo + j:lo + j + tl, :] * cw_ref[j:j + 1, :]
    tail = xbuf[tl + lo:tl + CONV_PAD, :]
    xbuf[lo:CONV_PAD, :] = tail

    @pl.when(c == last)
    def _():
        cfin_ref[0] = tail

    xr = xr + cb_ref[...]
    ga = jnp.concatenate([_dot(xr[:, LRU_BLOCK_W * n:LRU_BLOCK_W * (n + 1)], wa_ref[n])
                          for n in range(LRU_BLOCKS)], axis=-1)
    gx = jnp.concatenate([_dot(xr[:, LRU_BLOCK_W * n:LRU_BLOCK_W * (n + 1)], wx_ref[n])
                          for n in range(LRU_BLOCKS)], axis=-1)
    gate_a = _sigmoid(ga + ba_ref[...])
    gate_x = _sigmoid(gx + bx_ref[...])
    a = jnp.exp2(gate_a * ((-LRU_C * LOG2_E) * _softplus(-lam_ref[...])))
    b = jnp.sqrt(1.0 - a * a) * gate_x * xr
    sub = lax.broadcasted_iota(jnp.int32, (tl, 1), 0) % SUBLANES
    d = 1
    while d < SUBLANES:
        abuf[pad:pad + tl, :] = a
        bbuf[pad:pad + tl, :] = b
        in_group = sub >= d
        a_sh = jnp.where(in_group, abuf[pad - d:pad - d + tl, :], 1.0)
        b_sh = jnp.where(in_group, bbuf[pad - d:pad - d + tl, :], 0.0)
        b = a * b_sh + b
        a = a * a_sh
        d *= 2
    carry = h_scr[...]
    pieces = []
    for r in range(0, tl, SUBLANES):
        h_grp = a[r:r + SUBLANES, :] * carry + b[r:r + SUBLANES, :]
        pieces.append(h_grp)
        carry = h_grp[SUBLANES - 1:SUBLANES, :]
    h = jnp.concatenate(pieces, axis=0)
    h_last = carry
    h_scr[...] = h_last
    y_ref[...] = h * _gelu(gate_ref[...])

    @pl.when(c == last)
    def _():
        hfin_ref[0] = h_last


def lru_mixer(p0, bsz, t, conv0, h0, conv_w, conv_b, w_a, b_a, w_x, b_x, lam):
    tl = min(t, LRU_ROWS)
    nc = t // tl
    n = bsz * t
    row = lambda v: v.reshape(1, LRU_W)
    return pl.pallas_call(
        functools.partial(_lru_body, tl),
        grid=(bsz, nc),
        in_specs=[pl.BlockSpec((tl, LRU_W), lambda b, c: (b * nc + c, IN0_LRU_IN_BLOCK)),
                  pl.BlockSpec((tl, LRU_W), lambda b, c: (b * nc + c, IN0_LRU_GATE_BLOCK)),
                  pl.BlockSpec((1, CONV_W - 1, LRU_W), lambda b, c: (b, 0, 0)),
                  pl.BlockSpec((1, 1, LRU_W), lambda b, c: (b, 0, 0)),
                  pl.BlockSpec((CONV_W, LRU_W), lambda b, c: (0, 0)),
                  pl.BlockSpec((1, LRU_W), lambda b, c: (0, 0)),
                  pl.BlockSpec((LRU_BLOCKS, LRU_BLOCK_W, LRU_BLOCK_W), lambda b, c: (0, 0, 0)),
                  pl.BlockSpec((1, LRU_W), lambda b, c: (0, 0)),
                  pl.BlockSpec((LRU_BLOCKS, LRU_BLOCK_W, LRU_BLOCK_W), lambda b, c: (0, 0, 0)),
                  pl.BlockSpec((1, LRU_W), lambda b, c: (0, 0)),
                  pl.BlockSpec((1, LRU_W), lambda b, c: (0, 0))],
        out_specs=[pl.BlockSpec((tl, LRU_W), lambda b, c: (b * nc + c, 0)),
                   pl.BlockSpec((1, 1, LRU_W), lambda b, c: (b, 0, 0)),
                   pl.BlockSpec((1, CONV_W - 1, LRU_W), lambda b, c: (b, 0, 0))],
        out_shape=[jax.ShapeDtypeStruct((n, LRU_W), F32),
                   jax.ShapeDtypeStruct((bsz, 1, LRU_W), F32),
                   jax.ShapeDtypeStruct((bsz, CONV_W - 1, LRU_W), F32)],
        scratch_shapes=[pltpu.VMEM((1, LRU_W), F32),
                        pltpu.VMEM((CONV_PAD + tl, LRU_W), F32),
                        pltpu.VMEM((tl // 2 + tl, LRU_W), F32),
                        pltpu.VMEM((tl // 2 + tl, LRU_W), F32)],
        compiler_params=_cparams(("arbitrary", "arbitrary")),
        name="lru_mixer",
    )(p0, p0, conv0, h0.reshape(bsz, 1, LRU_W), conv_w, row(conv_b), w_a, row(b_a), w_x, row(b_x), row(lam))


def _xattn_body(x_ref, mk_ref, mv_ref, wq_ref, wo_ref, g_in_ref, g_out_ref, o_ref):
    tm = x_ref.shape[0]
    sub = min(tm, XATTN_SUB_ROWS)
    parts = range(tm // sub)
    x = [x_ref[sub * j:sub * (j + 1), :] for j in parts]
    q = [jnp.dot(_rms(x[j], g_in_ref[...]).astype(BF16), wq_ref[...], preferred_element_type=F32) for j in parts]
    mk = mk_ref[0].astype(BF16)
    mv = mv_ref[0].astype(BF16)
    outs = [[] for _ in parts]
    for h in range(MEM_HEADS):
        sl = slice(MEM_HD * h, MEM_HD * (h + 1))
        s = [_dot_nt(q[j][:, sl], mk[:, sl]) * (MEM_HD ** -0.5) for j in parts]
        m = [jnp.max(s[j], axis=-1, keepdims=True) for j in parts]
        p = [jnp.exp(s[j] - m[j]) for j in parts]
        for j in parts:
            outs[j].append(_dot(p[j], mv[:, sl]) / jnp.sum(p[j], axis=-1, keepdims=True))
    y = [jnp.dot(jnp.concatenate(outs[j], axis=-1).astype(BF16), wo_ref[...], preferred_element_type=F32)
         for j in parts]
    for j in parts:
        o_ref[sub * j:sub * (j + 1), :] = x[j] + _rms(y[j], g_out_ref[...])


def cross_attention(x, bsz, t, mem_k, mem_v, wq, wo, g_in, g_out):
    tm = min(t, XATTN_TM)
    nt = t // tm
    n, d = x.shape
    return pl.pallas_call(
        _xattn_body,
        grid=(bsz, nt),
        in_specs=[pl.BlockSpec((tm, d), lambda b, i: (b * nt + i, 0)),
                  pl.BlockSpec((1, MEM_LEN, MEM_W), lambda b, i: (b, 0, 0)),
                  pl.BlockSpec((1, MEM_LEN, MEM_W), lambda b, i: (b, 0, 0)),
                  pl.BlockSpec((d, MEM_W), lambda b, i: (0, 0)),
                  pl.BlockSpec((MEM_W, d), lambda b, i: (0, 0)),
                  pl.BlockSpec((1, d), lambda b, i: (0, 0)),
                  pl.BlockSpec((1, d), lambda b, i: (0, 0))],
        out_specs=pl.BlockSpec((tm, d), lambda b, i: (b * nt + i, 0)),
        out_shape=jax.ShapeDtypeStruct((n, d), F32),
        compiler_params=_cparams(("arbitrary", "arbitrary")),
        name="cross_attention",
    )(x, mem_k, mem_v, wq, wo, g_in.reshape(1, d), g_out.reshape(1, d))


def _ffn_body(x_ref, g_in_ref, wg_ref, wu_ref, wd_ref, g_out_ref, o_ref, xn_ref, acc_ref):
    f = pl.program_id(1)

    @pl.when(f == 0)
    def _():
        xn_ref[...] = _rms(x_ref[...], g_in_ref[...]).astype(BF16)
        acc_ref[...] = jnp.zeros_like(acc_ref)

    xn = xn_ref[...]
    gate = jnp.dot(xn, wg_ref[...], preferred_element_type=F32)
    up = jnp.dot(xn, wu_ref[...], preferred_element_type=F32)
    acc_ref[...] += jnp.dot((_silu(gate) * up).astype(BF16), wd_ref[...], preferred_element_type=F32)

    @pl.when(f == pl.num_programs(1) - 1)
    def _():
        o_ref[...] = x_ref[...] + _rms(acc_ref[...], g_out_ref[...])


def dense_ffn(x, g_in, wg, wu, wd, g_out, tm=FFN_TM, tf=FFN_TF):
    n, d = x.shape
    ff = wg.shape[1]
    tm = min(tm, n)
    return pl.pallas_call(
        _ffn_body,
        grid=(n // tm, ff // tf),
        in_specs=[pl.BlockSpec((tm, d), lambda i, f: (i, 0)),
                  pl.BlockSpec((1, d), lambda i, f: (0, 0)),
                  pl.BlockSpec((d, tf), lambda i, f: (0, f)),
                  pl.BlockSpec((d, tf), lambda i, f: (0, f)),
                  pl.BlockSpec((tf, d), lambda i, f: (f, 0)),
                  pl.BlockSpec((1, d), lambda i, f: (0, 0))],
        out_specs=pl.BlockSpec((tm, d), lambda i, f: (i, 0)),
        out_shape=jax.ShapeDtypeStruct((n, d), F32),
        scratch_shapes=[pltpu.VMEM((tm, d), BF16), pltpu.VMEM((tm, d), F32)],
        compiler_params=_cparams(("arbitrary", "arbitrary")),
        name="dense_ffn",
    )(x, g_in.reshape(1, d), wg, wu, wd, g_out.reshape(1, d))


def _swa_body(start, nsub, q_ref, kv_ref, cos_ref, sin_ref, kprev_ref, vprev_ref, sink_ref,
              o_ref, krot_ref, kbuf, vbuf):
    c = pl.program_id(1)
    rows_t = nsub * CHUNK

    @pl.when(c == 0)
    def _():
        kbuf[0:WINDOW, :] = kprev_ref[0]
        vbuf[0:WINDOW, :] = vprev_ref[0]

    cos = cos_ref[...]
    sin = sin_ref[...]
    lane = lax.broadcasted_iota(jnp.int32, (rows_t, LANES), 1)
    first_half = (lane % SWA_HD) < (SWA_HD // 2)

    def rope(x):
        outs = []
        for j in range(x.shape[1] // LANES):
            xb = x[:, LANES * j:LANES * (j + 1)]
            fwd = pltpu.roll(xb, LANES - SWA_HD // 2, 1)
            bwd = pltpu.roll(xb, SWA_HD // 2, 1)
            outs.append(xb * cos + jnp.where(first_half, fwd, bwd) * sin)
        return jnp.concatenate(outs, axis=-1)

    q = rope(q_ref[...])
    kv = kv_ref[...]
    k = rope(kv[:, :SWA_KV_W])
    krot_ref[...] = k
    kbuf[WINDOW:WINDOW + rows_t, :] = k
    vbuf[WINDOW:WINDOW + rows_t, :] = kv[:, SWA_KV_W:]

    nk = WINDOW + CHUNK
    rows = SWA_GROUP * CHUNK
    key_off = lax.broadcasted_iota(jnp.int32, (rows, nk), 1) - WINDOW
    row_head = lax.broadcasted_iota(jnp.int32, (rows, 1), 0) // CHUNK
    kvh = range(SWA_KV_HEADS)
    units = [(ci, hk) for ci in range(nsub) for hk in kvh]
    un = range(len(units))
    valid = [start + (c * nsub + ci) * CHUNK + key_off >= 0 for ci in range(nsub)]
    qg = [jnp.concatenate([q[ci * CHUNK:(ci + 1) * CHUNK,
                             SWA_HD * (hk * SWA_GROUP + gi):SWA_HD * (hk * SWA_GROUP + gi + 1)]
                           for gi in range(SWA_GROUP)], axis=0) for ci, hk in units]
    kh = [kbuf[ci * CHUNK:ci * CHUNK + nk, SWA_HD * hk:SWA_HD * (hk + 1)] for ci, hk in units]
    vh = [vbuf[ci * CHUNK:ci * CHUNK + nk, SWA_HD * hk:SWA_HD * (hk + 1)] for ci, hk in units]
    sink_h = []
    for hk in kvh:
        col = jnp.full((rows, 1), sink_ref[hk * SWA_GROUP], F32)
        for gi in range(1, SWA_GROUP):
            col = jnp.where(row_head == gi, sink_ref[hk * SWA_GROUP + gi], col)
        sink_h.append(col)
    sink = [sink_h[hk] for _, hk in units]
    s = [jnp.where(valid[units[u][0]], _dot_nt(qg[u], kh[u]) * (SWA_HD ** -0.5), -jnp.inf) for u in un]
    m = [jnp.maximum(jnp.max(s[u], axis=-1, keepdims=True), sink[u]) for u in un]
    p = [jnp.exp(s[u] - m[u]) for u in un]
    denom = [jnp.sum(p[u], axis=-1, keepdims=True) + jnp.exp(sink[u] - m[u]) for u in un]
    og = [_dot(p[u], vh[u]) / denom[u] for u in un]
    for u, (ci, hk) in enumerate(units):
        for pair in range(SWA_GROUP // 2):
            lo_rows = og[u][CHUNK * 2 * pair:CHUNK * (2 * pair + 1)]
            hi_rows = og[u][CHUNK * (2 * pair + 1):CHUNK * (2 * pair + 2)]
            lane0 = SWA_HD * (hk * SWA_GROUP + 2 * pair)
            o_ref[ci * CHUNK:(ci + 1) * CHUNK, lane0:lane0 + 2 * SWA_HD] = jnp.concatenate(
                [lo_rows, hi_rows], axis=-1)

    k_keep = kbuf[rows_t:rows_t + WINDOW, :]
    v_keep = vbuf[rows_t:rows_t + WINDOW, :]
    kbuf[0:WINDOW, :] = k_keep
    vbuf[0:WINDOW, :] = v_keep


def _rope_tables(start, t):
    half = SWA_HD // 2
    inv_freq = jnp.exp(-math.log(ROPE_THETA) * jnp.arange(half, dtype=F32) / half)
    ang = (start + jnp.arange(t)).astype(F32)[:, None] * inv_freq[None, :]
    cos = jnp.cos(ang)
    sin = jnp.sin(ang)
    return jnp.tile(cos, (1, LANES // half)), jnp.tile(jnp.concatenate([-sin, sin], axis=-1), (1, LANES // SWA_HD))


def swa_mixer(p1, bsz, t, start, k_prev, v_prev, sinks):
    nsub = min(SWA_CHUNKS_PER_STEP, t // CHUNK)
    rows = nsub * CHUNK
    nc = t // rows
    n = bsz * t
    qw = SWA_Q_HEADS * SWA_HD
    cos, sin = _rope_tables(start, t)
    return pl.pallas_call(
        functools.partial(_swa_body, start, nsub),
        grid=(bsz, nc),
        in_specs=[pl.BlockSpec((rows, qw), lambda b, c: (b * nc + c, 0)),
                  pl.BlockSpec((rows, 2 * SWA_KV_W), lambda b, c: (b * nc + c, 3 * qw // (2 * SWA_KV_W))),
                  pl.BlockSpec((rows, LANES), lambda b, c: (c, 0)),
                  pl.BlockSpec((rows, LANES), lambda b, c: (c, 0)),
                  pl.BlockSpec((1, WINDOW, SWA_KV_W), lambda b, c: (b, 0, 0)),
                  pl.BlockSpec((1, WINDOW, SWA_KV_W), lambda b, c: (b, 0, 0)),
                  pl.BlockSpec(memory_space=pltpu.SMEM)],
        out_specs=[pl.BlockSpec((rows, qw), lambda b, c: (b * nc + c, 0)),
                   pl.BlockSpec((rows, SWA_KV_W), lambda b, c: (b * nc + c, 0))],
        out_shape=[jax.ShapeDtypeStruct((n, qw), F32),
                   jax.ShapeDtypeStruct((n, SWA_KV_W), F32)],
        scratch_shapes=[pltpu.VMEM((WINDOW + rows, SWA_KV_W), F32),
                        pltpu.VMEM((WINDOW + rows, SWA_KV_W), F32)],
        compiler_params=_cparams(("arbitrary", "arbitrary")),
        name="swa_mixer",
    )(p1, p1, cos, sin, k_prev, v_prev, sinks)


def _smlp_body(lc, u_ref, v_ref, lg_ref, lb_ref, ws_ref, bs_ref, y_ref, vn_ref):
    v = _gelu(v_ref[...])
    mu = jnp.mean(v, axis=-1, keepdims=True)
    vc = v - mu
    vn = vc * lax.rsqrt(jnp.mean(vc * vc, axis=-1, keepdims=True) + EPS) * lg_ref[...] + lb_ref[...]
    vn_ref[...] = vn
    u = _gelu(u_ref[...])
    row = lax.broadcasted_iota(jnp.int32, (lc, lc), 0)
    col = lax.broadcasted_iota(jnp.int32, (lc, lc), 1)
    for g in range(SMLP_GROUPS):
        sl = slice(SMLP_GROUP_W * g, SMLP_GROUP_W * (g + 1))
        w = jnp.where(row >= col, ws_ref[g, 0:lc, 0:lc], 0.0)
        for c in range(u_ref.shape[0] // lc):
            rows = slice(lc * c, lc * (c + 1))
            s = _dot(w, vn[rows, sl]) + bs_ref[0:lc, g:g + 1]
            y_ref[rows, sl] = u[rows, sl] * s


def smlp_mixer(p1, bsz, t, ln_g, ln_b, w_spatial, b_spatial):
    lc = min(SMLP_CHUNK, t)
    n = bsz * t
    rows = lc * min(SMLP_CHUNKS_PER_STEP, n // lc)
    row = lambda v: v.reshape(1, SMLP_W)
    return pl.pallas_call(
        functools.partial(_smlp_body, lc),
        grid=(n // rows,),
        in_specs=[pl.BlockSpec((rows, SMLP_W), lambda i: (i, IN1_U_BLOCK)),
                  pl.BlockSpec((rows, SMLP_W), lambda i: (i, IN1_VG_BLOCK)),
                  pl.BlockSpec((1, SMLP_W), lambda i: (0, 0)),
                  pl.BlockSpec((1, SMLP_W), lambda i: (0, 0)),
                  pl.BlockSpec((SMLP_GROUPS, SMLP_CHUNK, SMLP_CHUNK), lambda i: (0, 0, 0)),
                  pl.BlockSpec((SMLP_CHUNK, SMLP_GROUPS), lambda i: (0, 0))],
        out_specs=[pl.BlockSpec((rows, SMLP_W), lambda i: (i, 0)),
                   pl.BlockSpec((rows, SMLP_W), lambda i: (i, 0))],
        out_shape=[jax.ShapeDtypeStruct((n, SMLP_W), F32),
                   jax.ShapeDtypeStruct((n, SMLP_W), F32)],
        compiler_params=_cparams(("arbitrary",)),
        name="smlp_mixer",
    )(p1, p1, row(ln_g), row(ln_b), w_spatial, b_spatial.T)


def _router_body(x_ref, g_ref, wr_ref, idx_ref, gate_ref):
    hn = _rms(x_ref[...], g_ref[...])
    wr = wr_ref[...]
    h1 = hn.astype(BF16)
    h2 = (hn - h1.astype(F32)).astype(BF16)
    w1 = wr.astype(BF16)
    w2 = (wr - w1.astype(F32)).astype(BF16)
    nt = (((1,), (1,)), ((), ()))
    lg = lax.dot_general(jnp.concatenate([w1, w2], axis=0), h1, nt, preferred_element_type=F32)
    logits = (lg[:N_EXPERTS] + lg[N_EXPERTS:]
              + lax.dot_general(w1, h2, nt, preferred_element_type=F32))
    e_iota = lax.broadcasted_iota(jnp.int32, logits.shape, 0)
    m1 = jnp.max(logits, axis=0, keepdims=True)
    i1 = jnp.min(jnp.where(logits == m1, e_iota, N_EXPERTS), axis=0, keepdims=True)
    rest = jnp.where(e_iota == i1, -jnp.inf, logits)
    m2 = jnp.max(rest, axis=0, keepdims=True)
    i2 = jnp.min(jnp.where(rest == m2, e_iota, N_EXPERTS), axis=0, keepdims=True)
    e2 = jnp.exp(m2 - m1)
    den = 1.0 + e2
    idx_ref[...] = jnp.concatenate([i1, i2], axis=0)
    tm = logits.shape[1]
    gates = jnp.concatenate([1.0 / den, e2 / den, jnp.zeros((LANES - 2, tm), F32)], axis=0)
    gate_ref[...] = gates.T


def moe_router(x, g, w_router, tm=ROUTER_TM):
    n, d = x.shape
    tm = min(tm, n)
    return pl.pallas_call(
        _router_body,
        grid=(n // tm,),
        in_specs=[pl.BlockSpec((tm, d), lambda i: (i, 0)),
                  pl.BlockSpec((1, d), lambda i: (0, 0)),
                  pl.BlockSpec((N_EXPERTS, d), lambda i: (0, 0))],
        out_specs=[pl.BlockSpec((2, tm), lambda i: (0, i)),
                   pl.BlockSpec((tm, LANES), lambda i: (i, 0))],
        out_shape=[jax.ShapeDtypeStruct((2, n), jnp.int32),
                   jax.ShapeDtypeStruct((n, LANES), F32)],
        compiler_params=_cparams(("arbitrary",)),
        name="moe_router",
    )(x, g.reshape(1, d), w_router.T)


def _moe_block_rows(n_tokens):
    return MOE_BM if 2 * n_tokens // N_EXPERTS >= 4 * MOE_BM else MOE_BM // 2


def _moe_plan(top_idx, bm):
    n = top_idx.shape[1]
    flat_e = top_idx.reshape(-1)
    onehot = (flat_e[:, None] == jnp.arange(N_EXPERTS, dtype=jnp.int32)[None, :]).astype(jnp.int32)
    rank = jnp.sum(jnp.cumsum(onehot, axis=0) * onehot, axis=1) - 1
    counts = jnp.sum(onehot, axis=0)
    padded = (counts + bm - 1) // bm * bm
    pad_end = jnp.cumsum(padded)
    pad_start = pad_end - padded
    dest = jnp.sum(onehot * pad_start[None, :], axis=1) + rank
    n_blk = -(-2 * n // bm) + N_EXPERTS
    blk_start = jnp.arange(n_blk, dtype=jnp.int32) * bm
    blk_e = jnp.minimum(jnp.sum((blk_start[:, None] >= pad_end[None, :]).astype(jnp.int32), axis=1),
                        N_EXPERTS - 1)
    blk_valid = jnp.clip((pad_start + counts)[blk_e] - blk_start, 0, bm).astype(jnp.int32)
    n_active = (pad_end[-1] // bm).astype(jnp.int32).reshape(1)
    return dest.reshape(2, n).astype(jnp.int32), blk_e.astype(jnp.int32), blk_valid, n_active, n_blk


def _moe_dispatch_body(tc, dest_ref, x_ref, g_ref, xs_hbm, hn_scr, sem):
    d = x_ref.shape[1]
    hn_scr[...] = _rms(x_ref[...], g_ref[...]).reshape(tc // SUBLANES, SUBLANES, d)

    def start(i8, carry):
        for j in range(SUBLANES):
            for slot in range(2):
                dst_row = dest_ref[0, slot, i8 * SUBLANES + j]
                pltpu.make_async_copy(hn_scr.at[i8, pl.ds(j, 1), :], xs_hbm.at[pl.ds(dst_row, 1), :],
                                      sem).start(priority=slot)
        return carry

    lax.fori_loop(0, tc // SUBLANES, start, 0)
    for _ in range(2):
        pltpu.make_async_copy(x_ref, xs_hbm.at[pl.ds(0, tc), :], sem).wait()


def _moe_dispatch_into_body(tc, dest_ref, x_ref, g_ref, xs_prev_hbm, xs_hbm, hn_scr, sem):
    del xs_prev_hbm
    _moe_dispatch_body(tc, dest_ref, x_ref, g_ref, xs_hbm, hn_scr, sem)


def moe_dispatch(x, g, dest_blocks, n_rows, tc, xs_prev=None):
    n, d = x.shape
    in_specs = [pl.BlockSpec((1, 2, tc), lambda i: (i, 0, 0), memory_space=pltpu.SMEM),
                pl.BlockSpec((tc, d), lambda i: (i, 0)),
                pl.BlockSpec((1, d), lambda i: (0, 0))]
    args = [dest_blocks, x, g.reshape(1, d)]
    body, aliases = _moe_dispatch_body, {}
    if xs_prev is not None:
        in_specs.append(pl.BlockSpec(memory_space=pl.ANY))
        args.append(xs_prev)
        body, aliases = _moe_dispatch_into_body, {3: 0}
    return pl.pallas_call(
        functools.partial(body, tc),
        grid=(n // tc,),
        in_specs=in_specs,
        out_specs=pl.BlockSpec(memory_space=pl.ANY),
        out_shape=jax.ShapeDtypeStruct((n_rows, d), F32),
        scratch_shapes=[pltpu.VMEM((tc // SUBLANES, SUBLANES, d), F32), pltpu.SemaphoreType.DMA(())],
        input_output_aliases=aliases,
        compiler_params=_cparams(("arbitrary",)),
        name="moe_dispatch",
    )(*args)


def _moe_ffn_body(blk_e_ref, blk_valid_ref, nact_ref, xs_ref, wg_hbm, wu_hbm, wd_hbm, ys_ref,
                  wg_scr, wu_scr, wd_scr, sem):
    i = pl.program_id(0)
    active = i < nact_ref[0]
    e = blk_e_ref[i]
    new_expert = jnp.logical_or(i == 0, e != blk_e_ref[jnp.maximum(i - 1, 0)])

    @pl.when(jnp.logical_and(active, new_expert))
    def _():
        copies = [pltpu.make_async_copy(w_hbm.at[e], w_scr, sem.at[j])
                  for j, (w_hbm, w_scr) in enumerate(((wg_hbm, wg_scr), (wu_hbm, wu_scr), (wd_hbm, wd_scr)))]
        for cp in copies:
            cp.start()
        for cp in copies:
            cp.wait()

    @pl.when(active)
    def _():
        row = lax.broadcasted_iota(jnp.int32, (xs_ref.shape[0], 1), 0)
        xb = jnp.where(row < blk_valid_ref[i], xs_ref[...], 0.0).astype(BF16)
        acc = None
        for j in range(FF_EXPERT // MOE_TF):
            sl = slice(MOE_TF * j, MOE_TF * (j + 1))
            gate = jnp.dot(xb, wg_scr[:, sl], preferred_element_type=F32)
            up = jnp.dot(xb, wu_scr[:, sl], preferred_element_type=F32)
            part = jnp.dot((_silu(gate) * up).astype(BF16), wd_scr[sl, :], preferred_element_type=F32)
            acc = part if acc is None else acc + part
        ys_ref[...] = acc

    @pl.when(jnp.logical_not(active))
    def _():
        ys_ref[...] = jnp.zeros_like(ys_ref)


def moe_expert_ffn(xs, blk_e, blk_valid, n_active, wg, wu, wd, bm):
    n_rows, d = xs.shape
    n_blk = n_rows // bm
    return pl.pallas_call(
        _moe_ffn_body,
        grid_spec=pltpu.PrefetchScalarGridSpec(
            num_scalar_prefetch=3,
            grid=(n_blk,),
            in_specs=[pl.BlockSpec((bm, d), lambda i, be, bv, na: (jnp.minimum(i, na[0] - 1), 0)),
                      pl.BlockSpec(memory_space=pl.ANY),
                      pl.BlockSpec(memory_space=pl.ANY),
                      pl.BlockSpec(memory_space=pl.ANY)],
            out_specs=pl.BlockSpec((bm, d), lambda i, be, bv, na: (i, 0)),
            scratch_shapes=[pltpu.VMEM((d, FF_EXPERT), BF16), pltpu.VMEM((d, FF_EXPERT), BF16),
                            pltpu.VMEM((FF_EXPERT, d), BF16), pltpu.SemaphoreType.DMA((3,))]),
        out_shape=jax.ShapeDtypeStruct((n_rows, d), F32),
        compiler_params=_cparams(("arbitrary",)),
        name="moe_expert_ffn",
    )(blk_e, blk_valid, n_active, xs, wg, wu, wd)


def _moe_combine_body(tc, pos_ref, pos_next_ref, ys_hbm, gate_ref, x_ref, g_ref, o_ref, buf, sem):
    i = pl.program_id(0)
    nb = pl.num_programs(0)
    slot = i % 2

    def issue(p_ref, s):
        def start(i8, carry):
            for j in range(SUBLANES):
                for choice in range(2):
                    src_row = p_ref[0, choice, i8 * SUBLANES + j]
                    pltpu.make_async_copy(ys_hbm.at[pl.ds(src_row, 1), :], buf.at[s, choice, i8, pl.ds(j, 1), :],
                                          sem.at[s]).start(priority=choice)
            return carry

        lax.fori_loop(0, tc // SUBLANES, start, 0)

    @pl.when(i == 0)
    def _():
        issue(pos_ref, 0)

    @pl.when(i + 1 < nb)
    def _():
        issue(pos_next_ref, 1 - slot)

    d = x_ref.shape[1]
    for choice in range(2):
        pltpu.make_async_copy(ys_hbm.at[pl.ds(0, tc), :], o_ref, sem.at[slot]).wait()
    gates = gate_ref[...]
    y = (gates[:, 0:1] * buf[slot, 0].reshape(tc, d) + gates[:, 1:2] * buf[slot, 1].reshape(tc, d))
    o_ref[...] = x_ref[...] + _rms(y, g_ref[...])


def moe_combine(ys, pos_blocks, gates, x, g, tc):
    n, d = x.shape
    nb = n // tc
    return pl.pallas_call(
        functools.partial(_moe_combine_body, tc),
        grid=(nb,),
        in_specs=[pl.BlockSpec((1, 2, tc), lambda i: (i, 0, 0), memory_space=pltpu.SMEM),
                  pl.BlockSpec((1, 2, tc), lambda i: (jnp.minimum(i + 1, nb - 1), 0, 0), memory_space=pltpu.SMEM),
                  pl.BlockSpec(memory_space=pl.ANY),
                  pl.BlockSpec((tc, LANES), lambda i: (i, 0)),
                  pl.BlockSpec((tc, d), lambda i: (i, 0)),
                  pl.BlockSpec((1, d), lambda i: (0, 0))],
        out_specs=pl.BlockSpec((tc, d), lambda i: (i, 0)),
        out_shape=jax.ShapeDtypeStruct((n, d), F32),
        scratch_shapes=[pltpu.VMEM((2, 2, tc // SUBLANES, SUBLANES, d), F32), pltpu.SemaphoreType.DMA((2,))],
        compiler_params=_cparams(("arbitrary",)),
        name="moe_combine",
    )(pos_blocks, pos_blocks, ys, gates, x, g.reshape(1, d))


def moe_block(xs_in, g_in, w_router, wg, wu, wd, g_out, tc=MOE_TC):
    routed = [moe_router(x, g_in, w_router) for x in xs_in]
    sizes = [x.shape[0] for x in xs_in]
    n_all = sum(sizes)
    bm = _moe_block_rows(n_all)
    dest, blk_e, blk_valid, n_active, n_blk = _moe_plan(jnp.concatenate([r[0] for r in routed], axis=1), bm)
    xs, dest_blocks, off = None, [], 0
    for x, n in zip(xs_in, sizes):
        t = min(tc, n)
        dest_blocks.append(dest[:, off:off + n].reshape(2, n // t, t).transpose(1, 0, 2))
        xs = moe_dispatch(x, g_in, dest_blocks[-1], n_blk * bm, t, xs_prev=xs)
        off += n
    ys = moe_expert_ffn(xs, blk_e, blk_valid, n_active, wg, wu, wd, bm)
    return [moe_combine(ys, db, r[1], x, g_out, min(tc, x.shape[0]))
            for x, db, r in zip(xs_in, dest_blocks, routed)]


def _forward(x, start, keep, mem_k, mem_v, gdn_conv0, gdn_s0, lru_conv0, lru_h0, swa_k0, swa_v0, p):
    bsz, t, d = x.shape
    x = x.reshape(bsz * t, d)
    ng = p['norm_g']
    p0 = norm_matmul(x, ng[0, 0], p['w_in0'], IN_PROJ_TM, IN0_TN)
    o_gdn, gdn_s, gdn_conv = gdn_mixer(p0, bsz, t, gdn_conv0, gdn_s0, p['gdn_conv_w'], p['gdn_a_log'],
                                       p['gdn_dt_bias'], p['gdn_norm_g'])
    y_lru, lru_h, lru_conv = lru_mixer(p0, bsz, t, lru_conv0, lru_h0, p['lru_conv_w'], p['lru_conv_b'],
                                       p['lru_w_a'], p['lru_b_a'], p['lru_w_x'], p['lru_b_x'], p['lru_lambda'])
    x = outproj_norm_resid(o_gdn, y_lru, p['w_out0'], x, ng[0, 1])
    x = cross_attention(x, bsz, t, mem_k[0], mem_v[0], p['w_xq'][0], p['w_xo'][0], ng[0, 2], ng[0, 3])
    x = dense_ffn(x, ng[0, 4], p['w_ff_gate'], p['w_ff_up'], p['w_ff_down'], ng[0, 5])
    p1 = norm_matmul(x, ng[1, 0], p['w_in1'], IN_PROJ_TM, IN1_TN)
    attn, k_rot = swa_mixer(p1, bsz, t, start, swa_k0, swa_v0, p['swa_sinks'])
    y_smlp, smlp_v = smlp_mixer(p1, bsz, t, p['smlp_ln_g'], p['smlp_ln_b'], p['w_spatial'], p['b_spatial'])
    x = outproj_norm_resid(attn, y_smlp, p['w_out1'], x, ng[1, 1])
    x = cross_attention(x, bsz, t, mem_k[1], mem_v[1], p['w_xq'][1], p['w_xo'][1], ng[1, 2], ng[1, 3])
    k_rows = k_rot.reshape(bsz, t, SWA_KV_W)[:, t - keep:].reshape(bsz, keep, SWA_KV_HEADS, SWA_HD)
    v_rows = p1.reshape(bsz, t, IN1_W)[:, t - keep:, IN1_W - SWA_KV_W:].reshape(bsz, keep, SWA_KV_HEADS, SWA_HD)
    return (x, gdn_conv, gdn_s, lru_conv, lru_h.reshape(bsz, LRU_W), k_rows, v_rows, smlp_v.reshape(bsz, t, SMLP_W))


def _prepare_weights(norm_g, w_in0, gdn_conv_w, gdn_a_log, gdn_dt_bias, gdn_norm_g, lru_conv_w, lru_conv_b,
                     lru_w_a, lru_b_a, lru_w_x, lru_b_x, lru_lambda, w_out0, w_in1, swa_sinks, smlp_ln_g,
                     smlp_ln_b, w_spatial, b_spatial, w_out1, w_xq, w_xo, w_ff_gate, w_ff_up, w_ff_down,
                     w_router, w_moe_gate, w_moe_up, w_moe_down):
    qkvz_w = GDN_QKV_W + GDN_HEADS * GDN_D
    bd_w = 2 * GDN_HEADS
    w0 = jnp.concatenate([w_in0[:, :qkvz_w], w_in0[:, qkvz_w + bd_w:], w_in0[:, qkvz_w:qkvz_w + bd_w],
                          jnp.zeros((D_MODEL, IN0_PAD_W - w_in0.shape[1]), w_in0.dtype)], axis=1)
    qw = SWA_Q_HEADS * SWA_HD
    w1 = jnp.concatenate([w_in1[:, :qw], w_in1[:, qw + 2 * SWA_KV_W:], w_in1[:, qw:qw + 2 * SWA_KV_W]], axis=1)
    return dict(
        norm_g=norm_g, w_in0=w0.astype(BF16), gdn_conv_w=gdn_conv_w, gdn_a_log=gdn_a_log,
        gdn_dt_bias=gdn_dt_bias, gdn_norm_g=gdn_norm_g, lru_conv_w=lru_conv_w, lru_conv_b=lru_conv_b,
        lru_w_a=lru_w_a, lru_b_a=lru_b_a, lru_w_x=lru_w_x, lru_b_x=lru_b_x, lru_lambda=lru_lambda,
        w_out0=w_out0.astype(BF16), w_in1=w1.astype(BF16), swa_sinks=swa_sinks, smlp_ln_g=smlp_ln_g,
        smlp_ln_b=smlp_ln_b, w_spatial=w_spatial, b_spatial=b_spatial, w_out1=w_out1.astype(BF16),
        w_xq=w_xq.astype(BF16), w_xo=w_xo.astype(BF16), w_ff_gate=w_ff_gate.astype(BF16),
        w_ff_up=w_ff_up.astype(BF16), w_ff_down=w_ff_down.astype(BF16), w_router=w_router,
        w_moe_gate=w_moe_gate.astype(BF16), w_moe_up=w_moe_up.astype(BF16), w_moe_down=w_moe_down.astype(BF16))


def kernel(x_prompt, x_sample, mem_prompt, cache_mem_k, cache_mem_v, state_gdn, state_gdn_conv, state_rglru_h, state_rglru_conv, cache_swa_k, cache_swa_v, norm_g, mem_norm_g, w_in0, gdn_conv_w, gdn_a_log, gdn_dt_bias, gdn_norm_g, lru_conv_w, lru_conv_b, lru_w_a, lru_b_a, lru_w_x, lru_b_x, lru_lambda, w_out0, w_in1, swa_sinks, smlp_ln_g, smlp_ln_b, w_spatial, b_spatial, w_out1, w_xq, w_xk, w_xv, w_xo, w_ff_gate, w_ff_up, w_ff_down, w_router, w_moe_gate, w_moe_up, w_moe_down):
    p = _prepare_weights(norm_g, w_in0, gdn_conv_w, gdn_a_log, gdn_dt_bias, gdn_norm_g, lru_conv_w, lru_conv_b,
                         lru_w_a, lru_b_a, lru_w_x, lru_b_x, lru_lambda, w_out0, w_in1, swa_sinks, smlp_ln_g,
                         smlp_ln_b, w_spatial, b_spatial, w_out1, w_xq, w_xo, w_ff_gate, w_ff_up, w_ff_down,
                         w_router, w_moe_gate, w_moe_up, w_moe_down)
    bsz, t, d = x_prompt.shape
    depth = w_xk.shape[0]
    mem_flat = mem_prompt.reshape(bsz * MEM_LEN, d)
    mem_k_p = jnp.stack([norm_matmul(mem_flat, mem_norm_g[l], w_xk[l].astype(BF16), MEM_PROJ_TM, MEM_W)
                         for l in range(depth)]).reshape(depth, bsz, MEM_LEN, MEM_W)
    mem_v_p = jnp.stack([norm_matmul(mem_flat, mem_norm_g[l], w_xv[l].astype(BF16), MEM_PROJ_TM, MEM_W)
                         for l in range(depth)]).reshape(depth, bsz, MEM_LEN, MEM_W)
    keep = min(WINDOW, t)
    (x_p, gdn_conv_p, gdn_s_p, lru_conv_p, lru_h_p, k_rows_p, v_rows_p, _) = _forward(
        x_prompt, 0, keep, mem_k_p, mem_v_p,
        jnp.zeros((bsz, CONV_W - 1, GDN_QKV_W), F32), jnp.zeros((bsz, GDN_HEADS, GDN_D, GDN_D), F32),
        jnp.zeros((bsz, CONV_W - 1, LRU_W), F32), jnp.zeros((bsz, LRU_W), F32),
        jnp.zeros((bsz, WINDOW, SWA_KV_W), F32), jnp.zeros((bsz, WINDOW, SWA_KV_W), F32), p)
    dbs, dec_t = x_sample.shape[:2]
    n_prev = cache_swa_k.shape[1]
    assert n_prev == WINDOW
    (x_s, gdn_conv_s, gdn_s_s, lru_conv_s, lru_h_s, k_rows_s, v_rows_s, smlp_v_s) = _forward(
        x_sample, PAST_LEN, dec_t, cache_mem_k.reshape(depth, dbs, MEM_LEN, MEM_W),
        cache_mem_v.reshape(depth, dbs, MEM_LEN, MEM_W), state_gdn_conv, state_gdn, state_rglru_conv,
        state_rglru_h, cache_swa_k.reshape(dbs, n_prev, SWA_KV_W), cache_swa_v.reshape(dbs, n_prev, SWA_KV_W), p)
    y_p, y_s = moe_block([x_p, x_s], norm_g[1, 4], p['w_router'], p['w_moe_gate'], p['w_moe_up'],
                         p['w_moe_down'], norm_g[1, 5])
    y_p = y_p.reshape(bsz, t, d)
    y_s = y_s.reshape(dbs, dec_t, d)
    shape5 = (depth, bsz, MEM_LEN, MEM_HEADS, MEM_HD)
    return (y_p, y_s, mem_k_p.reshape(shape5), mem_v_p.reshape(shape5), gdn_s_p, gdn_conv_p, lru_h_p, lru_conv_p,
            k_rows_p, v_rows_p, gdn_s_s, gdn_conv_s, lru_h_s, lru_conv_s,
            k_rows_s, v_rows_s, smlp_v_s)
```
